```python
import math
import jax, jax.numpy as jnp
from jax import lax
import numpy as np

D_MODEL = 1024
BATCH = 8
SEQ = 4096
DEPTH = 4

CTX_LEN = 256
GRID_W = 64
A_WIDTH = 512
A_GROUPS = 4
CHUNK = 128
B_HEADS = 8
B_QK_DIM = 64
B_V_DIM = 2 * B_QK_DIM
B_WIDTH = B_HEADS * B_V_DIM
QK_COLS = 2 * B_HEADS * B_QK_DIM
Q_BLOCK = 128
ROPE_THETA = 10000.0
C_WIDTH = 512
CONV_W = 3
N_BRANCH = 3
D_FF = 2816
EPS = 1e-6
A_U = 0
A_V = A_U + A_WIDTH
B_Q = A_V + A_WIDTH
B_K = B_Q + QK_COLS
B_V = B_K + QK_COLS
C_IN = B_V + B_WIDTH
GATE = C_IN + 3 * C_WIDTH
IN_COLS = GATE + N_BRANCH * D_MODEL

kernel_name = 'hybrid_sgu_diffattn_shortconv_dit'


def rmsnorm(x, g):
    xf = x.astype(jnp.float32)
    y = xf * lax.rsqrt(jnp.mean(xf * xf, axis=-1, keepdims=True) + EPS)
    return (y * g.astype(jnp.float32)).astype(x.dtype)


def layernorm(x, g):
    xf = x.astype(jnp.float32)
    mu = jnp.mean(xf, axis=-1, keepdims=True)
    var = jnp.mean(jnp.square(xf - mu), axis=-1, keepdims=True)
    return ((xf - mu) * lax.rsqrt(var + EPS) * g.astype(jnp.float32)).astype(x.dtype)


def modulate(h, shift, scale):
    return h * (1.0 + scale) + shift


def dwconv3(x, w):
    xp = jnp.pad(x, ((0, 0), (1, 1), (0, 0)))
    return xp[:, :-2] * w[0] + xp[:, 1:-1] * w[1] + xp[:, 2:] * w[2]


def axial_rope_tables(n_tokens):
    rows = n_tokens // GRID_W
    row = jnp.repeat(jnp.arange(rows, dtype=jnp.float32), GRID_W)
    col = jnp.tile(jnp.arange(GRID_W, dtype=jnp.float32), rows)
    n_freq = B_QK_DIM // 4
    inv = ROPE_THETA ** (-jnp.arange(n_freq, dtype=jnp.float32) / n_freq)
    ang = jnp.stack([row[:, None] * inv, col[:, None] * inv], axis=1)
    ang = ang[None, :, None, None]
    return jnp.cos(ang), jnp.sin(ang)


def apply_rope(t, cos, sin):
    tr = t.astype(jnp.float32).reshape(*t.shape[:-1], 2, 2, B_QK_DIM // 4)
    x1, x2 = tr[..., 0, :], tr[..., 1, :]
    out = jnp.stack([x1 * cos - x2 * sin, x1 * sin + x2 * cos], axis=-2)
    return out.reshape(t.shape).astype(t.dtype)


def qk_heads(t, g):
    bn, tn, _ = t.shape
    return rmsnorm(t.reshape(bn, tn, 2, B_HEADS, B_QK_DIM), g)


def v_heads(t):
    bn, tn, _ = t.shape
    return t.reshape(bn, tn, B_HEADS, B_V_DIM)


def diff_attention(q, k, v, lam):
    s = jnp.einsum('bqahd,bkahd->bahqk', q, k).astype(jnp.float32) * (B_QK_DIM ** -0.5)
    p = jax.nn.softmax(s, axis=-1)
    a = (p[:, 0] - lam * p[:, 1]).astype(v.dtype)
    return jnp.einsum('bhqk,bkhe->bqhe', a, v)


def blocked_diff_attention(q, k, v, lam):
    bn, tn = q.shape[:2]
    nblk = tn // Q_BLOCK
    qb = jnp.moveaxis(q.reshape(bn, nblk, Q_BLOCK, *q.shape[2:]), 1, 0)
    ob = lax.map(lambda qi: diff_attention(qi, k, v, lam), qb)
    return jnp.moveaxis(ob, 0, 1).reshape(bn, tn, B_HEADS, B_V_DIM)


def spatial_gating(u, v, ln_g, w_s, b_s):
    u = jax.nn.gelu(u)
    v = layernorm(jax.nn.gelu(v), ln_g)
    bn, tn, _ = v.shape
    vc = v.reshape(bn, tn // CHUNK, CHUNK, A_GROUPS, A_WIDTH // A_GROUPS)
    mixed = jnp.einsum('gpq,bnqgc->bnpgc', w_s, vc) + b_s.T[:, :, None]
    return u * mixed.reshape(bn, tn, A_WIDTH)


def mixer_merge(z, attn, lam_init, ln_g, w_s, b_s, cw, sub_g, wa, wb, wc, wo):
    bn, tn, _ = z.shape
    y_a = spatial_gating(z[..., A_U:A_V], z[..., A_V:B_Q], ln_g, w_s, b_s)
    y_b = (rmsnorm(attn, sub_g) * (1.0 - lam_init)).reshape(bn, tn, B_WIDTH)
    gb, gc, xin = jnp.split(z[..., C_IN:GATE], 3, axis=-1)
    y_c = gb * dwconv3(gc * xin, cw)
    g_a, g_b, g_c = jnp.split(jax.nn.sigmoid(z[..., GATE:]), N_BRANCH, axis=-1)
    m = g_a * (y_a @ wa) + g_b * (y_b @ wb) + g_c * (y_c @ wc)
    return m @ wo


def conv_ffn(h, w_up, cw, cb, w_down):
    u = dwconv3(h @ w_up, cw) + cb
    g, val = jnp.split(u, 2, axis=-1)
    return (jax.nn.silu(g) * val) @ w_down


def setup_inputs(seed: int = 0) -> dict:
    key = jax.random.key(seed)
    ks = jax.random.split(key, 32)
    D = D_MODEL

    def nrm(k, shape, scale):
        return jax.random.normal(k, shape, jnp.float32) * scale

    return {
        'x': nrm(ks[0], (BATCH, SEQ, D), 1.0),
        'c': nrm(ks[1], (BATCH, D), 1.0),
        'ctx': nrm(ks[2], (BATCH, CTX_LEN, D), 1.0),
        'c_ctx': nrm(ks[3], (D,), 1.0),
        'ada_w': nrm(ks[4], (DEPTH, D, 6 * D), 0.5 * D ** -0.5),
        'ada_b': nrm(ks[5], (DEPTH, 6 * D), 0.02),
        'norm1_g': 1.0 + nrm(ks[6], (DEPTH, D), 0.02),
        'norm2_g': 1.0 + nrm(ks[7], (DEPTH, D), 0.02),
        'w_in': nrm(ks[8], (DEPTH, D, IN_COLS), D ** -0.5),
        'sgu_ln_g': 1.0 + nrm(ks[9], (DEPTH, A_WIDTH), 0.02),
        'sgu_w': nrm(ks[10], (DEPTH, A_GROUPS, CHUNK, CHUNK), CHUNK ** -0.5),
        'sgu_b': 1.0 + nrm(ks[11], (DEPTH, A_GROUPS, CHUNK), 0.02),
        'q_norm_g': 1.0 + nrm(ks[12], (DEPTH, B_QK_DIM), 0.02),
        'k_norm_g': 1.0 + nrm(ks[13], (DEPTH, B_QK_DIM), 0.02),
        'lam_q1': nrm(ks[14], (DEPTH, B_QK_DIM), 0.1),
        'lam_k1': nrm(ks[15], (DEPTH, B_QK_DIM), 0.1),
        'lam_q2': nrm(ks[16], (DEPTH, B_QK_DIM), 0.1),
        'lam_k2': nrm(ks[17], (DEPTH, B_QK_DIM), 0.1),
        'subln_g': 1.0 + nrm(ks[18], (DEPTH, B_V_DIM), 0.02),
        'conv_w': nrm(ks[19], (DEPTH, CONV_W, C_WIDTH), CONV_W ** -0.5),
        'w_br_a': nrm(ks[20], (DEPTH, A_WIDTH, D), A_WIDTH ** -0.5),
        'w_br_b': nrm(ks[21], (DEPTH, B_WIDTH, D), B_WIDTH ** -0.5),
        'w_br_c': nrm(ks[22], (DEPTH, C_WIDTH, D), C_WIDTH ** -0.5),
        'w_out': nrm(ks[23], (DEPTH, D, D), D ** -0.5),
        'ffn_up': nrm(ks[24], (DEPTH, D, 2 * D_FF), D ** -0.5),
        'ffn_conv_w': nrm(ks[25], (DEPTH, CONV_W, 2 * D_FF), CONV_W ** -0.5),
        'ffn_conv_b': nrm(ks[26], (DEPTH, 2 * D_FF), 0.02),
        'ffn_down': nrm(ks[27], (DEPTH, D_FF, D), D_FF ** -0.5),
    }


def reference(x, c, ctx, c_ctx, ada_w, ada_b, norm1_g, norm2_g, w_in,
              sgu_ln_g, sgu_w, sgu_b, q_norm_g, k_norm_g,
              lam_q1, lam_k1, lam_q2, lam_k2, subln_g, conv_w,
              w_br_a, w_br_b, w_br_c, w_out,
              ffn_up, ffn_conv_w, ffn_conv_b, ffn_down):
    n_lat = x.shape[1]
    cos, sin = axial_rope_tables(n_lat)
    silu_c = jax.nn.silu(c)
    silu_cc = jax.nn.silu(c_ctx)
    f32 = jnp.float32
    for l in range(DEPTH):
        last = l == DEPTH - 1
        lam_init = 0.8 - 0.6 * math.exp(-0.3 * l)
        lam = (jnp.exp(jnp.sum(lam_q1[l].astype(f32) * lam_k1[l].astype(f32)))
               - jnp.exp(jnp.sum(lam_q2[l].astype(f32) * lam_k2[l].astype(f32))) + lam_init)
        mod_lat = (silu_c @ ada_w[l] + ada_b[l])[:, None, :]
        mod_ctx = silu_cc @ ada_w[l] + ada_b[l]
        sh1, sc1, g1, sh2, sc2, g2 = jnp.split(mod_lat, 6, axis=-1)
        csh1, csc1, cg1, csh2, csc2, cg2 = jnp.split(mod_ctx, 6, axis=-1)

        h_lat = modulate(rmsnorm(x, norm1_g[l]), sh1, sc1)
        h_ctx = modulate(rmsnorm(ctx, norm1_g[l]), csh1, csc1)
        z_lat = h_lat @ w_in[l]
        if last:
            kv_ctx = h_ctx @ w_in[l][:, B_K:C_IN]
        else:
            z_ctx = h_ctx @ w_in[l]
            kv_ctx = z_ctx[..., B_K:C_IN]
        k_ctx = qk_heads(kv_ctx[..., :QK_COLS], k_norm_g[l])
        v_ctx = v_heads(kv_ctx[..., QK_COLS:])
        q_lat = apply_rope(qk_heads(z_lat[..., B_Q:B_K], q_norm_g[l]), cos, sin)
        k_lat = apply_rope(qk_heads(z_lat[..., B_K:B_V], k_norm_g[l]), cos, sin)
        v_lat = v_heads(z_lat[..., B_V:C_IN])
        k_all = jnp.concatenate([k_lat, k_ctx], axis=1)
        v_all = jnp.concatenate([v_lat, v_ctx], axis=1)
        attn_lat = blocked_diff_attention(q_lat, k_all, v_all, lam)
        out_lat = mixer_merge(z_lat, attn_lat, lam_init, sgu_ln_g[l], sgu_w[l], sgu_b[l],
                              conv_w[l], subln_g[l], w_br_a[l], w_br_b[l], w_br_c[l], w_out[l])
        x = x + g1 * out_lat
        h2 = modulate(rmsnorm(x, norm2_g[l]), sh2, sc2)
        x = x + g2 * conv_ffn(h2, ffn_up[l], ffn_conv_w[l], ffn_conv_b[l], ffn_down[l])

        if not last:
            q_ctx = qk_heads(z_ctx[..., B_Q:B_K], q_norm_g[l])
            attn_ctx = diff_attention(q_ctx, k_ctx, v_ctx, lam)
            out_ctx = mixer_merge(z_ctx, attn_ctx, lam_init, sgu_ln_g[l], sgu_w[l], sgu_b[l],
                                  conv_w[l], subln_g[l], w_br_a[l], w_br_b[l], w_br_c[l], w_out[l])
            ctx = ctx + cg1 * out_ctx
            h2c = modulate(rmsnorm(ctx, norm2_g[l]), csh2, csc2)
            ctx = ctx + cg2 * conv_ffn(h2c, ffn_up[l], ffn_conv_w[l], ffn_conv_b[l], ffn_down[l])
    return x
```

```python
import functools
import math

import jax
import jax.numpy as jnp
import numpy as np
from jax import lax
from jax.experimental import pallas as pl
from jax.experimental.pallas import tpu as pltpu

F32 = jnp.float32
BF16 = jnp.bfloat16

D_MODEL = 1024
GRID_W = 64
A_WIDTH = 512
A_GROUPS = 4
CHUNK = 128
B_HEADS = 8
B_QK_DIM = 64
B_V_DIM = 128
B_WIDTH = B_HEADS * B_V_DIM
QK_COLS = 2 * B_HEADS * B_QK_DIM
ROPE_THETA = 10000.0
C_WIDTH = 512
N_BRANCH = 3
D_FF = 2816
EPS = 1e-6
IN_COLS = 8704

_REF_A_U = 0
_REF_A_V = 512
_REF_B_Q = 1024
_REF_B_K = 2048
_REF_B_V = 3072
_REF_C_IN = 4096
_REF_GATE = 5632

PROJ_TN = 512
_J_U = 0
_J_V = 1
_J_Q = 2
_J_K = 4
_J_VAL = 6
_J_GATE = 8
_J_CIN = 14
_N_PROJ_TILES = IN_COLS // PROJ_TN

HALO = 16
LANES = 128
FF_TN = 256
QK_SCALE_LOG2E = (B_QK_DIM ** -0.5) * math.log2(math.e)
SAFE_EXP2_RANGE = 60.0
VMEM_LIMIT = 56 * 1024 * 1024


def _proj_column_order():
    idx = []
    idx += list(range(_REF_A_U, _REF_A_U + 512))
    idx += list(range(_REF_A_V, _REF_A_V + 512))
    for base in (_REF_B_Q, _REF_B_K):
        for h in range(B_HEADS):
            for a in range(2):
                start = base + a * B_HEADS * B_QK_DIM + h * B_QK_DIM
                idx += list(range(start, start + B_QK_DIM))
    idx += list(range(_REF_B_V, _REF_B_V + B_WIDTH))
    idx += list(range(_REF_GATE, _REF_GATE + N_BRANCH * D_MODEL))
    idx += list(range(_REF_C_IN, _REF_C_IN + 3 * C_WIDTH))
    assert len(idx) == IN_COLS
    return np.asarray(idx, dtype=np.int32)


def _rope_tables(n_tokens):
    rows = n_tokens // GRID_W
    row = jnp.repeat(jnp.arange(rows, dtype=F32), GRID_W)
    col = jnp.tile(jnp.arange(GRID_W, dtype=F32), rows)
    n_freq = B_QK_DIM // 4
    inv = ROPE_THETA ** (-jnp.arange(n_freq, dtype=F32) / n_freq)
    ang_r = row[:, None] * inv
    ang_c = col[:, None] * inv
    cos64 = jnp.concatenate([jnp.cos(ang_r), jnp.cos(ang_r), jnp.cos(ang_c), jnp.cos(ang_c)], axis=1)
    sin64 = jnp.concatenate([-jnp.sin(ang_r), jnp.sin(ang_r), -jnp.sin(ang_c), jnp.sin(ang_c)], axis=1)
    return jnp.tile(cos64, (1, 2)), jnp.tile(sin64, (1, 2))


def _mod_kernel(cc_ref, w_ref, b_ref, o_ref):
    cc = cc_ref[...]
    s = cc * jax.nn.sigmoid(cc)
    o_ref[...] = jnp.dot(s, w_ref[...], preferred_element_type=F32,
                         precision=lax.Precision.HIGHEST) + b_ref[...]


def _modulation(cc, ada_w, ada_b):
    depth, d, n = ada_w.shape
    tn = 1536
    rows = cc.shape[0]
    return pl.pallas_call(
        _mod_kernel,
        grid=(depth, n // tn),
        in_specs=[
            pl.BlockSpec((rows, d), lambda l, j: (0, 0)),
            pl.BlockSpec((None, d, tn), lambda l, j: (l, 0, j)),
            pl.BlockSpec((None, 1, tn), lambda l, j: (l, 0, j)),
        ],
        out_specs=pl.BlockSpec((None, rows, tn), lambda l, j: (l, 0, j)),
        out_shape=jax.ShapeDtypeStruct((depth, rows, n), F32),
        compiler_params=pltpu.CompilerParams(
            dimension_semantics=("arbitrary", "arbitrary"), vmem_limit_bytes=VMEM_LIMIT),
        name="modulation",
    )(cc, ada_w, ada_b.reshape(depth, 1, n))


def _norm_modulate(x, gain, shift, scale):
    ms = jnp.mean(x * x, axis=-1, keepdims=True)
    y = x * lax.rsqrt(ms + EPS) * gain
    return y * (1.0 + scale) + shift


def _proj_kernel(x_ref, sh_ref, sc_ref, ng_ref, w_ref, qg_ref, kg_ref, lng_ref,
                 cos_ref, sin_ref, gmat_ref, o_ref, h_ref, *, tm, use_rope):
    j = pl.program_id(1)

    @pl.when(j == 0)
    def _():
        h_ref[...] = _norm_modulate(x_ref[...], ng_ref[...], sh_ref[...], sc_ref[...]).astype(BF16)

    acc = jnp.dot(h_ref[...], w_ref[...], preferred_element_type=F32)

    @pl.when(j == _J_U)
    def _():
        o_ref[...] = jax.nn.gelu(acc).astype(BF16)

    @pl.when(j == _J_V)
    def _():
        g = jax.nn.gelu(acc)
        mu = jnp.mean(g, axis=-1, keepdims=True)
        c = g - mu
        var = jnp.mean(c * c, axis=-1, keepdims=True)
        o_ref[...] = (c * lax.rsqrt(var + EPS) * lng_ref[...]).astype(BF16)

    def qk_epilogue(gain, scale):
        ms = jnp.dot((acc * acc).astype(BF16), gmat_ref[...], preferred_element_type=F32)
        y = acc * lax.rsqrt(ms + EPS) * gain
        if not use_rope:
            o_ref[...] = (y * scale).astype(BF16)
            return
        lane = lax.broadcasted_iota(jnp.int32, (tm, LANES), 1)
        first_half = (lane & 16) == 0
        cos = cos_ref[...]
        sin = sin_ref[...]
        for c in range(PROJ_TN // LANES):
            yc = y[:, c * LANES:(c + 1) * LANES]
            partner = jnp.where(first_half, pltpu.roll(yc, LANES - 16, 1), pltpu.roll(yc, 16, 1))
            o_ref[:, c * LANES:(c + 1) * LANES] = ((yc * cos + partner * sin) * scale).astype(BF16)

    @pl.when((j >= _J_Q) & (j < _J_K))
    def _():
        qk_epilogue(qg_ref[...], QK_SCALE_LOG2E)

    @pl.when((j >= _J_K) & (j < _J_VAL))
    def _():
        qk_epilogue(kg_ref[...], 1.0)

    @pl.when(((j >= _J_VAL) & (j < _J_GATE)) | (j >= _J_CIN))
    def _():
        o_ref[...] = acc.astype(BF16)

    @pl.when((j >= _J_GATE) & (j < _J_CIN))
    def _():
        o_ref[...] = jax.nn.sigmoid(acc).astype(BF16)


def _proj(x2d, shift, scale, norm_g, w, qg, kg, lng, cos, sin, gmat, *, tm, seq_len, use_rope):
    rows, d = x2d.shape
    tps = seq_len // tm
    nb = shift.shape[0]
    pos_tiles = cos.shape[0] // tm

    def bidx(i):
        return (i // tps) % nb

    return pl.pallas_call(
        functools.partial(_proj_kernel, tm=tm, use_rope=use_rope),
        grid=(rows // tm, _N_PROJ_TILES),
        in_specs=[
            pl.BlockSpec((tm, d), lambda i, j: (i, 0)),
            pl.BlockSpec((None, 1, d), lambda i, j: (bidx(i), 0, 0)),
            pl.BlockSpec((None, 1, d), lambda i, j: (bidx(i), 0, 0)),
            pl.BlockSpec((1, d), lambda i, j: (0, 0)),
            pl.BlockSpec((d, PROJ_TN), lambda i, j: (0, j)),
            pl.BlockSpec((1, PROJ_TN), lambda i, j: (0, 0)),
            pl.BlockSpec((1, PROJ_TN), lambda i, j: (0, 0)),
            pl.BlockSpec((1, PROJ_TN), lambda i, j: (0, 0)),
            pl.BlockSpec((tm, LANES), lambda i, j: (i % pos_tiles, 0)),
            pl.BlockSpec((tm, LANES), lambda i, j: (i % pos_tiles, 0)),
            pl.BlockSpec((PROJ_TN, PROJ_TN), lambda i, j: (0, 0)),
        ],
        out_specs=pl.BlockSpec((tm, PROJ_TN), lambda i, j: (i, j)),
        out_shape=jax.ShapeDtypeStruct((rows, IN_COLS), BF16),
        scratch_shapes=[pltpu.VMEM((tm, d), BF16)],
        compiler_params=pltpu.CompilerParams(
            dimension_semantics=("arbitrary", "arbitrary"), vmem_limit_bytes=VMEM_LIMIT),
        name="proj",
    )(x2d, shift, scale, norm_g, w, qg, kg, lng, cos, sin, gmat)


def _attn_kernel(*refs, tq, tk, src_rows, lam_init):
    n_src = len(src_rows)
    q_ref = refs[0]
    kv_refs = [(refs[1 + 2 * s], refs[2 + 2 * s]) for s in range(n_src)]
    lam_ref, subg_ref, o_ref, kmax_ref = refs[1 + 2 * n_src:]
    qi = pl.program_id(2)

    lane_k = lax.broadcasted_iota(jnp.int32, (1, LANES), 1)
    map1 = lane_k < B_QK_DIM

    @pl.when(qi == 0)
    def _():
        best1 = jnp.zeros((1, 1), F32)
        best2 = jnp.zeros((1, 1), F32)
        for k_ref, _ in kv_refs:
            kf = k_ref[...].astype(F32)
            sq = kf * kf
            n1 = jnp.sum(jnp.where(map1, sq, 0.0), axis=1, keepdims=True)
            n2 = jnp.sum(jnp.where(map1, 0.0, sq), axis=1, keepdims=True)
            best1 = jnp.maximum(best1, jnp.max(n1, axis=0, keepdims=True))
            best2 = jnp.maximum(best2, jnp.max(n2, axis=0, keepdims=True))
        kmax_ref[0:1, :] = jnp.broadcast_to(best1, (1, LANES))
        kmax_ref[1:2, :] = jnp.broadcast_to(best2, (1, LANES))

    q = q_ref[...]
    zero = jnp.zeros_like(q)
    q1 = jnp.where(map1, q, zero)
    q2 = jnp.where(map1, zero, q)
    qf = q.astype(F32)
    qsq = qf * qf
    qn1 = jnp.sum(jnp.where(map1, qsq, 0.0), axis=1, keepdims=True)
    qn2 = jnp.sum(jnp.where(map1, 0.0, qsq), axis=1, keepdims=True)
    bound1 = jnp.sqrt(qn1 * kmax_ref[0:1, 0:1])
    bound2 = jnp.sqrt(qn2 * kmax_ref[1:2, 0:1])
    worst = jnp.max(jnp.maximum(bound1, bound2))

    nt = (((1,), (1,)), ((), ()))

    def scores(qm, k_ref, start, size):
        kb = k_ref[pl.ds(start, size), :]
        return lax.dot_general(qm, kb, nt, preferred_element_type=F32)

    def blocks(fn, carry):
        for (k_ref, v_ref), rows in zip(kv_refs, src_rows):
            size = min(tk, rows)
            nblk = rows // size
            if nblk == 1:
                carry = fn(k_ref, v_ref, 0, size, carry)
            else:
                def body(b, c, k_ref=k_ref, v_ref=v_ref, size=size):
                    return fn(k_ref, v_ref, pl.multiple_of(b * size, size), size, c)
                carry = lax.fori_loop(0, nblk, body, carry)
        return carry

    def lane_fold(p):
        acc = p[:, 0:LANES]
        for c in range(1, p.shape[1] // LANES):
            acc = acc + p[:, c * LANES:(c + 1) * LANES]
        return acc

    def finish(shift1, shift2):
        def step(k_ref, v_ref, start, size, carry):
            l1, o1, l2, o2 = carry
            vb = v_ref[pl.ds(start, size), :]
            p1 = jnp.exp2(scores(q1, k_ref, start, size) - shift1)
            p2 = jnp.exp2(scores(q2, k_ref, start, size) - shift2)
            l1 = l1 + lane_fold(p1)
            l2 = l2 + lane_fold(p2)
            o1 = o1 + jnp.dot(p1.astype(BF16), vb, preferred_element_type=F32)
            o2 = o2 + jnp.dot(p2.astype(BF16), vb, preferred_element_type=F32)
            return l1, o1, l2, o2

        z = jnp.zeros((tq, LANES), F32)
        l1, o1, l2, o2 = blocks(step, (z, z, z, z))
        s1 = jnp.sum(l1, axis=1, keepdims=True)
        s2 = jnp.sum(l2, axis=1, keepdims=True)
        lv = lam_ref[...]
        lam = (jnp.exp(jnp.sum(lv[0:1] * lv[1:2], axis=1, keepdims=True))
               - jnp.exp(jnp.sum(lv[2:3] * lv[3:4], axis=1, keepdims=True)) + lam_init)
        a = o1 / s1 - lam * (o2 / s2)
        ms = jnp.mean(a * a, axis=-1, keepdims=True)
        y = a * lax.rsqrt(ms + EPS) * subg_ref[...]
        o_ref[...] = (y * (1.0 - lam_init)).astype(BF16)

    @pl.when(worst < SAFE_EXP2_RANGE)
    def _():
        finish(bound1, bound2)

    @pl.when(jnp.logical_not(worst < SAFE_EXP2_RANGE))
    def _():
        def mx_step(k_ref, v_ref, start, size, carry):
            m1, m2 = carry
            m1 = jnp.maximum(m1, jnp.max(scores(q1, k_ref, start, size), axis=1, keepdims=True))
            m2 = jnp.maximum(m2, jnp.max(scores(q2, k_ref, start, size), axis=1, keepdims=True))
            return m1, m2

        neg = jnp.full((tq, 1), -jnp.inf, F32)
        m1, m2 = blocks(mx_step, (neg, neg))
        finish(m1, m2)


def _attention(zq, kv_sources, lam_vecs, subg, *, batch, seq_len, tq, tk, lam_init):
    rows = zq.shape[0]
    nq = seq_len // tq
    q_col = _J_Q * PROJ_TN // LANES
    k_col = _J_K * PROJ_TN // LANES
    v_col = _J_VAL * PROJ_TN // LANES

    in_specs = [pl.BlockSpec((tq, LANES), lambda b, h, i: (b * nq + i, q_col + h))]
    args = [zq]
    src_rows = []
    for z, n in kv_sources:
        in_specs.append(pl.BlockSpec((n, LANES), lambda b, h, i: (b, k_col + h)))
        in_specs.append(pl.BlockSpec((n, LANES), lambda b, h, i: (b, v_col + h)))
        args += [z, z]
        src_rows.append(n)
    in_specs.append(pl.BlockSpec((4, B_QK_DIM), lambda b, h, i: (0, 0)))
    in_specs.append(pl.BlockSpec((1, B_V_DIM), lambda b, h, i: (0, 0)))
    args += [lam_vecs, subg]

    return pl.pallas_call(
        functools.partial(_attn_kernel, tq=tq, tk=tk, src_rows=tuple(src_rows), lam_init=lam_init),
        grid=(batch, B_HEADS, nq),
        in_specs=in_specs,
        out_specs=pl.BlockSpec((tq, LANES), lambda b, h, i: (b * nq + i, h)),
        out_shape=jax.ShapeDtypeStruct((rows, B_WIDTH), BF16),
        scratch_shapes=[pltpu.VMEM((8, LANES), F32)],
        compiler_params=pltpu.CompilerParams(
            dimension_semantics=("arbitrary", "arbitrary", "arbitrary"), vmem_limit_bytes=VMEM_LIMIT),
        name="attn",
    )(*args)


def _merge_kernel(u_ref, v_ref, yb_ref, ga_ref, gb_ref, gc_ref, cb_ref, cg_ref, cx_ref,
                  pg_ref, px_ref, ng_ref, nx_ref, x_ref, g1_ref,
                  sw_ref, sb_ref, cw_ref, wa_ref, wb_ref, wc_ref, wo_ref,
                  o_ref, ya_ref, *, tm, tps):
    i = pl.program_id(0)
    first = (i % tps) == 0
    last = (i % tps) == (tps - 1)

    for c in range(tm // CHUNK):
        rs = slice(c * CHUNK, (c + 1) * CHUNK)
        for g in range(A_GROUPS):
            cs = slice(g * LANES, (g + 1) * LANES)
            mixed = jnp.dot(sw_ref[g], v_ref[rs, cs], preferred_element_type=F32) + sb_ref[:, cs]
            ya_ref[rs, cs] = (u_ref[rs, cs].astype(F32) * mixed).astype(BF16)

    t = cg_ref[...].astype(F32) * cx_ref[...].astype(F32)
    t_prev_row = pg_ref[HALO - 1:HALO, :].astype(F32) * px_ref[HALO - 1:HALO, :].astype(F32)
    t_next_row = ng_ref[0:1, :].astype(F32) * nx_ref[0:1, :].astype(F32)
    t_prev_row = jnp.where(first, 0.0, t_prev_row)
    t_next_row = jnp.where(last, 0.0, t_next_row)
    row = lax.broadcasted_iota(jnp.int32, (tm, C_WIDTH), 0)
    t_prev = jnp.where(row == 0, t_prev_row, pltpu.roll(t, 1, 0))
    t_next = jnp.where(row == tm - 1, t_next_row, pltpu.roll(t, tm - 1, 0))
    cw = cw_ref[...]
    y_c = cb_ref[...].astype(F32) * (t_prev * cw[0:1] + t * cw[1:2] + t_next * cw[2:3])

    m = ga_ref[...].astype(F32) * jnp.dot(ya_ref[...], wa_ref[...], preferred_element_type=F32)
    m = m + gb_ref[...].astype(F32) * jnp.dot(yb_ref[...], wb_ref[...], preferred_element_type=F32)
    m = m + gc_ref[...].astype(F32) * jnp.dot(y_c.astype(BF16), wc_ref[...], preferred_element_type=F32)
    out = jnp.dot(m.astype(BF16), wo_ref[...], preferred_element_type=F32)
    o_ref[...] = x_ref[...] + g1_ref[...] * out


def _merge(z, yb, x2d, g1, sw, sb, cw, wa, wb, wc, wo, *, tm, seq_len):
    rows, d = x2d.shape
    tps = seq_len // tm
    nb = g1.shape[0]
    hb = tm // HALO
    n_halo = rows // HALO
    gate_col = _J_GATE * PROJ_TN // D_MODEL

    def bidx(i):
        return (i // tps) % nb

    def prev_blk(i):
        return jnp.maximum(i * hb - 1, 0)

    def next_blk(i):
        return jnp.minimum((i + 1) * hb, n_halo - 1)

    const = lambda i: (0, 0)
    return pl.pallas_call(
        functools.partial(_merge_kernel, tm=tm, tps=tps),
        grid=(rows // tm,),
        in_specs=[
            pl.BlockSpec((tm, A_WIDTH), lambda i: (i, _J_U)),
            pl.BlockSpec((tm, A_WIDTH), lambda i: (i, _J_V)),
            pl.BlockSpec((tm, B_WIDTH), lambda i: (i, 0)),
            pl.BlockSpec((tm, d), lambda i: (i, gate_col)),
            pl.BlockSpec((tm, d), lambda i: (i, gate_col + 1)),
            pl.BlockSpec((tm, d), lambda i: (i, gate_col + 2)),
            pl.BlockSpec((tm, C_WIDTH), lambda i: (i, _J_CIN)),
            pl.BlockSpec((tm, C_WIDTH), lambda i: (i, _J_CIN + 1)),
            pl.BlockSpec((tm, C_WIDTH), lambda i: (i, _J_CIN + 2)),
            pl.BlockSpec((HALO, C_WIDTH), lambda i: (prev_blk(i), _J_CIN + 1)),
            pl.BlockSpec((HALO, C_WIDTH), lambda i: (prev_blk(i), _J_CIN + 2)),
            pl.BlockSpec((HALO, C_WIDTH), lambda i: (next_blk(i), _J_CIN + 1)),
            pl.BlockSpec((HALO, C_WIDTH), lambda i: (next_blk(i), _J_CIN + 2)),
            pl.BlockSpec((tm, d), lambda i: (i, 0)),
            pl.BlockSpec((None, 1, d), lambda i: (bidx(i), 0, 0)),
            pl.BlockSpec((A_GROUPS, CHUNK, CHUNK), lambda i: (0, 0, 0)),
            pl.BlockSpec((CHUNK, A_WIDTH), const),
            pl.BlockSpec((3, C_WIDTH), const),
            pl.BlockSpec((A_WIDTH, d), const),
            pl.BlockSpec((B_WIDTH, d), const),
            pl.BlockSpec((C_WIDTH, d), const),
            pl.BlockSpec((d, d), const),
        ],
        out_specs=pl.BlockSpec((tm, d), lambda i: (i, 0)),
        out_shape=jax.ShapeDtypeStruct((rows, d), F32),
        scratch_shapes=[pltpu.VMEM((tm, A_WIDTH), BF16)],
        compiler_params=pltpu.CompilerParams(
            dimension_semantics=("arbitrary",), vmem_limit_bytes=VMEM_LIMIT),
        name="merge",
    )(z, z, yb, z, z, z, z, z, z, z, z, z, z, x2d, g1, sw, sb, cw, wa, wb, wc, wo)


def _ffn_kernel(x_ref, xp_ref, xn_ref, sh_ref, sc_ref, gt_ref, ng_ref,
                wug_ref, wuv_ref, cwg_ref, cwv_ref, cbg_ref, cbv_ref, wd_ref,
                o_ref, h_ref, ug_ref, uv_ref, *, tm, tps, n_ff):
    i = pl.program_id(0)
    j = pl.program_id(1)

    @pl.when(j == 0)
    def _():
        def nm(x):
            return _norm_modulate(x, ng_ref[...], sh_ref[...], sc_ref[...])

        first = (i % tps) == 0
        last = (i % tps) == (tps - 1)
        h_ref[HALO:HALO + tm, :] = nm(x_ref[...]).astype(BF16)
        h_ref[0:HALO, :] = jnp.where(first, 0.0, nm(xp_ref[...])).astype(BF16)
        h_ref[HALO + tm:, :] = jnp.where(last, 0.0, nm(xn_ref[...])).astype(BF16)

    h = h_ref[...]
    ug_ref[...] = jnp.dot(h, wug_ref[...], preferred_element_type=F32)
    uv_ref[...] = jnp.dot(h, wuv_ref[...], preferred_element_type=F32)

    def conv(u_ref, cw_ref, cb_ref):
        cw = cw_ref[...]
        return (u_ref[HALO - 1:HALO - 1 + tm, :] * cw[0:1] + u_ref[HALO:HALO + tm, :] * cw[1:2]
                + u_ref[HALO + 1:HALO + 1 + tm, :] * cw[2:3] + cb_ref[...])

    g = conv(ug_ref, cwg_ref, cbg_ref)
    val = conv(uv_ref, cwv_ref, cbv_ref)
    a = (g * jax.nn.sigmoid(g) * val).astype(BF16)
    part = jnp.dot(a, wd_ref[...], preferred_element_type=F32)

    @pl.when(j == 0)
    def _():
        o_ref[...] = part

    @pl.when(j > 0)
    def _():
        o_ref[...] += part

    @pl.when(j == n_ff - 1)
    def _():
        o_ref[...] = x_ref[...] + gt_ref[...] * o_ref[...]


def _ffn(x2d, shift, scale, gate, norm_g, w_up, cw, cb, w_down, *, tm, seq_len):
    rows, d = x2d.shape
    tps = seq_len // tm
    nb = shift.shape[0]
    hb = tm // HALO
    n_halo = rows // HALO
    n_ff = D_FF // FF_TN

    def bidx(i):
        return (i // tps) % nb

    vec = pl.BlockSpec((None, 1, d), lambda i, j: (bidx(i), 0, 0))
    return pl.pallas_call(
        functools.partial(_ffn_kernel, tm=tm, tps=tps, n_ff=n_ff),
        grid=(rows // tm, n_ff),
        in_specs=[
            pl.BlockSpec((tm, d), lambda i, j: (i, 0)),
            pl.BlockSpec((HALO, d), lambda i, j: (jnp.maximum(i * hb - 1, 0), 0)),
            pl.BlockSpec((HALO, d), lambda i, j: (jnp.minimum((i + 1) * hb, n_halo - 1), 0)),
            vec, vec, vec,
            pl.BlockSpec((1, d), lambda i, j: (0, 0)),
            pl.BlockSpec((d, FF_TN), lambda i, j: (0, j)),
            pl.BlockSpec((d, FF_TN), lambda i, j: (0, n_ff + j)),
            pl.BlockSpec((3, FF_TN), lambda i, j: (0, j)),
            pl.BlockSpec((3, FF_TN), lambda i, j: (0, n_ff + j)),
            pl.BlockSpec((1, FF_TN), lambda i, j: (0, j)),
            pl.BlockSpec((1, FF_TN), lambda i, j: (0, n_ff + j)),
            pl.BlockSpec((FF_TN, d), lambda i, j: (j, 0)),
        ],
        out_specs=pl.BlockSpec((tm, d), lambda i, j: (i, 0)),
        out_shape=jax.ShapeDtypeStruct((rows, d), F32),
        scratch_shapes=[
            pltpu.VMEM((tm + 2 * HALO, d), BF16),
            pltpu.VMEM((tm + 2 * HALO, FF_TN), F32),
            pltpu.VMEM((tm + 2 * HALO, FF_TN), F32),
        ],
        compiler_params=pltpu.CompilerParams(
            dimension_semantics=("arbitrary", "arbitrary"), vmem_limit_bytes=VMEM_LIMIT),
        name="ffn",
    )(x2d, x2d, x2d, shift, scale, gate, norm_g, w_up, w_up, cw, cw, cb, cb, w_down)


def kernel(x, c, ctx, c_ctx, ada_w, ada_b, norm1_g, norm2_g, w_in, sgu_ln_g, sgu_w, sgu_b, q_norm_g, k_norm_g, lam_q1, lam_k1, lam_q2, lam_k2, subln_g, conv_w, w_br_a, w_br_b, w_br_c, w_out, ffn_up, ffn_conv_w, ffn_conv_b, ffn_down):
    batch, seq, d = x.shape
    ctx_len = ctx.shape[1]
    depth = ada_w.shape[0]
    assert d == D_MODEL and seq % GRID_W == 0

    tm_lat = min(1024, seq)
    tm_mrg = min(512, seq)
    tm_ctx = ctx_len
    tq_lat = min(256, seq)
    tk = 512
    assert seq % tm_lat == 0 and seq % tq_lat == 0 and ctx_len % HALO == 0

    perm = _proj_column_order()
    w_in_b = jnp.take(w_in, perm, axis=2).astype(BF16)
    wa_b, wb_b, wc_b, wo_b = (w.astype(BF16) for w in (w_br_a, w_br_b, w_br_c, w_out))
    up_b = ffn_up.astype(BF16)
    down_b = ffn_down.astype(BF16)
    sw_b = sgu_w.astype(BF16)
    sgu_bias = jnp.repeat(jnp.swapaxes(sgu_b, 1, 2), LANES, axis=2)
    gmat = jnp.asarray(np.kron(np.eye(PROJ_TN // B_QK_DIM), np.full((B_QK_DIM, B_QK_DIM), 1.0 / B_QK_DIM)), BF16)
    cos, sin = _rope_tables(seq)
    cos_ctx = jnp.ones((tm_ctx, LANES), F32)
    sin_ctx = jnp.zeros((tm_ctx, LANES), F32)
    lam_vecs = jnp.stack([lam_q1, lam_k1, lam_q2, lam_k2], axis=1).astype(F32)

    pad = (-(batch + 1)) % 8
    cc = jnp.concatenate([c, c_ctx[None, :], jnp.zeros((pad, d), F32)], axis=0)
    mod = _modulation(cc, ada_w, ada_b)
    mod = mod.reshape(depth, cc.shape[0], 6, 1, d)

    xs = x.reshape(batch * seq, d)
    cs = ctx.reshape(batch * ctx_len, d)

    for l in range(depth):
        last = l == depth - 1
        lam_init = 0.8 - 0.6 * math.exp(-0.3 * l)
        lat = [mod[l, :batch, k] for k in range(6)]
        cm = [mod[l, batch:batch + 1, k] for k in range(6)]
        n1 = norm1_g[l].reshape(1, d)
        n2 = norm2_g[l].reshape(1, d)
        qg = jnp.tile(q_norm_g[l], PROJ_TN // B_QK_DIM).reshape(1, PROJ_TN)
        kg = jnp.tile(k_norm_g[l], PROJ_TN // B_QK_DIM).reshape(1, PROJ_TN)
        lng = sgu_ln_g[l].reshape(1, A_WIDTH)
        subg = subln_g[l].reshape(1, B_V_DIM)

        z_lat = _proj(xs, lat[0], lat[1], n1, w_in_b[l], qg, kg, lng, cos, sin, gmat,
                      tm=tm_lat, seq_len=seq, use_rope=True)
        z_ctx = _proj(cs, cm[0], cm[1], n1, w_in_b[l], qg, kg, lng, cos_ctx, sin_ctx, gmat,
                      tm=tm_ctx, seq_len=ctx_len, use_rope=False)

        yb_lat = _attention(z_lat, [(z_lat, seq), (z_ctx, ctx_len)], lam_vecs[l], subg,
                            batch=batch, seq_len=seq, tq=tq_lat, tk=tk, lam_init=lam_init)
        merge_w = (sw_b[l], sgu_bias[l], conv_w[l], wa_b[l], wb_b[l], wc_b[l], wo_b[l])
        ffn_w = (up_b[l], ffn_conv_w[l], ffn_conv_b[l].reshape(1, 2 * D_FF), down_b[l])
        xs = _merge(z_lat, yb_lat, xs, lat[2], *merge_w, tm=tm_mrg, seq_len=seq)
        xs = _ffn(xs, lat[3], lat[4], lat[5], n2, *ffn_w, tm=tm_lat, seq_len=seq)

        if not last:
            yb_ctx = _attention(z_ctx, [(z_ctx, ctx_len)], lam_vecs[l], subg,
                                batch=batch, seq_len=ctx_len, tq=tm_ctx, tk=tk, lam_init=lam_init)
            cs = _merge(z_ctx, yb_ctx, cs, cm[2], *merge_w, tm=tm_ctx, seq_len=ctx_len)
            cs = _ffn(cs, cm[3], cm[4], cm[5], n2, *ffn_w, tm=tm_ctx, seq_len=ctx_len)

    return xs.reshape(batch, seq, d)
```

```python
import functools
import math

import jax
import jax.numpy as jnp
import numpy as np
from jax import lax
from jax.experimental import pallas as pl
from jax.experimental.pallas import tpu as pltpu

F32 = jnp.float32
BF16 = jnp.bfloat16

D_MODEL = 1024
GRID_W = 64
A_WIDTH = 512
A_GROUPS = 4
CHUNK = 128
B_HEADS = 8
B_QK_DIM = 64
B_V_DIM = 128
B_WIDTH = B_HEADS * B_V_DIM
QK_COLS = 2 * B_HEADS * B_QK_DIM
ROPE_THETA = 10000.0
C_WIDTH = 512
N_BRANCH = 3
D_FF = 2816
EPS = 1e-6
IN_COLS = 8704

_REF_A_U = 0
_REF_A_V = 512
_REF_B_Q = 1024
_REF_B_K = 2048
_REF_B_V = 3072
_REF_C_IN = 4096
_REF_GATE = 5632

PROJ_TN = 512
_J_U = 0
_J_V = 1
_J_Q = 2
_J_K = 4
_J_VAL = 6
_J_GATE = 8
_J_CIN = 14
_N_PROJ_TILES = IN_COLS // PROJ_TN

HALO = 16
LANES = 128
FF_TN = 256
QK_SCALE_LOG2E = (B_QK_DIM ** -0.5) * math.log2(math.e)
SAFE_EXP2_RANGE = 60.0
VMEM_LIMIT = 56 * 1024 * 1024


def _proj_column_order():
    idx = []
    idx += list(range(_REF_A_U, _REF_A_U + 512))
    idx += list(range(_REF_A_V, _REF_A_V + 512))
    for base in (_REF_B_Q, _REF_B_K):
        for h in range(B_HEADS):
            for a in range(2):
                start = base + a * B_HEADS * B_QK_DIM + h * B_QK_DIM
                idx += list(range(start, start + B_QK_DIM))
    idx += list(range(_REF_B_V, _REF_B_V + B_WIDTH))
    idx += list(range(_REF_GATE, _REF_GATE + N_BRANCH * D_MODEL))
    idx += list(range(_REF_C_IN, _REF_C_IN + 3 * C_WIDTH))
    assert len(idx) == IN_COLS
    return np.asarray(idx, dtype=np.int32)


def _rope_tables(n_tokens):
    rows = n_tokens // GRID_W
    row = jnp.repeat(jnp.arange(rows, dtype=F32), GRID_W)
    col = jnp.tile(jnp.arange(GRID_W, dtype=F32), rows)
    n_freq = B_QK_DIM // 4
    inv = ROPE_THETA ** (-jnp.arange(n_freq, dtype=F32) / n_freq)
    ang_r = row[:, None] * inv
    ang_c = col[:, None] * inv
    cos64 = jnp.concatenate([jnp.cos(ang_r), jnp.cos(ang_r), jnp.cos(ang_c), jnp.cos(ang_c)], axis=1)
    sin64 = jnp.concatenate([-jnp.sin(ang_r), jnp.sin(ang_r), -jnp.sin(ang_c), jnp.sin(ang_c)], axis=1)
    return jnp.tile(cos64, (1, 2)), jnp.tile(sin64, (1, 2))


def _mod_kernel(cc_ref, w_ref, b_ref, o_ref):
    cc = cc_ref[...]
    s = cc * jax.nn.sigmoid(cc)
    o_ref[...] = jnp.dot(s, w_ref[...], preferred_element_type=F32,
                         precision=lax.Precision.HIGHEST) + b_ref[...]


def _modulation(cc, ada_w, ada_b):
    depth, d, n = ada_w.shape
    tn = 1536
    rows = cc.shape[0]
    return pl.pallas_call(
        _mod_kernel,
        grid=(depth, n // tn),
        in_specs=[
            pl.BlockSpec((rows, d), lambda l, j: (0, 0)),
            pl.BlockSpec((None, d, tn), lambda l, j: (l, 0, j)),
            pl.BlockSpec((None, 1, tn), lambda l, j: (l, 0, j)),
        ],
        out_specs=pl.BlockSpec((None, rows, tn), lambda l, j: (l, 0, j)),
        out_shape=jax.ShapeDtypeStruct((depth, rows, n), F32),
        compiler_params=pltpu.CompilerParams(
            dimension_semantics=("arbitrary", "arbitrary"), vmem_limit_bytes=VMEM_LIMIT),
        name="modulation",
    )(cc, ada_w, ada_b.reshape(depth, 1, n))


def _norm_modulate(x, gain, shift, scale):
    ms = jnp.mean(x * x, axis=-1, keepdims=True)
    y = x * lax.rsqrt(ms + EPS) * gain
    return y * (1.0 + scale) + shift


def _proj_kernel(x_ref, sh_ref, sc_ref, ng_ref, w_ref, qg_ref, kg_ref, lng_ref,
                 cos_ref, sin_ref, gmat_ref, o_ref, h_ref, *, tm, use_rope):
    j = pl.program_id(1)

    @pl.when(j == 0)
    def _():
        h_ref[...] = _norm_modulate(x_ref[...], ng_ref[...], sh_ref[...], sc_ref[...]).astype(BF16)

    acc = jnp.dot(h_ref[...], w_ref[...], preferred_element_type=F32)

    @pl.when(j == _J_U)
    def _():
        o_ref[...] = jax.nn.gelu(acc).astype(BF16)

    @pl.when(j == _J_V)
    def _():
        g = jax.nn.gelu(acc)
        mu = jnp.mean(g, axis=-1, keepdims=True)
        c = g - mu
        var = jnp.mean(c * c, axis=-1, keepdims=True)
        o_ref[...] = (c * lax.rsqrt(var + EPS) * lng_ref[...]).astype(BF16)

    def qk_epilogue(gain, scale):
        ms = jnp.dot((acc * acc).astype(BF16), gmat_ref[...], preferred_element_type=F32)
        y = acc * lax.rsqrt(ms + EPS) * gain
        if not use_rope:
            o_ref[...] = (y * scale).astype(BF16)
            return
        lane = lax.broadcasted_iota(jnp.int32, (tm, LANES), 1)
        first_half = (lane & 16) == 0
        cos = cos_ref[...]
        sin = sin_ref[...]
        for c in range(PROJ_TN // LANES):
            yc = y[:, c * LANES:(c + 1) * LANES]
            partner = jnp.where(first_half, pltpu.roll(yc, LANES - 16, 1), pltpu.roll(yc, 16, 1))
            o_ref[:, c * LANES:(c + 1) * LANES] = ((yc * cos + partner * sin) * scale).astype(BF16)

    @pl.when((j >= _J_Q) & (j < _J_K))
    def _():
        qk_epilogue(qg_ref[...], QK_SCALE_LOG2E)

    @pl.when((j >= _J_K) & (j < _J_VAL))
    def _():
        qk_epilogue(kg_ref[...], 1.0)

    @pl.when(((j >= _J_VAL) & (j < _J_GATE)) | (j >= _J_CIN))
    def _():
        o_ref[...] = acc.astype(BF16)

    @pl.when((j >= _J_GATE) & (j < _J_CIN))
    def _():
        o_ref[...] = jax.nn.sigmoid(acc).astype(BF16)


def _proj(x2d, shift, scale, norm_g, w, qg, kg, lng, cos, sin, gmat, *, tm, seq_len, use_rope):
    rows, d = x2d.shape
    tps = seq_len // tm
    nb = shift.shape[0]
    pos_tiles = cos.shape[0] // tm

    def bidx(i):
        return (i // tps) % nb

    return pl.pallas_call(
        functools.partial(_proj_kernel, tm=tm, use_rope=use_rope),
        grid=(rows // tm, _N_PROJ_TILES),
        in_specs=[
            pl.BlockSpec((tm, d), lambda i, j: (i, 0)),
            pl.BlockSpec((None, 1, d), lambda i, j: (bidx(i), 0, 0)),
            pl.BlockSpec((None, 1, d), lambda i, j: (bidx(i), 0, 0)),
            pl.BlockSpec((1, d), lambda i, j: (0, 0)),
            pl.BlockSpec((d, PROJ_TN), lambda i, j: (0, j)),
            pl.BlockSpec((1, PROJ_TN), lambda i, j: (0, 0)),
            pl.BlockSpec((1, PROJ_TN), lambda i, j: (0, 0)),
            pl.BlockSpec((1, PROJ_TN), lambda i, j: (0, 0)),
            pl.BlockSpec((tm, LANES), lambda i, j: (i % pos_tiles, 0)),
            pl.BlockSpec((tm, LANES), lambda i, j: (i % pos_tiles, 0)),
            pl.BlockSpec((PROJ_TN, PROJ_TN), lambda i, j: (0, 0)),
        ],
        out_specs=pl.BlockSpec((tm, PROJ_TN), lambda i, j: (i, j)),
        out_shape=jax.ShapeDtypeStruct((rows, IN_COLS), BF16),
        scratch_shapes=[pltpu.VMEM((tm, d), BF16)],
        compiler_params=pltpu.CompilerParams(
            dimension_semantics=("arbitrary", "arbitrary"), vmem_limit_bytes=VMEM_LIMIT),
        name="proj",
    )(x2d, shift, scale, norm_g, w, qg, kg, lng, cos, sin, gmat)


def _attn_kernel(*refs, tq, tk, src_rows, lam_init):
    n_src = len(src_rows)
    q_ref = refs[0]
    kv_refs = [(refs[1 + 2 * s], refs[2 + 2 * s]) for s in range(n_src)]
    lam_ref, subg_ref, o_ref, kmax_ref = refs[1 + 2 * n_src:]
    qi = pl.program_id(2)

    lane_k = lax.broadcasted_iota(jnp.int32, (1, LANES), 1)
    map1 = lane_k < B_QK_DIM

    @pl.when(qi == 0)
    def _():
        best1 = jnp.zeros((1, 1), F32)
        best2 = jnp.zeros((1, 1), F32)
        for k_ref, _ in kv_refs:
            kf = k_ref[...].astype(F32)
            sq = kf * kf
            n1 = jnp.sum(jnp.where(map1, sq, 0.0), axis=1, keepdims=True)
            n2 = jnp.sum(jnp.where(map1, 0.0, sq), axis=1, keepdims=True)
            best1 = jnp.maximum(best1, jnp.max(n1, axis=0, keepdims=True))
            best2 = jnp.maximum(best2, jnp.max(n2, axis=0, keepdims=True))
        kmax_ref[0:1, :] = jnp.broadcast_to(best1, (1, LANES))
        kmax_ref[1:2, :] = jnp.broadcast_to(best2, (1, LANES))

    q = q_ref[...]
    zero = jnp.zeros_like(q)
    q1 = jnp.where(map1, q, zero)
    q2 = jnp.where(map1, zero, q)
    qf = q.astype(F32)
    qsq = qf * qf
    qn1 = jnp.sum(jnp.where(map1, qsq, 0.0), axis=1, keepdims=True)
    qn2 = jnp.sum(jnp.where(map1, 0.0, qsq), axis=1, keepdims=True)
    bound1 = jnp.sqrt(qn1 * kmax_ref[0:1, 0:1])
    bound2 = jnp.sqrt(qn2 * kmax_ref[1:2, 0:1])
    worst = jnp.max(jnp.maximum(bound1, bound2))

    nt = (((1,), (1,)), ((), ()))

    def scores(qm, k_ref, start, size):
        kb = k_ref[pl.ds(start, size), :]
        return lax.dot_general(qm, kb, nt, preferred_element_type=F32)

    def blocks(fn, carry):
        for (k_ref, v_ref), rows in zip(kv_refs, src_rows):
            size = min(tk, rows)
            nblk = rows // size
            for b in range(nblk):
                carry = fn(k_ref, v_ref, b * size, size, carry)
        return carry

    def lane_fold(p):
        acc = p[:, 0:LANES]
        for c in range(1, p.shape[1] // LANES):
            acc = acc + p[:, c * LANES:(c + 1) * LANES]
        return acc

    def finish(shift1, shift2):
        def step(k_ref, v_ref, start, size, carry):
            l1, o1, l2, o2 = carry
            vb = v_ref[pl.ds(start, size), :]
            p1 = jnp.exp2(scores(q1, k_ref, start, size) - shift1)
            p2 = jnp.exp2(scores(q2, k_ref, start, size) - shift2)
            l1 = l1 + lane_fold(p1)
            l2 = l2 + lane_fold(p2)
            o1 = o1 + jnp.dot(p1.astype(BF16), vb, preferred_element_type=F32)
            o2 = o2 + jnp.dot(p2.astype(BF16), vb, preferred_element_type=F32)
            return l1, o1, l2, o2

        z = jnp.zeros((tq, LANES), F32)
        l1, o1, l2, o2 = blocks(step, (z, z, z, z))
        s1 = jnp.sum(l1, axis=1, keepdims=True)
        s2 = jnp.sum(l2, axis=1, keepdims=True)
        lv = lam_ref[...]
        lam = (jnp.exp(jnp.sum(lv[0:1] * lv[1:2], axis=1, keepdims=True))
               - jnp.exp(jnp.sum(lv[2:3] * lv[3:4], axis=1, keepdims=True)) + lam_init)
        a = o1 / s1 - lam * (o2 / s2)
        ms = jnp.mean(a * a, axis=-1, keepdims=True)
        y = a * lax.rsqrt(ms + EPS) * subg_ref[...]
        o_ref[...] = (y * (1.0 - lam_init)).astype(BF16)

    @pl.when(worst < SAFE_EXP2_RANGE)
    def _():
        finish(bound1, bound2)

    @pl.when(jnp.logical_not(worst < SAFE_EXP2_RANGE))
    def _():
        def mx_step(k_ref, v_ref, start, size, carry):
            m1, m2 = carry
            m1 = jnp.maximum(m1, jnp.max(scores(q1, k_ref, start, size), axis=1, keepdims=True))
            m2 = jnp.maximum(m2, jnp.max(scores(q2, k_ref, start, size), axis=1, keepdims=True))
            return m1, m2

        neg = jnp.full((tq, 1), -jnp.inf, F32)
        m1, m2 = blocks(mx_step, (neg, neg))
        finish(m1, m2)


def _attention(zq, kv_sources, lam_vecs, subg, *, batch, seq_len, tq, tk, lam_init):
    rows = zq.shape[0]
    nq = seq_len // tq
    q_col = _J_Q * PROJ_TN // LANES
    k_col = _J_K * PROJ_TN // LANES
    v_col = _J_VAL * PROJ_TN // LANES

    in_specs = [pl.BlockSpec((tq, LANES), lambda b, h, i: (b * nq + i, q_col + h))]
    args = [zq]
    src_rows = []
    for z, n in kv_sources:
        in_specs.append(pl.BlockSpec((n, LANES), lambda b, h, i: (b, k_col + h)))
        in_specs.append(pl.BlockSpec((n, LANES), lambda b, h, i: (b, v_col + h)))
        args += [z, z]
        src_rows.append(n)
    in_specs.append(pl.BlockSpec((4, B_QK_DIM), lambda b, h, i: (0, 0)))
    in_specs.append(pl.BlockSpec((1, B_V_DIM), lambda b, h, i: (0, 0)))
    args += [lam_vecs, subg]

    return pl.pallas_call(
        functools.partial(_attn_kernel, tq=tq, tk=tk, src_rows=tuple(src_rows), lam_init=lam_init),
        grid=(batch, B_HEADS, nq),
        in_specs=in_specs,
        out_specs=pl.BlockSpec((tq, LANES), lambda b, h, i: (b * nq + i, h)),
        out_shape=jax.ShapeDtypeStruct((rows, B_WIDTH), BF16),
        scratch_shapes=[pltpu.VMEM((8, LANES), F32)],
        compiler_params=pltpu.CompilerParams(
            dimension_semantics=("arbitrary", "arbitrary", "arbitrary"), vmem_limit_bytes=VMEM_LIMIT),
        name="attn",
    )(*args)


def _merge_kernel(u_ref, v_ref, yb_ref, ga_ref, gb_ref, gc_ref, cb_ref, cg_ref, cx_ref,
                  pg_ref, px_ref, ng_ref, nx_ref, x_ref, g1_ref,
                  sw_ref, sb_ref, cw_ref, wa_ref, wb_ref, wc_ref, wo_ref,
                  o_ref, ya_ref, *, tm, tps):
    i = pl.program_id(0)
    first = (i % tps) == 0
    last = (i % tps) == (tps - 1)

    for c in range(tm // CHUNK):
        rs = slice(c * CHUNK, (c + 1) * CHUNK)
        for g in range(A_GROUPS):
            cs = slice(g * LANES, (g + 1) * LANES)
            mixed = jnp.dot(sw_ref[g], v_ref[rs, cs], preferred_element_type=F32) + sb_ref[:, cs]
            ya_ref[rs, cs] = (u_ref[rs, cs].astype(F32) * mixed).astype(BF16)

    t = cg_ref[...].astype(F32) * cx_ref[...].astype(F32)
    t_prev_row = pg_ref[HALO - 1:HALO, :].astype(F32) * px_ref[HALO - 1:HALO, :].astype(F32)
    t_next_row = ng_ref[0:1, :].astype(F32) * nx_ref[0:1, :].astype(F32)
    t_prev_row = jnp.where(first, 0.0, t_prev_row)
    t_next_row = jnp.where(last, 0.0, t_next_row)
    row = lax.broadcasted_iota(jnp.int32, (tm, C_WIDTH), 0)
    t_prev = jnp.where(row == 0, t_prev_row, pltpu.roll(t, 1, 0))
    t_next = jnp.where(row == tm - 1, t_next_row, pltpu.roll(t, tm - 1, 0))
    cw = cw_ref[...]
    y_c = cb_ref[...].astype(F32) * (t_prev * cw[0:1] + t * cw[1:2] + t_next * cw[2:3])

    m = ga_ref[...].astype(F32) * jnp.dot(ya_ref[...], wa_ref[...], preferred_element_type=F32)
    m = m + gb_ref[...].astype(F32) * jnp.dot(yb_ref[...], wb_ref[...], preferred_element_type=F32)
    m = m + gc_ref[...].astype(F32) * jnp.dot(y_c.astype(BF16), wc_ref[...], preferred_element_type=F32)
    out = jnp.dot(m.astype(BF16), wo_ref[...], preferred_element_type=F32)
    o_ref[...] = x_ref[...] + g1_ref[...] * out


def _merge(z, yb, x2d, g1, sw, sb, cw, wa, wb, wc, wo, *, tm, seq_len):
    rows, d = x2d.shape
    tps = seq_len // tm
    nb = g1.shape[0]
    hb = tm // HALO
    n_halo = rows // HALO
    gate_col = _J_GATE * PROJ_TN // D_MODEL

    def bidx(i):
        return (i // tps) % nb

    def prev_blk(i):
        return jnp.maximum(i * hb - 1, 0)

    def next_blk(i):
        return jnp.minimum((i + 1) * hb, n_halo - 1)

    const = lambda i: (0, 0)
    return pl.pallas_call(
        functools.partial(_merge_kernel, tm=tm, tps=tps),
        grid=(rows // tm,),
        in_specs=[
            pl.BlockSpec((tm, A_WIDTH), lambda i: (i, _J_U)),
            pl.BlockSpec((tm, A_WIDTH), lambda i: (i, _J_V)),
            pl.BlockSpec((tm, B_WIDTH), lambda i: (i, 0)),
            pl.BlockSpec((tm, d), lambda i: (i, gate_col)),
            pl.BlockSpec((tm, d), lambda i: (i, gate_col + 1)),
            pl.BlockSpec((tm, d), lambda i: (i, gate_col + 2)),
            pl.BlockSpec((tm, C_WIDTH), lambda i: (i, _J_CIN)),
            pl.BlockSpec((tm, C_WIDTH), lambda i: (i, _J_CIN + 1)),
            pl.BlockSpec((tm, C_WIDTH), lambda i: (i, _J_CIN + 2)),
            pl.BlockSpec((HALO, C_WIDTH), lambda i: (prev_blk(i), _J_CIN + 1)),
            pl.BlockSpec((HALO, C_WIDTH), lambda i: (prev_blk(i), _J_CIN + 2)),
            pl.BlockSpec((HALO, C_WIDTH), lambda i: (next_blk(i), _J_CIN + 1)),
            pl.BlockSpec((HALO, C_WIDTH), lambda i: (next_blk(i), _J_CIN + 2)),
            pl.BlockSpec((tm, d), lambda i: (i, 0)),
            pl.BlockSpec((None, 1, d), lambda i: (bidx(i), 0, 0)),
            pl.BlockSpec((A_GROUPS, CHUNK, CHUNK), lambda i: (0, 0, 0)),
            pl.BlockSpec((CHUNK, A_WIDTH), const),
            pl.BlockSpec((3, C_WIDTH), const),
            pl.BlockSpec((A_WIDTH, d), const),
            pl.BlockSpec((B_WIDTH, d), const),
            pl.BlockSpec((C_WIDTH, d), const),
            pl.BlockSpec((d, d), const),
        ],
        out_specs=pl.BlockSpec((tm, d), lambda i: (i, 0)),
        out_shape=jax.ShapeDtypeStruct((rows, d), F32),
        scratch_shapes=[pltpu.VMEM((tm, A_WIDTH), BF16)],
        compiler_params=pltpu.CompilerParams(
            dimension_semantics=("arbitrary",), vmem_limit_bytes=VMEM_LIMIT),
        name="merge",
    )(z, z, yb, z, z, z, z, z, z, z, z, z, z, x2d, g1, sw, sb, cw, wa, wb, wc, wo)


def _ffn_kernel(x_ref, xp_ref, xn_ref, sh_ref, sc_ref, gt_ref, ng_ref,
                wug_ref, wuv_ref, cwg_ref, cwv_ref, cbg_ref, cbv_ref, wd_ref,
                o_ref, h_ref, ug_ref, uv_ref, *, tm, tps, n_ff):
    i = pl.program_id(0)
    j = pl.program_id(1)

    @pl.when(j == 0)
    def _():
        def nm(x):
            return _norm_modulate(x, ng_ref[...], sh_ref[...], sc_ref[...])

        first = (i % tps) == 0
        last = (i % tps) == (tps - 1)
        h_ref[HALO:HALO + tm, :] = nm(x_ref[...]).astype(BF16)
        h_ref[0:HALO, :] = jnp.where(first, 0.0, nm(xp_ref[...])).astype(BF16)
        h_ref[HALO + tm:, :] = jnp.where(last, 0.0, nm(xn_ref[...])).astype(BF16)

    h = h_ref[...]
    ug_ref[...] = jnp.dot(h, wug_ref[...], preferred_element_type=F32)
    uv_ref[...] = jnp.dot(h, wuv_ref[...], preferred_element_type=F32)

    def conv(u_ref, cw_ref, cb_ref):
        cw = cw_ref[...]
        return (u_ref[HALO - 1:HALO - 1 + tm, :] * cw[0:1] + u_ref[HALO:HALO + tm, :] * cw[1:2]
                + u_ref[HALO + 1:HALO + 1 + tm, :] * cw[2:3] + cb_ref[...])

    g = conv(ug_ref, cwg_ref, cbg_ref)
    val = conv(uv_ref, cwv_ref, cbv_ref)
    a = (g * jax.nn.sigmoid(g) * val).astype(BF16)
    part = jnp.dot(a, wd_ref[...], preferred_element_type=F32)

    @pl.when(j == 0)
    def _():
        o_ref[...] = part

    @pl.when(j > 0)
    def _():
        o_ref[...] += part

    @pl.when(j == n_ff - 1)
    def _():
        o_ref[...] = x_ref[...] + gt_ref[...] * o_ref[...]


def _ffn(x2d, shift, scale, gate, norm_g, w_up, cw, cb, w_down, *, tm, seq_len):
    rows, d = x2d.shape
    tps = seq_len // tm
    nb = shift.shape[0]
    hb = tm // HALO
    n_halo = rows // HALO
    n_ff = D_FF // FF_TN

    def bidx(i):
        return (i // tps) % nb

    vec = pl.BlockSpec((None, 1, d), lambda i, j: (bidx(i), 0, 0))
    return pl.pallas_call(
        functools.partial(_ffn_kernel, tm=tm, tps=tps, n_ff=n_ff),
        grid=(rows // tm, n_ff),
        in_specs=[
            pl.BlockSpec((tm, d), lambda i, j: (i, 0)),
            pl.BlockSpec((HALO, d), lambda i, j: (jnp.maximum(i * hb - 1, 0), 0)),
            pl.BlockSpec((HALO, d), lambda i, j: (jnp.minimum((i + 1) * hb, n_halo - 1), 0)),
            vec, vec, vec,
            pl.BlockSpec((1, d), lambda i, j: (0, 0)),
            pl.BlockSpec((d, FF_TN), lambda i, j: (0, j)),
            pl.BlockSpec((d, FF_TN), lambda i, j: (0, n_ff + j)),
            pl.BlockSpec((3, FF_TN), lambda i, j: (0, j)),
            pl.BlockSpec((3, FF_TN), lambda i, j: (0, n_ff + j)),
            pl.BlockSpec((1, FF_TN), lambda i, j: (0, j)),
            pl.BlockSpec((1, FF_TN), lambda i, j: (0, n_ff + j)),
            pl.BlockSpec((FF_TN, d), lambda i, j: (j, 0)),
        ],
        out_specs=pl.BlockSpec((tm, d), lambda i, j: (i, 0)),
        out_shape=jax.ShapeDtypeStruct((rows, d), F32),
        scratch_shapes=[
            pltpu.VMEM((tm + 2 * HALO, d), BF16),
            pltpu.VMEM((tm + 2 * HALO, FF_TN), F32),
            pltpu.VMEM((tm + 2 * HALO, FF_TN), F32),
        ],
        compiler_params=pltpu.CompilerParams(
            dimension_semantics=("arbitrary", "arbitrary"), vmem_limit_bytes=VMEM_LIMIT),
        name="ffn",
    )(x2d, x2d, x2d, shift, scale, gate, norm_g, w_up, w_up, cw, cw, cb, cb, w_down)


def kernel(x, c, ctx, c_ctx, ada_w, ada_b, norm1_g, norm2_g, w_in, sgu_ln_g, sgu_w, sgu_b, q_norm_g, k_norm_g, lam_q1, lam_k1, lam_q2, lam_k2, subln_g, conv_w, w_br_a, w_br_b, w_br_c, w_out, ffn_up, ffn_conv_w, ffn_conv_b, ffn_down):
    batch, seq, d = x.shape
    ctx_len = ctx.shape[1]
    depth = ada_w.shape[0]
    assert d == D_MODEL and seq % GRID_W == 0

    tm_lat = min(1024, seq)
    tm_mrg = min(512, seq)
    tm_ctx = ctx_len
    tq_lat = min(256, seq)
    tk = 512
    assert seq % tm_lat == 0 and seq % tq_lat == 0 and ctx_len % HALO == 0

    perm = _proj_column_order()
    w_in_b = jnp.take(w_in, perm, axis=2).astype(BF16)
    wa_b, wb_b, wc_b, wo_b = (w.astype(BF16) for w in (w_br_a, w_br_b, w_br_c, w_out))
    up_b = ffn_up.astype(BF16)
    down_b = ffn_down.astype(BF16)
    sw_b = sgu_w.astype(BF16)
    sgu_bias = jnp.repeat(jnp.swapaxes(sgu_b, 1, 2), LANES, axis=2)
    gmat = jnp.asarray(np.kron(np.eye(PROJ_TN // B_QK_DIM), np.full((B_QK_DIM, B_QK_DIM), 1.0 / B_QK_DIM)), BF16)
    cos, sin = _rope_tables(seq)
    cos_ctx = jnp.ones((tm_ctx, LANES), F32)
    sin_ctx = jnp.zeros((tm_ctx, LANES), F32)
    lam_vecs = jnp.stack([lam_q1, lam_k1, lam_q2, lam_k2], axis=1).astype(F32)

    pad = (-(batch + 1)) % 8
    cc = jnp.concatenate([c, c_ctx[None, :], jnp.zeros((pad, d), F32)], axis=0)
    mod = _modulation(cc, ada_w, ada_b)
    mod = mod.reshape(depth, cc.shape[0], 6, 1, d)

    xs = x.reshape(batch * seq, d)
    cs = ctx.reshape(batch * ctx_len, d)

    for l in range(depth):
        last = l == depth - 1
        lam_init = 0.8 - 0.6 * math.exp(-0.3 * l)
        lat = [mod[l, :batch, k] for k in range(6)]
        cm = [mod[l, batch:batch + 1, k] for k in range(6)]
        n1 = norm1_g[l].reshape(1, d)
        n2 = norm2_g[l].reshape(1, d)
        qg = jnp.tile(q_norm_g[l], PROJ_TN // B_QK_DIM).reshape(1, PROJ_TN)
        kg = jnp.tile(k_norm_g[l], PROJ_TN // B_QK_DIM).reshape(1, PROJ_TN)
        lng = sgu_ln_g[l].reshape(1, A_WIDTH)
        subg = subln_g[l].reshape(1, B_V_DIM)

        z_lat = _proj(xs, lat[0], lat[1], n1, w_in_b[l], qg, kg, lng, cos, sin, gmat,
                      tm=tm_lat, seq_len=seq, use_rope=True)
        z_ctx = _proj(cs, cm[0], cm[1], n1, w_in_b[l], qg, kg, lng, cos_ctx, sin_ctx, gmat,
                      tm=tm_ctx, seq_len=ctx_len, use_rope=False)

        yb_lat = _attention(z_lat, [(z_lat, seq), (z_ctx, ctx_len)], lam_vecs[l], subg,
                            batch=batch, seq_len=seq, tq=tq_lat, tk=tk, lam_init=lam_init)
        merge_w = (sw_b[l], sgu_bias[l], conv_w[l], wa_b[l], wb_b[l], wc_b[l], wo_b[l])
        ffn_w = (up_b[l], ffn_conv_w[l], ffn_conv_b[l].reshape(1, 2 * D_FF), down_b[l])
        xs = _merge(z_lat, yb_lat, xs, lat[2], *merge_w, tm=tm_mrg, seq_len=seq)
        xs = _ffn(xs, lat[3], lat[4], lat[5], n2, *ffn_w, tm=tm_lat, seq_len=seq)

        if not last:
            yb_ctx = _attention(z_ctx, [(z_ctx, ctx_len)], lam_vecs[l], subg,
                                batch=batch, seq_len=ctx_len, tq=tm_ctx, tk=tk, lam_init=lam_init)
            cs = _merge(z_ctx, yb_ctx, cs, cm[2], *merge_w, tm=tm_ctx, seq_len=ctx_len)
            cs = _ffn(cs, cm[3], cm[4], cm[5], n2, *ffn_w, tm=tm_ctx, seq_len=ctx_len)

    return xs.reshape(batch, seq, d)
```

```python
import functools
import math

import jax
import jax.numpy as jnp
import numpy as np
from jax import lax
from jax.experimental import pallas as pl
from jax.experimental.pallas import tpu as pltpu

F32 = jnp.float32
BF16 = jnp.bfloat16

D_MODEL = 1024
GRID_W = 64
A_WIDTH = 512
A_GROUPS = 4
CHUNK = 128
B_HEADS = 8
B_QK_DIM = 64
B_V_DIM = 128
B_WIDTH = B_HEADS * B_V_DIM
ROPE_THETA = 10000.0
C_WIDTH = 512
N_BRANCH = 3
D_FF = 2816
EPS = 1e-6

_REF_A_U = 0
_REF_A_V = 512
_REF_B_Q = 1024
_REF_B_K = 2048
_REF_B_V = 3072
_REF_C_IN = 4096
_REF_GATE = 5632

PROJ_TN = 512
_J_U = 0
_J_V = 1
_J_Q = 2
_J_K = 4
_J_GATE = 6
_J_CIN = 12
_J_VALT = 15
_N_Z_TILES = 15
_N_PROJ_STEPS = 17
Z_COLS = _N_Z_TILES * PROJ_TN

HALO = 16
LANES = 128
SUBLANES = 8
FF_TN = 256
QK_SCALE_LOG2E = (B_QK_DIM ** -0.5) * math.log2(math.e)
SAFE_EXP2_RANGE = 60.0
VMEM_LIMIT = 56 * 1024 * 1024
_NT = (((1,), (1,)), ((), ()))


def _proj_column_order():
    idx = []
    idx += list(range(_REF_A_U, _REF_A_U + 512))
    idx += list(range(_REF_A_V, _REF_A_V + 512))
    for base in (_REF_B_Q, _REF_B_K):
        for h in range(B_HEADS):
            for a in range(2):
                start = base + a * B_HEADS * B_QK_DIM + h * B_QK_DIM
                idx += list(range(start, start + B_QK_DIM))
    idx += list(range(_REF_GATE, _REF_GATE + N_BRANCH * D_MODEL))
    idx += list(range(_REF_C_IN, _REF_C_IN + 3 * C_WIDTH))
    assert len(idx) == Z_COLS
    return np.asarray(idx, dtype=np.int32)


def _rope_tables(n_tokens):
    rows = n_tokens // GRID_W
    row = jnp.repeat(jnp.arange(rows, dtype=F32), GRID_W)
    col = jnp.tile(jnp.arange(GRID_W, dtype=F32), rows)
    n_freq = B_QK_DIM // 4
    inv = ROPE_THETA ** (-jnp.arange(n_freq, dtype=F32) / n_freq)
    ang_r = row[:, None] * inv
    ang_c = col[:, None] * inv
    cos64 = jnp.concatenate([jnp.cos(ang_r), jnp.cos(ang_r), jnp.cos(ang_c), jnp.cos(ang_c)], axis=1)
    sin64 = jnp.concatenate([-jnp.sin(ang_r), jnp.sin(ang_r), -jnp.sin(ang_c), jnp.sin(ang_c)], axis=1)
    return jnp.tile(cos64, (1, 2)), jnp.tile(sin64, (1, 2))


def _mod_kernel(cc_ref, w_ref, b_ref, o_ref):
    cc = cc_ref[...]
    s = cc * jax.nn.sigmoid(cc)
    o_ref[...] = jnp.dot(s, w_ref[...], preferred_element_type=F32,
                         precision=lax.Precision.HIGHEST) + b_ref[...]


def _modulation(cc, ada_w, ada_b):
    depth, d, n = ada_w.shape
    tn = 1536
    rows = cc.shape[0]
    return pl.pallas_call(
        _mod_kernel,
        grid=(depth, n // tn),
        in_specs=[
            pl.BlockSpec((rows, d), lambda l, j: (0, 0)),
            pl.BlockSpec((None, d, tn), lambda l, j: (l, 0, j)),
            pl.BlockSpec((None, 1, tn), lambda l, j: (l, 0, j)),
        ],
        out_specs=pl.BlockSpec((None, rows, tn), lambda l, j: (l, 0, j)),
        out_shape=jax.ShapeDtypeStruct((depth, rows, n), F32),
        compiler_params=pltpu.CompilerParams(
            dimension_semantics=("arbitrary", "arbitrary"), vmem_limit_bytes=VMEM_LIMIT),
        name="modulation",
    )(cc, ada_w, ada_b.reshape(depth, 1, n))


def _norm_modulate(x, gain, shift, scale):
    ms = jnp.mean(x * x, axis=-1, keepdims=True)
    y = x * lax.rsqrt(ms + EPS) * gain
    return y * (1.0 + scale) + shift


def _proj_kernel(x_ref, sh_ref, sc_ref, ng_ref, w_ref, wvt_ref, qg_ref, kg_ref, lng_ref,
                 cos_ref, sin_ref, gmat_ref, o_ref, vt_ref, h_ref, *, tm, use_rope):
    j = pl.program_id(1)

    @pl.when(j == 0)
    def _():
        h_ref[...] = _norm_modulate(x_ref[...], ng_ref[...], sh_ref[...], sc_ref[...]).astype(BF16)

    def project():
        return jnp.dot(h_ref[...], w_ref[...], preferred_element_type=F32)

    @pl.when(j == _J_U)
    def _():
        o_ref[...] = jax.nn.gelu(project()).astype(BF16)

    @pl.when(j == _J_V)
    def _():
        g = jax.nn.gelu(project())
        mu = jnp.mean(g, axis=-1, keepdims=True)
        c = g - mu
        var = jnp.mean(c * c, axis=-1, keepdims=True)
        o_ref[...] = (c * lax.rsqrt(var + EPS) * lng_ref[...]).astype(BF16)

    def qk_epilogue(gain, scale):
        acc = project()
        ms = jnp.dot((acc * acc).astype(BF16), gmat_ref[...], preferred_element_type=F32)
        y = acc * lax.rsqrt(ms + EPS) * gain
        if not use_rope:
            o_ref[...] = (y * scale).astype(BF16)
            return
        lane = lax.broadcasted_iota(jnp.int32, (tm, LANES), 1)
        first_half = (lane & 16) == 0
        cos = cos_ref[...]
        sin = sin_ref[...]
        for c in range(PROJ_TN // LANES):
            yc = y[:, c * LANES:(c + 1) * LANES]
            partner = jnp.where(first_half, pltpu.roll(yc, LANES - 16, 1), pltpu.roll(yc, 16, 1))
            o_ref[:, c * LANES:(c + 1) * LANES] = ((yc * cos + partner * sin) * scale).astype(BF16)

    @pl.when((j >= _J_Q) & (j < _J_K))
    def _():
        qk_epilogue(qg_ref[...], QK_SCALE_LOG2E)

    @pl.when((j >= _J_K) & (j < _J_GATE))
    def _():
        qk_epilogue(kg_ref[...], 1.0)

    @pl.when((j >= _J_GATE) & (j < _J_CIN))
    def _():
        o_ref[...] = jax.nn.sigmoid(project()).astype(BF16)

    @pl.when((j >= _J_CIN) & (j < _J_VALT))
    def _():
        o_ref[...] = project().astype(BF16)

    @pl.when(j >= _J_VALT)
    def _():
        vt_ref[...] = lax.dot_general(wvt_ref[...], h_ref[...], _NT, preferred_element_type=F32).astype(BF16)


def _proj(x2d, shift, scale, norm_g, w, wvt, qg, kg, lng, cos, sin, gmat, *, tm, seq_len, use_rope):
    rows, d = x2d.shape
    tps = seq_len // tm
    nb = shift.shape[0]
    pos_tiles = cos.shape[0] // tm

    def bidx(i):
        return (i // tps) % nb

    def zcol(j):
        return jnp.minimum(j, _N_Z_TILES - 1)

    def vrow(j):
        return jnp.maximum(j - _J_VALT, 0)

    return pl.pallas_call(
        functools.partial(_proj_kernel, tm=tm, use_rope=use_rope),
        grid=(rows // tm, _N_PROJ_STEPS),
        in_specs=[
            pl.BlockSpec((tm, d), lambda i, j: (i, 0)),
            pl.BlockSpec((None, 1, d), lambda i, j: (bidx(i), 0, 0)),
            pl.BlockSpec((None, 1, d), lambda i, j: (bidx(i), 0, 0)),
            pl.BlockSpec((1, d), lambda i, j: (0, 0)),
            pl.BlockSpec((d, PROJ_TN), lambda i, j: (0, zcol(j))),
            pl.BlockSpec((PROJ_TN, d), lambda i, j: (vrow(j), 0)),
            pl.BlockSpec((1, PROJ_TN), lambda i, j: (0, 0)),
            pl.BlockSpec((1, PROJ_TN), lambda i, j: (0, 0)),
            pl.BlockSpec((1, PROJ_TN), lambda i, j: (0, 0)),
            pl.BlockSpec((tm, LANES), lambda i, j: (i % pos_tiles, 0)),
            pl.BlockSpec((tm, LANES), lambda i, j: (i % pos_tiles, 0)),
            pl.BlockSpec((PROJ_TN, PROJ_TN), lambda i, j: (0, 0)),
        ],
        out_specs=[
            pl.BlockSpec((tm, PROJ_TN), lambda i, j: (i, zcol(j))),
            pl.BlockSpec((PROJ_TN, tm), lambda i, j: (vrow(j), i)),
        ],
        out_shape=[
            jax.ShapeDtypeStruct((rows, Z_COLS), BF16),
            jax.ShapeDtypeStruct((B_WIDTH, rows), BF16),
        ],
        scratch_shapes=[pltpu.VMEM((tm, d), BF16)],
        compiler_params=pltpu.CompilerParams(
            dimension_semantics=("arbitrary", "arbitrary"), vmem_limit_bytes=VMEM_LIMIT),
        name="proj",
    )(x2d, shift, scale, norm_g, w, wvt, qg, kg, lng, cos, sin, gmat)


def _attn_kernel(*refs, tq, tk, src_rows, lam_init):
    n_src = len(src_rows)
    q_ref = refs[0]
    kv_refs = [(refs[1 + 2 * s], refs[2 + 2 * s]) for s in range(n_src)]
    lam_ref, subg_ref, o_ref, kmax_ref, p1_ref, p2_ref = refs[1 + 2 * n_src:]
    qi = pl.program_id(2)

    lane_k = lax.broadcasted_iota(jnp.int32, (1, LANES), 1)
    map1 = lane_k < B_QK_DIM

    @pl.when(qi == 0)
    def _():
        best1 = jnp.zeros((1, 1), F32)
        best2 = jnp.zeros((1, 1), F32)
        for k_ref, _ in kv_refs:
            kf = k_ref[...].astype(F32)
            sq = kf * kf
            n1 = jnp.sum(jnp.where(map1, sq, 0.0), axis=1, keepdims=True)
            n2 = jnp.sum(jnp.where(map1, 0.0, sq), axis=1, keepdims=True)
            best1 = jnp.maximum(best1, jnp.max(n1, axis=0, keepdims=True))
            best2 = jnp.maximum(best2, jnp.max(n2, axis=0, keepdims=True))
        kmax_ref[0:1, :] = jnp.broadcast_to(best1, (1, LANES))
        kmax_ref[1:2, :] = jnp.broadcast_to(best2, (1, LANES))

    q = q_ref[...]
    zero = jnp.zeros_like(q)
    q1 = jnp.where(map1, q, zero)
    q2 = jnp.where(map1, zero, q)

    sel_row = lax.broadcasted_iota(jnp.int32, (SUBLANES, LANES), 0)
    sel_lane = lax.broadcasted_iota(jnp.int32, (SUBLANES, LANES), 1)
    sel = jnp.where(((sel_row == 0) & (sel_lane < B_QK_DIM)) | ((sel_row == 1) & (sel_lane >= B_QK_DIM)),
                    1.0, 0.0).astype(BF16)
    qf = q.astype(F32)
    qn = lax.dot_general(sel, (qf * qf).astype(BF16), _NT, preferred_element_type=F32)
    bound1 = jnp.sqrt(qn[0:1] * kmax_ref[0:1, 0:1])
    bound2 = jnp.sqrt(qn[1:2] * kmax_ref[1:2, 0:1])
    worst = jnp.max(jnp.maximum(bound1, bound2))

    def scores_t(qm, k_ref, start, size):
        kb = k_ref[start:start + size, :]
        return lax.dot_general(kb, qm, _NT, preferred_element_type=F32)

    src_offsets = [sum(src_rows[:s]) for s in range(n_src)]

    def blocks(fn, carry):
        for (k_ref, _), rows, offset in zip(kv_refs, src_rows, src_offsets):
            size = min(tk, rows)
            for b in range(rows // size):
                carry = fn(k_ref, b * size, size, offset + b * size, carry)
        return carry

    def fold(p, op):
        acc = p[0:SUBLANES, :]
        for r in range(1, p.shape[0] // SUBLANES):
            acc = op(acc, p[r * SUBLANES:(r + 1) * SUBLANES, :])
        return acc

    def finish(shift1, shift2):
        def step(k_ref, start, size, base, carry):
            l1, l2 = carry
            p1 = jnp.exp2(scores_t(q1, k_ref, start, size) - shift1)
            p2 = jnp.exp2(scores_t(q2, k_ref, start, size) - shift2)
            p1_ref[base:base + size, :] = p1.astype(BF16)
            p2_ref[base:base + size, :] = p2.astype(BF16)
            return l1 + fold(p1, jnp.add), l2 + fold(p2, jnp.add)

        zl = jnp.zeros((SUBLANES, tq), F32)
        l1, l2 = blocks(step, (zl, zl))

        o1 = jnp.zeros((B_V_DIM, tq), F32)
        o2 = jnp.zeros((B_V_DIM, tq), F32)
        for (_, vt_ref), rows, base in zip(kv_refs, src_rows, src_offsets):
            vt = vt_ref[...]
            o1 = o1 + jnp.dot(vt, p1_ref[base:base + rows, :], preferred_element_type=F32)
            o2 = o2 + jnp.dot(vt, p2_ref[base:base + rows, :], preferred_element_type=F32)

        r1 = 1.0 / jnp.sum(l1, axis=0, keepdims=True)
        r2 = 1.0 / jnp.sum(l2, axis=0, keepdims=True)
        lv = lam_ref[...]
        lam = (jnp.exp(jnp.sum(lv[0:1] * lv[1:2], axis=1, keepdims=True))
               - jnp.exp(jnp.sum(lv[2:3] * lv[3:4], axis=1, keepdims=True)) + lam_init)
        a = o1 * r1 - o2 * (lam * r2)
        ms = jnp.mean(a * a, axis=0, keepdims=True)
        y = a * lax.rsqrt(ms + EPS) * subg_ref[...] * (1.0 - lam_init)
        o_ref[...] = y.T.astype(BF16)

    @pl.when(worst < SAFE_EXP2_RANGE)
    def _():
        finish(bound1, bound2)

    @pl.when(jnp.logical_not(worst < SAFE_EXP2_RANGE))
    def _():
        def mx_step(k_ref, start, size, base, carry):
            m1, m2 = carry
            m1 = jnp.maximum(m1, fold(scores_t(q1, k_ref, start, size), jnp.maximum))
            m2 = jnp.maximum(m2, fold(scores_t(q2, k_ref, start, size), jnp.maximum))
            return m1, m2

        neg = jnp.full((SUBLANES, tq), -jnp.inf, F32)
        m1, m2 = blocks(mx_step, (neg, neg))
        finish(jnp.max(m1, axis=0, keepdims=True), jnp.max(m2, axis=0, keepdims=True))


def _attention(zq, kv_sources, lam_vecs, subg, *, batch, seq_len, tq, tk, lam_init):
    rows = zq.shape[0]
    nq = seq_len // tq
    q_col = _J_Q * PROJ_TN // LANES
    k_col = _J_K * PROJ_TN // LANES

    in_specs = [pl.BlockSpec((tq, LANES), lambda b, h, i: (b * nq + i, q_col + h))]
    args = [zq]
    src_rows = []
    for z, vt, n in kv_sources:
        in_specs.append(pl.BlockSpec((n, LANES), lambda b, h, i: (b, k_col + h)))
        in_specs.append(pl.BlockSpec((B_V_DIM, n), lambda b, h, i: (h, b)))
        args += [z, vt]
        src_rows.append(n)
    in_specs.append(pl.BlockSpec((4, B_QK_DIM), lambda b, h, i: (0, 0)))
    in_specs.append(pl.BlockSpec((B_V_DIM, 1), lambda b, h, i: (0, 0)))
    args += [lam_vecs, subg]

    return pl.pallas_call(
        functools.partial(_attn_kernel, tq=tq, tk=tk, src_rows=tuple(src_rows), lam_init=lam_init),
        grid=(batch, B_HEADS, nq),
        in_specs=in_specs,
        out_specs=pl.BlockSpec((tq, LANES), lambda b, h, i: (b * nq + i, h)),
        out_shape=jax.ShapeDtypeStruct((rows, B_WIDTH), BF16),
        scratch_shapes=[
            pltpu.VMEM((SUBLANES, LANES), F32),
            pltpu.VMEM((sum(src_rows), tq), BF16),
            pltpu.VMEM((sum(src_rows), tq), BF16),
        ],
        compiler_params=pltpu.CompilerParams(
            dimension_semantics=("arbitrary", "arbitrary", "arbitrary"), vmem_limit_bytes=VMEM_LIMIT),
        name="attn",
    )(*args)


def _merge_kernel(u_ref, v_ref, yb_ref, ga_ref, gb_ref, gc_ref, cb_ref, cg_ref, cx_ref,
                  pg_ref, px_ref, ng_ref, nx_ref, x_ref, g1_ref,
                  sw_ref, sb_ref, cw_ref, wa_ref, wb_ref, wc_ref, wo_ref,
                  o_ref, ya_ref, *, tm, tps):
    i = pl.program_id(0)
    first = (i % tps) == 0
    last = (i % tps) == (tps - 1)

    for c in range(tm // CHUNK):
        rs = slice(c * CHUNK, (c + 1) * CHUNK)
        for g in range(A_GROUPS):
            cs = slice(g * LANES, (g + 1) * LANES)
            mixed = jnp.dot(sw_ref[g], v_ref[rs, cs], preferred_element_type=F32) + sb_ref[:, cs]
            ya_ref[rs, cs] = (u_ref[rs, cs].astype(F32) * mixed).astype(BF16)

    t = cg_ref[...].astype(F32) * cx_ref[...].astype(F32)
    t_prev_row = pg_ref[HALO - 1:HALO, :].astype(F32) * px_ref[HALO - 1:HALO, :].astype(F32)
    t_next_row = ng_ref[0:1, :].astype(F32) * nx_ref[0:1, :].astype(F32)
    t_prev_row = jnp.where(first, 0.0, t_prev_row)
    t_next_row = jnp.where(last, 0.0, t_next_row)
    row = lax.broadcasted_iota(jnp.int32, (tm, C_WIDTH), 0)
    t_prev = jnp.where(row == 0, t_prev_row, pltpu.roll(t, 1, 0))
    t_next = jnp.where(row == tm - 1, t_next_row, pltpu.roll(t, tm - 1, 0))
    cw = cw_ref[...]
    y_c = cb_ref[...].astype(F32) * (t_prev * cw[0:1] + t * cw[1:2] + t_next * cw[2:3])

    m = ga_ref[...].astype(F32) * jnp.dot(ya_ref[...], wa_ref[...], preferred_element_type=F32)
    m = m + gb_ref[...].astype(F32) * jnp.dot(yb_ref[...], wb_ref[...], preferred_element_type=F32)
    m = m + gc_ref[...].astype(F32) * jnp.dot(y_c.astype(BF16), wc_ref[...], preferred_element_type=F32)
    out = jnp.dot(m.astype(BF16), wo_ref[...], preferred_element_type=F32)
    o_ref[...] = x_ref[...] + g1_ref[...] * out


def _merge(z, yb, x2d, g1, sw, sb, cw, wa, wb, wc, wo, *, tm, seq_len):
    rows, d = x2d.shape
    tps = seq_len // tm
    nb = g1.shape[0]
    hb = tm // HALO
    n_halo = rows // HALO
    gate_col = _J_GATE * PROJ_TN // D_MODEL

    def bidx(i):
        return (i // tps) % nb

    def prev_blk(i):
        return jnp.maximum(i * hb - 1, 0)

    def next_blk(i):
        return jnp.minimum((i + 1) * hb, n_halo - 1)

    const = lambda i: (0, 0)
    return pl.pallas_call(
        functools.partial(_merge_kernel, tm=tm, tps=tps),
        grid=(rows // tm,),
        in_specs=[
            pl.BlockSpec((tm, A_WIDTH), lambda i: (i, _J_U)),
            pl.BlockSpec((tm, A_WIDTH), lambda i: (i, _J_V)),
            pl.BlockSpec((tm, B_WIDTH), lambda i: (i, 0)),
            pl.BlockSpec((tm, d), lambda i: (i, gate_col)),
            pl.BlockSpec((tm, d), lambda i: (i, gate_col + 1)),
            pl.BlockSpec((tm, d), lambda i: (i, gate_col + 2)),
            pl.BlockSpec((tm, C_WIDTH), lambda i: (i, _J_CIN)),
            pl.BlockSpec((tm, C_WIDTH), lambda i: (i, _J_CIN + 1)),
            pl.BlockSpec((tm, C_WIDTH), lambda i: (i, _J_CIN + 2)),
            pl.BlockSpec((HALO, C_WIDTH), lambda i: (prev_blk(i), _J_CIN + 1)),
            pl.BlockSpec((HALO, C_WIDTH), lambda i: (prev_blk(i), _J_CIN + 2)),
            pl.BlockSpec((HALO, C_WIDTH), lambda i: (next_blk(i), _J_CIN + 1)),
            pl.BlockSpec((HALO, C_WIDTH), lambda i: (next_blk(i), _J_CIN + 2)),
            pl.BlockSpec((tm, d), lambda i: (i, 0)),
            pl.BlockSpec((None, 1, d), lambda i: (bidx(i), 0, 0)),
            pl.BlockSpec((A_GROUPS, CHUNK, CHUNK), lambda i: (0, 0, 0)),
            pl.BlockSpec((CHUNK, A_WIDTH), const),
            pl.BlockSpec((3, C_WIDTH), const),
            pl.BlockSpec((A_WIDTH, d), const),
            pl.BlockSpec((B_WIDTH, d), const),
            pl.BlockSpec((C_WIDTH, d), const),
            pl.BlockSpec((d, d), const),
        ],
        out_specs=pl.BlockSpec((tm, d), lambda i: (i, 0)),
        out_shape=jax.ShapeDtypeStruct((rows, d), F32),
        scratch_shapes=[pltpu.VMEM((tm, A_WIDTH), BF16)],
        compiler_params=pltpu.CompilerParams(
            dimension_semantics=("arbitrary",), vmem_limit_bytes=VMEM_LIMIT),
        name="merge",
    )(z, z, yb, z, z, z, z, z, z, z, z, z, z, x2d, g1, sw, sb, cw, wa, wb, wc, wo)


def _ffn_kernel(x_ref, xp_ref, xn_ref, sh_ref, sc_ref, gt_ref, ng_ref,
                wug_ref, wuv_ref, cwg_ref, cwv_ref, cbg_ref, cbv_ref, wd_ref,
                o_ref, h_ref, ug_ref, uv_ref, *, tm, tps, n_ff):
    i = pl.program_id(0)
    j = pl.program_id(1)

    @pl.when(j == 0)
    def _():
        def nm(x):
            return _norm_modulate(x, ng_ref[...], sh_ref[...], sc_ref[...])

        first = (i % tps) == 0
        last = (i % tps) == (tps - 1)
        h_ref[HALO:HALO + tm, :] = nm(x_ref[...]).astype(BF16)
        h_ref[0:HALO, :] = jnp.where(first, 0.0, nm(xp_ref[...])).astype(BF16)
        h_ref[HALO + tm:, :] = jnp.where(last, 0.0, nm(xn_ref[...])).astype(BF16)

    h = h_ref[...]
    ug_ref[...] = jnp.dot(h, wug_ref[...], preferred_element_type=F32)
    uv_ref[...] = jnp.dot(h, wuv_ref[...], preferred_element_type=F32)

    def conv(u_ref, cw_ref, cb_ref):
        cw = cw_ref[...]
        return (u_ref[HALO - 1:HALO - 1 + tm, :] * cw[0:1] + u_ref[HALO:HALO + tm, :] * cw[1:2]
                + u_ref[HALO + 1:HALO + 1 + tm, :] * cw[2:3] + cb_ref[...])

    g = conv(ug_ref, cwg_ref, cbg_ref)
    val = conv(uv_ref, cwv_ref, cbv_ref)
    a = (g * jax.nn.sigmoid(g) * val).astype(BF16)
    part = jnp.dot(a, wd_ref[...], preferred_element_type=F32)

    @pl.when(j == 0)
    def _():
        o_ref[...] = part

    @pl.when(j > 0)
    def _():
        o_ref[...] += part

    @pl.when(j == n_ff - 1)
    def _():
        o_ref[...] = x_ref[...] + gt_ref[...] * o_ref[...]


def _ffn(x2d, shift, scale, gate, norm_g, w_up, cw, cb, w_down, *, tm, seq_len):
    rows, d = x2d.shape
    tps = seq_len // tm
    nb = shift.shape[0]
    hb = tm // HALO
    n_halo = rows // HALO
    n_ff = D_FF // FF_TN

    def bidx(i):
        return (i // tps) % nb

    vec = pl.BlockSpec((None, 1, d), lambda i, j: (bidx(i), 0, 0))
    return pl.pallas_call(
        functools.partial(_ffn_kernel, tm=tm, tps=tps, n_ff=n_ff),
        grid=(rows // tm, n_ff),
        in_specs=[
            pl.BlockSpec((tm, d), lambda i, j: (i, 0)),
            pl.BlockSpec((HALO, d), lambda i, j: (jnp.maximum(i * hb - 1, 0), 0)),
            pl.BlockSpec((HALO, d), lambda i, j: (jnp.minimum((i + 1) * hb, n_halo - 1), 0)),
            vec, vec, vec,
            pl.BlockSpec((1, d), lambda i, j: (0, 0)),
            pl.BlockSpec((d, FF_TN), lambda i, j: (0, j)),
            pl.BlockSpec((d, FF_TN), lambda i, j: (0, n_ff + j)),
            pl.BlockSpec((3, FF_TN), lambda i, j: (0, j)),
            pl.BlockSpec((3, FF_TN), lambda i, j: (0, n_ff + j)),
            pl.BlockSpec((1, FF_TN), lambda i, j: (0, j)),
            pl.BlockSpec((1, FF_TN), lambda i, j: (0, n_ff + j)),
            pl.BlockSpec((FF_TN, d), lambda i, j: (j, 0)),
        ],
        out_specs=pl.BlockSpec((tm, d), lambda i, j: (i, 0)),
        out_shape=jax.ShapeDtypeStruct((rows, d), F32),
        scratch_shapes=[
            pltpu.VMEM((tm + 2 * HALO, d), BF16),
            pltpu.VMEM((tm + 2 * HALO, FF_TN), F32),
            pltpu.VMEM((tm + 2 * HALO, FF_TN), F32),
        ],
        compiler_params=pltpu.CompilerParams(
            dimension_semantics=("arbitrary", "arbitrary"), vmem_limit_bytes=VMEM_LIMIT),
        name="ffn",
    )(x2d, x2d, x2d, shift, scale, gate, norm_g, w_up, w_up, cw, cw, cb, cb, w_down)


def kernel(x, c, ctx, c_ctx, ada_w, ada_b, norm1_g, norm2_g, w_in, sgu_ln_g, sgu_w, sgu_b, q_norm_g, k_norm_g, lam_q1, lam_k1, lam_q2, lam_k2, subln_g, conv_w, w_br_a, w_br_b, w_br_c, w_out, ffn_up, ffn_conv_w, ffn_conv_b, ffn_down):
    batch, seq, d = x.shape
    ctx_len = ctx.shape[1]
    depth = ada_w.shape[0]
    assert d == D_MODEL and seq % GRID_W == 0

    tm_lat = min(1024, seq)
    tm_mrg = min(512, seq)
    tm_ctx = ctx_len
    tq_lat = min(256, seq)
    tk = 512
    assert seq % tm_lat == 0 and seq % tq_lat == 0 and ctx_len % HALO == 0

    perm = _proj_column_order()
    w_in_b = jnp.take(w_in, perm, axis=2).astype(BF16)
    wvt_b = jnp.swapaxes(w_in[:, :, _REF_B_V:_REF_B_V + B_WIDTH], 1, 2).astype(BF16)
    wa_b, wb_b, wc_b, wo_b = (w.astype(BF16) for w in (w_br_a, w_br_b, w_br_c, w_out))
    up_b = ffn_up.astype(BF16)
    down_b = ffn_down.astype(BF16)
    sw_b = sgu_w.astype(BF16)
    sgu_bias = jnp.repeat(jnp.swapaxes(sgu_b, 1, 2), LANES, axis=2)
    gmat = jnp.asarray(np.kron(np.eye(PROJ_TN // B_QK_DIM), np.full((B_QK_DIM, B_QK_DIM), 1.0 / B_QK_DIM)), BF16)
    cos, sin = _rope_tables(seq)
    cos_ctx = jnp.ones((tm_ctx, LANES), F32)
    sin_ctx = jnp.zeros((tm_ctx, LANES), F32)
    lam_vecs = jnp.stack([lam_q1, lam_k1, lam_q2, lam_k2], axis=1).astype(F32)

    pad = (-(batch + 1)) % 8
    cc = jnp.concatenate([c, c_ctx[None, :], jnp.zeros((pad, d), F32)], axis=0)
    mod = _modulation(cc, ada_w, ada_b)
    mod = mod.reshape(depth, cc.shape[0], 6, 1, d)

    xs = x.reshape(batch * seq, d)
    cs = ctx.reshape(batch * ctx_len, d)

    for l in range(depth):
        last = l == depth - 1
        lam_init = 0.8 - 0.6 * math.exp(-0.3 * l)
        lat = [mod[l, :batch, k] for k in range(6)]
        cm = [mod[l, batch:batch + 1, k] for k in range(6)]
        n1 = norm1_g[l].reshape(1, d)
        n2 = norm2_g[l].reshape(1, d)
        qg = jnp.tile(q_norm_g[l], PROJ_TN // B_QK_DIM).reshape(1, PROJ_TN)
        kg = jnp.tile(k_norm_g[l], PROJ_TN // B_QK_DIM).reshape(1, PROJ_TN)
        lng = sgu_ln_g[l].reshape(1, A_WIDTH)
        subg = subln_g[l].reshape(B_V_DIM, 1)

        z_lat, vt_lat = _proj(xs, lat[0], lat[1], n1, w_in_b[l], wvt_b[l], qg, kg, lng, cos, sin, gmat,
                              tm=tm_lat, seq_len=seq, use_rope=True)
        z_ctx, vt_ctx = _proj(cs, cm[0], cm[1], n1, w_in_b[l], wvt_b[l], qg, kg, lng, cos_ctx, sin_ctx, gmat,
                              tm=tm_ctx, seq_len=ctx_len, use_rope=False)

        yb_lat = _attention(z_lat, [(z_lat, vt_lat, seq), (z_ctx, vt_ctx, ctx_len)], lam_vecs[l], subg,
                            batch=batch, seq_len=seq, tq=tq_lat, tk=tk, lam_init=lam_init)
        merge_w = (sw_b[l], sgu_bias[l], conv_w[l], wa_b[l], wb_b[l], wc_b[l], wo_b[l])
        ffn_w = (up_b[l], ffn_conv_w[l], ffn_conv_b[l].reshape(1, 2 * D_FF), down_b[l])
        xs = _merge(z_lat, yb_lat, xs, lat[2], *merge_w, tm=tm_mrg, seq_len=seq)
        xs = _ffn(xs, lat[3], lat[4], lat[5], n2, *ffn_w, tm=tm_lat, seq_len=seq)

        if not last:
            yb_ctx = _attention(z_ctx, [(z_ctx, vt_ctx, ctx_len)], lam_vecs[l], subg,
                                batch=batch, seq_len=ctx_len, tq=tm_ctx, tk=tk, lam_init=lam_init)
            cs = _merge(z_ctx, yb_ctx, cs, cm[2], *merge_w, tm=tm_ctx, seq_len=ctx_len)
            cs = _ffn(cs, cm[3], cm[4], cm[5], n2, *ffn_w, tm=tm_ctx, seq_len=ctx_len)

    return xs.reshape(batch, seq, d)
```

```python
import functools
import math

import jax
import jax.numpy as jnp
import numpy as np
from jax import lax
from jax.experimental import pallas as pl
from jax.experimental.pallas import tpu as pltpu

F32 = jnp.float32
BF16 = jnp.bfloat16

D_MODEL = 1024
GRID_W = 64
A_WIDTH = 512
A_GROUPS = 4
CHUNK = 128
B_HEADS = 8
B_QK_DIM = 64
B_V_DIM = 128
B_WIDTH = B_HEADS * B_V_DIM
ROPE_THETA = 10000.0
C_WIDTH = 512
N_BRANCH = 3
D_FF = 2816
EPS = 1e-6

_REF_A_U = 0
_REF_A_V = 512
_REF_B_Q = 1024
_REF_B_K = 2048
_REF_B_V = 3072
_REF_C_IN = 4096
_REF_GATE = 5632

PROJ_TN = 512
_J_U = 0
_J_V = 1
_J_Q = 2
_J_K = 4
_J_GATE = 6
_J_CIN = 12
_J_VALT = 15
_N_Z_TILES = 15
_N_PROJ_STEPS = 17
Z_COLS = _N_Z_TILES * PROJ_TN

HALO = 16
LANES = 128
SUBLANES = 8
FF_TN = 256
QK_SCALE_LOG2E = (B_QK_DIM ** -0.5) * math.log2(math.e)
SAFE_EXP2_RANGE = 60.0
VMEM_LIMIT = 56 * 1024 * 1024
_NT = (((1,), (1,)), ((), ()))


def _proj_column_order():
    idx = []
    idx += list(range(_REF_A_U, _REF_A_U + 512))
    idx += list(range(_REF_A_V, _REF_A_V + 512))
    for base in (_REF_B_Q, _REF_B_K):
        for h in range(B_HEADS):
            for a in range(2):
                start = base + a * B_HEADS * B_QK_DIM + h * B_QK_DIM
                idx += list(range(start, start + B_QK_DIM))
    idx += list(range(_REF_GATE, _REF_GATE + N_BRANCH * D_MODEL))
    idx += list(range(_REF_C_IN, _REF_C_IN + 3 * C_WIDTH))
    assert len(idx) == Z_COLS
    return np.asarray(idx, dtype=np.int32)


def _rope_tables(n_tokens):
    rows = n_tokens // GRID_W
    row = jnp.repeat(jnp.arange(rows, dtype=F32), GRID_W)
    col = jnp.tile(jnp.arange(GRID_W, dtype=F32), rows)
    n_freq = B_QK_DIM // 4
    inv = ROPE_THETA ** (-jnp.arange(n_freq, dtype=F32) / n_freq)
    ang_r = row[:, None] * inv
    ang_c = col[:, None] * inv
    cos64 = jnp.concatenate([jnp.cos(ang_r), jnp.cos(ang_r), jnp.cos(ang_c), jnp.cos(ang_c)], axis=1)
    sin64 = jnp.concatenate([-jnp.sin(ang_r), jnp.sin(ang_r), -jnp.sin(ang_c), jnp.sin(ang_c)], axis=1)
    return jnp.tile(cos64, (1, 2)), jnp.tile(sin64, (1, 2))


def _mod_kernel(cc_ref, w_ref, b_ref, o_ref):
    cc = cc_ref[...]
    s = cc * jax.nn.sigmoid(cc)
    o_ref[...] = jnp.dot(s, w_ref[...], preferred_element_type=F32,
                         precision=lax.Precision.HIGHEST) + b_ref[...]


def _modulation(cc, ada_w, ada_b):
    depth, d, n = ada_w.shape
    tn = 1536
    rows = cc.shape[0]
    return pl.pallas_call(
        _mod_kernel,
        grid=(depth, n // tn),
        in_specs=[
            pl.BlockSpec((rows, d), lambda l, j: (0, 0)),
            pl.BlockSpec((None, d, tn), lambda l, j: (l, 0, j)),
            pl.BlockSpec((None, 1, tn), lambda l, j: (l, 0, j)),
        ],
        out_specs=pl.BlockSpec((None, rows, tn), lambda l, j: (l, 0, j)),
        out_shape=jax.ShapeDtypeStruct((depth, rows, n), F32),
        compiler_params=pltpu.CompilerParams(
            dimension_semantics=("arbitrary", "arbitrary"), vmem_limit_bytes=VMEM_LIMIT),
        name="modulation",
    )(cc, ada_w, ada_b.reshape(depth, 1, n))


def _norm_modulate(x, gain, shift, scale):
    ms = jnp.mean(x * x, axis=-1, keepdims=True)
    y = x * lax.rsqrt(ms + EPS) * gain
    return y * (1.0 + scale) + shift


def _sigmoid(x):
    return 0.5 * jnp.tanh(0.5 * x) + 0.5


def _proj_kernel(x_ref, sh_ref, sc_ref, ng_ref, w_ref, wvt_ref, qg_ref, kg_ref, lng_ref,
                 cos_ref, sin_ref, gmat_ref, o_ref, vt_ref, h_ref, *, tm, use_rope):
    h_ref[...] = _norm_modulate(x_ref[...], ng_ref[...], sh_ref[...], sc_ref[...]).astype(BF16)

    def qk_epilogue(acc, cols, gain, scale):
        ms = jnp.dot((acc * acc).astype(BF16), gmat_ref[...], preferred_element_type=F32)
        y = acc * lax.rsqrt(ms + EPS) * gain
        if not use_rope:
            o_ref[:, cols] = (y * scale).astype(BF16)
            return
        lane = lax.broadcasted_iota(jnp.int32, (tm, LANES), 1)
        first_half = (lane & 16) == 0
        cos = cos_ref[...]
        sin = sin_ref[...]
        for c in range(PROJ_TN // LANES):
            yc = y[:, c * LANES:(c + 1) * LANES]
            partner = jnp.where(first_half, pltpu.roll(yc, LANES - 16, 1), pltpu.roll(yc, 16, 1))
            lo = cols.start + c * LANES
            o_ref[:, lo:lo + LANES] = ((yc * cos + partner * sin) * scale).astype(BF16)

    for j in range(_N_Z_TILES):
        cols = slice(j * PROJ_TN, (j + 1) * PROJ_TN)
        acc = jnp.dot(h_ref[...], w_ref[:, cols], preferred_element_type=F32)
        if j == _J_U:
            o_ref[:, cols] = jax.nn.gelu(acc).astype(BF16)
        elif j == _J_V:
            g = jax.nn.gelu(acc)
            mu = jnp.mean(g, axis=-1, keepdims=True)
            c = g - mu
            var = jnp.mean(c * c, axis=-1, keepdims=True)
            o_ref[:, cols] = (c * lax.rsqrt(var + EPS) * lng_ref[...]).astype(BF16)
        elif j < _J_K:
            qk_epilogue(acc, cols, qg_ref[...], QK_SCALE_LOG2E)
        elif j < _J_GATE:
            qk_epilogue(acc, cols, kg_ref[...], 1.0)
        elif j < _J_CIN:
            o_ref[:, cols] = _sigmoid(acc).astype(BF16)
        else:
            o_ref[:, cols] = acc.astype(BF16)

    for r in range(B_WIDTH // PROJ_TN):
        rs = slice(r * PROJ_TN, (r + 1) * PROJ_TN)
        vt_ref[rs, :] = lax.dot_general(wvt_ref[rs, :], h_ref[...], _NT,
                                        preferred_element_type=F32).astype(BF16)


def _resident(shape):
    return pl.BlockSpec(shape, lambda *_: (0,) * len(shape), pipeline_mode=pl.Buffered(1))


def _proj(x2d, shift, scale, norm_g, w, wvt, qg, kg, lng, cos, sin, gmat, *, tm, seq_len, use_rope):
    rows, d = x2d.shape
    tps = seq_len // tm
    nb = shift.shape[0]
    pos_tiles = cos.shape[0] // tm

    def bidx(i):
        return (i // tps) % nb

    return pl.pallas_call(
        functools.partial(_proj_kernel, tm=tm, use_rope=use_rope),
        grid=(rows // tm,),
        in_specs=[
            pl.BlockSpec((tm, d), lambda i: (i, 0)),
            pl.BlockSpec((None, 1, d), lambda i: (bidx(i), 0, 0)),
            pl.BlockSpec((None, 1, d), lambda i: (bidx(i), 0, 0)),
            _resident((1, d)),
            _resident((d, Z_COLS)),
            _resident((B_WIDTH, d)),
            _resident((1, PROJ_TN)),
            _resident((1, PROJ_TN)),
            _resident((1, PROJ_TN)),
            pl.BlockSpec((tm, LANES), lambda i: (i % pos_tiles, 0)),
            pl.BlockSpec((tm, LANES), lambda i: (i % pos_tiles, 0)),
            _resident((PROJ_TN, PROJ_TN)),
        ],
        out_specs=[
            pl.BlockSpec((tm, Z_COLS), lambda i: (i, 0)),
            pl.BlockSpec((B_WIDTH, tm), lambda i: (0, i)),
        ],
        out_shape=[
            jax.ShapeDtypeStruct((rows, Z_COLS), BF16),
            jax.ShapeDtypeStruct((B_WIDTH, rows), BF16),
        ],
        scratch_shapes=[pltpu.VMEM((tm, d), BF16)],
        compiler_params=pltpu.CompilerParams(
            dimension_semantics=("arbitrary",), vmem_limit_bytes=VMEM_LIMIT),
        name="proj",
    )(x2d, shift, scale, norm_g, w, wvt, qg, kg, lng, cos, sin, gmat)


def _attn_kernel(*refs, tq, tk, src_rows, lam_init):
    n_src = len(src_rows)
    q_ref = refs[0]
    kv_refs = [(refs[1 + 2 * s], refs[2 + 2 * s]) for s in range(n_src)]
    lam_ref, subg_ref, o_ref, kmax_ref, p1_ref, p2_ref = refs[1 + 2 * n_src:]
    qi = pl.program_id(2)

    lane_k = lax.broadcasted_iota(jnp.int32, (1, LANES), 1)
    map1 = lane_k < B_QK_DIM

    @pl.when(qi == 0)
    def _():
        best1 = jnp.zeros((1, 1), F32)
        best2 = jnp.zeros((1, 1), F32)
        for k_ref, _ in kv_refs:
            kf = k_ref[...].astype(F32)
            sq = kf * kf
            n1 = jnp.sum(jnp.where(map1, sq, 0.0), axis=1, keepdims=True)
            n2 = jnp.sum(jnp.where(map1, 0.0, sq), axis=1, keepdims=True)
            best1 = jnp.maximum(best1, jnp.max(n1, axis=0, keepdims=True))
            best2 = jnp.maximum(best2, jnp.max(n2, axis=0, keepdims=True))
        kmax_ref[0:1, :] = jnp.broadcast_to(best1, (1, LANES))
        kmax_ref[1:2, :] = jnp.broadcast_to(best2, (1, LANES))

    q = q_ref[...]
    zero = jnp.zeros_like(q)
    q1 = jnp.where(map1, q, zero)
    q2 = jnp.where(map1, zero, q)

    sel_row = lax.broadcasted_iota(jnp.int32, (SUBLANES, LANES), 0)
    sel_lane = lax.broadcasted_iota(jnp.int32, (SUBLANES, LANES), 1)
    sel = jnp.where(((sel_row == 0) & (sel_lane < B_QK_DIM)) | ((sel_row == 1) & (sel_lane >= B_QK_DIM)),
                    1.0, 0.0).astype(BF16)
    qf = q.astype(F32)
    qn = lax.dot_general(sel, (qf * qf).astype(BF16), _NT, preferred_element_type=F32)
    bound1 = jnp.sqrt(qn[0:1] * kmax_ref[0:1, 0:1])
    bound2 = jnp.sqrt(qn[1:2] * kmax_ref[1:2, 0:1])
    worst = jnp.max(jnp.maximum(bound1, bound2))

    def scores_t(qm, k_ref, start, size):
        kb = k_ref[start:start + size, :]
        return lax.dot_general(kb, qm, _NT, preferred_element_type=F32)

    src_offsets = [sum(src_rows[:s]) for s in range(n_src)]

    def blocks(fn, carry):
        for (k_ref, _), rows, offset in zip(kv_refs, src_rows, src_offsets):
            size = min(tk, rows)
            for b in range(rows // size):
                carry = fn(k_ref, b * size, size, offset + b * size, carry)
        return carry

    def fold(p, op):
        acc = p[0:SUBLANES, :]
        for r in range(1, p.shape[0] // SUBLANES):
            acc = op(acc, p[r * SUBLANES:(r + 1) * SUBLANES, :])
        return acc

    def finish(shift1, shift2):
        def step(k_ref, start, size, base, carry):
            l1, l2 = carry
            p1 = jnp.exp2(scores_t(q1, k_ref, start, size) - shift1)
            p2 = jnp.exp2(scores_t(q2, k_ref, start, size) - shift2)
            p1_ref[base:base + size, :] = p1.astype(BF16)
            p2_ref[base:base + size, :] = p2.astype(BF16)
            return l1 + fold(p1, jnp.add), l2 + fold(p2, jnp.add)

        zl = jnp.zeros((SUBLANES, tq), F32)
        l1, l2 = blocks(step, (zl, zl))

        o1 = jnp.zeros((B_V_DIM, tq), F32)
        o2 = jnp.zeros((B_V_DIM, tq), F32)
        for (_, vt_ref), rows, base in zip(kv_refs, src_rows, src_offsets):
            vt = vt_ref[...]
            o1 = o1 + jnp.dot(vt, p1_ref[base:base + rows, :], preferred_element_type=F32)
            o2 = o2 + jnp.dot(vt, p2_ref[base:base + rows, :], preferred_element_type=F32)

        r1 = 1.0 / jnp.sum(l1, axis=0, keepdims=True)
        r2 = 1.0 / jnp.sum(l2, axis=0, keepdims=True)
        lv = lam_ref[...]
        lam = (jnp.exp(jnp.sum(lv[0:1] * lv[1:2], axis=1, keepdims=True))
               - jnp.exp(jnp.sum(lv[2:3] * lv[3:4], axis=1, keepdims=True)) + lam_init)
        a = o1 * r1 - o2 * (lam * r2)
        ms = jnp.mean(a * a, axis=0, keepdims=True)
        y = a * lax.rsqrt(ms + EPS) * subg_ref[...] * (1.0 - lam_init)
        o_ref[...] = y.T.astype(BF16)

    @pl.when(worst < SAFE_EXP2_RANGE)
    def _():
        finish(bound1, bound2)

    @pl.when(jnp.logical_not(worst < SAFE_EXP2_RANGE))
    def _():
        def mx_step(k_ref, start, size, base, carry):
            m1, m2 = carry
            m1 = jnp.maximum(m1, fold(scores_t(q1, k_ref, start, size), jnp.maximum))
            m2 = jnp.maximum(m2, fold(scores_t(q2, k_ref, start, size), jnp.maximum))
            return m1, m2

        neg = jnp.full((SUBLANES, tq), -jnp.inf, F32)
        m1, m2 = blocks(mx_step, (neg, neg))
        finish(jnp.max(m1, axis=0, keepdims=True), jnp.max(m2, axis=0, keepdims=True))


def _attention(zq, kv_sources, lam_vecs, subg, *, batch, seq_len, tq, tk, lam_init):
    rows = zq.shape[0]
    nq = seq_len // tq
    q_col = _J_Q * PROJ_TN // LANES
    k_col = _J_K * PROJ_TN // LANES

    in_specs = [pl.BlockSpec((tq, LANES), lambda b, h, i: (b * nq + i, q_col + h))]
    args = [zq]
    src_rows = []
    for z, vt, n in kv_sources:
        in_specs.append(pl.BlockSpec((n, LANES), lambda b, h, i: (b, k_col + h)))
        in_specs.append(pl.BlockSpec((B_V_DIM, n), lambda b, h, i: (h, b)))
        args += [z, vt]
        src_rows.append(n)
    in_specs.append(pl.BlockSpec((4, B_QK_DIM), lambda b, h, i: (0, 0)))
    in_specs.append(pl.BlockSpec((B_V_DIM, 1), lambda b, h, i: (0, 0)))
    args += [lam_vecs, subg]

    return pl.pallas_call(
        functools.partial(_attn_kernel, tq=tq, tk=tk, src_rows=tuple(src_rows), lam_init=lam_init),
        grid=(batch, B_HEADS, nq),
        in_specs=in_specs,
        out_specs=pl.BlockSpec((tq, LANES), lambda b, h, i: (b * nq + i, h)),
        out_shape=jax.ShapeDtypeStruct((rows, B_WIDTH), BF16),
        scratch_shapes=[
            pltpu.VMEM((SUBLANES, LANES), F32),
            pltpu.VMEM((sum(src_rows), tq), BF16),
            pltpu.VMEM((sum(src_rows), tq), BF16),
        ],
        compiler_params=pltpu.CompilerParams(
            dimension_semantics=("arbitrary", "arbitrary", "arbitrary"), vmem_limit_bytes=VMEM_LIMIT),
        name="attn",
    )(*args)


def _merge_kernel(u_ref, v_ref, yb_ref, ga_ref, gb_ref, gc_ref, cb_ref, cg_ref, cx_ref,
                  pg_ref, px_ref, ng_ref, nx_ref, x_ref, g1_ref,
                  sw_ref, sb_ref, cw_ref, wa_ref, wb_ref, wc_ref, wo_ref,
                  o_ref, ya_ref, *, tm, tps):
    i = pl.program_id(0)
    first = (i % tps) == 0
    last = (i % tps) == (tps - 1)

    for c in range(tm // CHUNK):
        rs = slice(c * CHUNK, (c + 1) * CHUNK)
        for g in range(A_GROUPS):
            cs = slice(g * LANES, (g + 1) * LANES)
            mixed = jnp.dot(sw_ref[g], v_ref[rs, cs], preferred_element_type=F32) + sb_ref[:, cs]
            ya_ref[rs, cs] = (u_ref[rs, cs].astype(F32) * mixed).astype(BF16)

    t = cg_ref[...].astype(F32) * cx_ref[...].astype(F32)
    t_prev_row = pg_ref[HALO - 1:HALO, :].astype(F32) * px_ref[HALO - 1:HALO, :].astype(F32)
    t_next_row = ng_ref[0:1, :].astype(F32) * nx_ref[0:1, :].astype(F32)
    t_prev_row = jnp.where(first, 0.0, t_prev_row)
    t_next_row = jnp.where(last, 0.0, t_next_row)
    row = lax.broadcasted_iota(jnp.int32, (tm, C_WIDTH), 0)
    t_prev = jnp.where(row == 0, t_prev_row, pltpu.roll(t, 1, 0))
    t_next = jnp.where(row == tm - 1, t_next_row, pltpu.roll(t, tm - 1, 0))
    cw = cw_ref[...]
    y_c = cb_ref[...].astype(F32) * (t_prev * cw[0:1] + t * cw[1:2] + t_next * cw[2:3])

    m = ga_ref[...].astype(F32) * jnp.dot(ya_ref[...], wa_ref[...], preferred_element_type=F32)
    m = m + gb_ref[...].astype(F32) * jnp.dot(yb_ref[...], wb_ref[...], preferred_element_type=F32)
    m = m + gc_ref[...].astype(F32) * jnp.dot(y_c.astype(BF16), wc_ref[...], preferred_element_type=F32)
    out = jnp.dot(m.astype(BF16), wo_ref[...], preferred_element_type=F32)
    o_ref[...] = x_ref[...] + g1_ref[...] * out


def _merge(z, yb, x2d, g1, sw, sb, cw, wa, wb, wc, wo, *, tm, seq_len):
    rows, d = x2d.shape
    tps = seq_len // tm
    nb = g1.shape[0]
    hb = tm // HALO
    n_halo = rows // HALO
    gate_col = _J_GATE * PROJ_TN // D_MODEL

    def bidx(i):
        return (i // tps) % nb

    def prev_blk(i):
        return jnp.maximum(i * hb - 1, 0)

    def next_blk(i):
        return jnp.minimum((i + 1) * hb, n_halo - 1)

    const = lambda i: (0, 0)
    return pl.pallas_call(
        functools.partial(_merge_kernel, tm=tm, tps=tps),
        grid=(rows // tm,),
        in_specs=[
            pl.BlockSpec((tm, A_WIDTH), lambda i: (i, _J_U)),
            pl.BlockSpec((tm, A_WIDTH), lambda i: (i, _J_V)),
            pl.BlockSpec((tm, B_WIDTH), lambda i: (i, 0)),
            pl.BlockSpec((tm, d), lambda i: (i, gate_col)),
            pl.BlockSpec((tm, d), lambda i: (i, gate_col + 1)),
            pl.BlockSpec((tm, d), lambda i: (i, gate_col + 2)),
            pl.BlockSpec((tm, C_WIDTH), lambda i: (i, _J_CIN)),
            pl.BlockSpec((tm, C_WIDTH), lambda i: (i, _J_CIN + 1)),
            pl.BlockSpec((tm, C_WIDTH), lambda i: (i, _J_CIN + 2)),
            pl.BlockSpec((HALO, C_WIDTH), lambda i: (prev_blk(i), _J_CIN + 1)),
            pl.BlockSpec((HALO, C_WIDTH), lambda i: (prev_blk(i), _J_CIN + 2)),
            pl.BlockSpec((HALO, C_WIDTH), lambda i: (next_blk(i), _J_CIN + 1)),
            pl.BlockSpec((HALO, C_WIDTH), lambda i: (next_blk(i), _J_CIN + 2)),
            pl.BlockSpec((tm, d), lambda i: (i, 0)),
            pl.BlockSpec((None, 1, d), lambda i: (bidx(i), 0, 0)),
            pl.BlockSpec((A_GROUPS, CHUNK, CHUNK), lambda i: (0, 0, 0)),
            pl.BlockSpec((CHUNK, A_WIDTH), const),
            pl.BlockSpec((3, C_WIDTH), const),
            pl.BlockSpec((A_WIDTH, d), const),
            pl.BlockSpec((B_WIDTH, d), const),
            pl.BlockSpec((C_WIDTH, d), const),
            pl.BlockSpec((d, d), const),
        ],
        out_specs=pl.BlockSpec((tm, d), lambda i: (i, 0)),
        out_shape=jax.ShapeDtypeStruct((rows, d), F32),
        scratch_shapes=[pltpu.VMEM((tm, A_WIDTH), BF16)],
        compiler_params=pltpu.CompilerParams(
            dimension_semantics=("arbitrary",), vmem_limit_bytes=VMEM_LIMIT),
        name="merge",
    )(z, z, yb, z, z, z, z, z, z, z, z, z, z, x2d, g1, sw, sb, cw, wa, wb, wc, wo)


def _ffn_kernel(x_ref, xp_ref, xn_ref, sh_ref, sc_ref, gt_ref, ng_ref,
                wu_ref, cw_ref, cb_ref, wd_ref, o_ref, h_ref, u_ref, a_ref, *, tm, tps):
    i = pl.program_id(0)

    def nm(x):
        return _norm_modulate(x, ng_ref[...], sh_ref[...], sc_ref[...])

    first = (i % tps) == 0
    last = (i % tps) == (tps - 1)
    h_ref[HALO:HALO + tm, :] = nm(x_ref[...]).astype(BF16)
    h_ref[0:HALO, :] = jnp.where(first, 0.0, nm(xp_ref[...])).astype(BF16)
    h_ref[HALO + tm:, :] = jnp.where(last, 0.0, nm(xn_ref[...])).astype(BF16)

    def conv(slot, cols):
        cw = cw_ref[:, cols]
        return (u_ref[slot, HALO - 1:HALO - 1 + tm, :] * cw[0:1] + u_ref[slot, HALO:HALO + tm, :] * cw[1:2]
                + u_ref[slot, HALO + 1:HALO + 1 + tm, :] * cw[2:3] + cb_ref[:, cols])

    for c in range(D_FF // FF_TN):
        g_cols = slice(c * FF_TN, (c + 1) * FF_TN)
        v_cols = slice(D_FF + c * FF_TN, D_FF + (c + 1) * FF_TN)
        g_slot = 2 * (c % 2)
        v_slot = g_slot + 1
        h = h_ref[...]
        u_ref[g_slot] = jnp.dot(h, wu_ref[:, g_cols], preferred_element_type=F32)
        u_ref[v_slot] = jnp.dot(h, wu_ref[:, v_cols], preferred_element_type=F32)
        g = conv(g_slot, g_cols)
        val = conv(v_slot, v_cols)
        a_ref[:, g_cols] = (g * _sigmoid(g) * val).astype(BF16)

    out = jnp.dot(a_ref[...], wd_ref[...], preferred_element_type=F32)
    o_ref[...] = x_ref[...] + gt_ref[...] * out


def _ffn(x2d, shift, scale, gate, norm_g, w_up, cw, cb, w_down, *, tm, seq_len):
    rows, d = x2d.shape
    tps = seq_len // tm
    nb = shift.shape[0]
    hb = tm // HALO
    n_halo = rows // HALO

    def bidx(i):
        return (i // tps) % nb

    vec = pl.BlockSpec((None, 1, d), lambda i: (bidx(i), 0, 0))
    return pl.pallas_call(
        functools.partial(_ffn_kernel, tm=tm, tps=tps),
        grid=(rows // tm,),
        in_specs=[
            pl.BlockSpec((tm, d), lambda i: (i, 0)),
            pl.BlockSpec((HALO, d), lambda i: (jnp.maximum(i * hb - 1, 0), 0)),
            pl.BlockSpec((HALO, d), lambda i: (jnp.minimum((i + 1) * hb, n_halo - 1), 0)),
            vec, vec, vec,
            _resident((1, d)),
            _resident((d, 2 * D_FF)),
            _resident((3, 2 * D_FF)),
            _resident((1, 2 * D_FF)),
            _resident((D_FF, d)),
        ],
        out_specs=pl.BlockSpec((tm, d), lambda i: (i, 0)),
        out_shape=jax.ShapeDtypeStruct((rows, d), F32),
        scratch_shapes=[
            pltpu.VMEM((tm + 2 * HALO, d), BF16),
            pltpu.VMEM((4, tm + 2 * HALO, FF_TN), F32),
            pltpu.VMEM((tm, D_FF), BF16),
        ],
        compiler_params=pltpu.CompilerParams(
            dimension_semantics=("arbitrary",), vmem_limit_bytes=VMEM_LIMIT),
        name="ffn",
    )(x2d, x2d, x2d, shift, scale, gate, norm_g, w_up, cw, cb, w_down)


def kernel(x, c, ctx, c_ctx, ada_w, ada_b, norm1_g, norm2_g, w_in, sgu_ln_g, sgu_w, sgu_b, q_norm_g, k_norm_g, lam_q1, lam_k1, lam_q2, lam_k2, subln_g, conv_w, w_br_a, w_br_b, w_br_c, w_out, ffn_up, ffn_conv_w, ffn_conv_b, ffn_down):
    batch, seq, d = x.shape
    ctx_len = ctx.shape[1]
    depth = ada_w.shape[0]
    assert d == D_MODEL and seq % GRID_W == 0

    tm_lat = min(512, seq)
    tm_mrg = min(512, seq)
    tm_ctx = ctx_len
    tq_lat = min(256, seq)
    tk = 512
    assert seq % tm_lat == 0 and seq % tq_lat == 0 and ctx_len % HALO == 0

    perm = _proj_column_order()
    w_in_b = jnp.take(w_in, perm, axis=2).astype(BF16)
    wvt_b = jnp.swapaxes(w_in[:, :, _REF_B_V:_REF_B_V + B_WIDTH], 1, 2).astype(BF16)
    wa_b, wb_b, wc_b, wo_b = (w.astype(BF16) for w in (w_br_a, w_br_b, w_br_c, w_out))
    up_b = ffn_up.astype(BF16)
    down_b = ffn_down.astype(BF16)
    sw_b = sgu_w.astype(BF16)
    sgu_bias = jnp.repeat(jnp.swapaxes(sgu_b, 1, 2), LANES, axis=2)
    gmat = jnp.asarray(np.kron(np.eye(PROJ_TN // B_QK_DIM), np.full((B_QK_DIM, B_QK_DIM), 1.0 / B_QK_DIM)), BF16)
    cos, sin = _rope_tables(seq)
    cos_ctx = jnp.ones((tm_ctx, LANES), F32)
    sin_ctx = jnp.zeros((tm_ctx, LANES), F32)
    lam_vecs = jnp.stack([lam_q1, lam_k1, lam_q2, lam_k2], axis=1).astype(F32)

    pad = (-(batch + 1)) % 8
    cc = jnp.concatenate([c, c_ctx[None, :], jnp.zeros((pad, d), F32)], axis=0)
    mod = _modulation(cc, ada_w, ada_b)
    mod = mod.reshape(depth, cc.shape[0], 6, 1, d)

    xs = x.reshape(batch * seq, d)
    cs = ctx.reshape(batch * ctx_len, d)

    for l in range(depth):
        last = l == depth - 1
        lam_init = 0.8 - 0.6 * math.exp(-0.3 * l)
        lat = [mod[l, :batch, k] for k in range(6)]
        cm = [mod[l, batch:batch + 1, k] for k in range(6)]
        n1 = norm1_g[l].reshape(1, d)
        n2 = norm2_g[l].reshape(1, d)
        qg = jnp.tile(q_norm_g[l], PROJ_TN // B_QK_DIM).reshape(1, PROJ_TN)
        kg = jnp.tile(k_norm_g[l], PROJ_TN // B_QK_DIM).reshape(1, PROJ_TN)
        lng = sgu_ln_g[l].reshape(1, A_WIDTH)
        subg = subln_g[l].reshape(B_V_DIM, 1)

        z_lat, vt_lat = _proj(xs, lat[0], lat[1], n1, w_in_b[l], wvt_b[l], qg, kg, lng, cos, sin, gmat,
                              tm=tm_lat, seq_len=seq, use_rope=True)
        z_ctx, vt_ctx = _proj(cs, cm[0], cm[1], n1, w_in_b[l], wvt_b[l], qg, kg, lng, cos_ctx, sin_ctx, gmat,
                              tm=tm_ctx, seq_len=ctx_len, use_rope=False)

        yb_lat = _attention(z_lat, [(z_lat, vt_lat, seq), (z_ctx, vt_ctx, ctx_len)], lam_vecs[l], subg,
                            batch=batch, seq_len=seq, tq=tq_lat, tk=tk, lam_init=lam_init)
        merge_w = (sw_b[l], sgu_bias[l], conv_w[l], wa_b[l], wb_b[l], wc_b[l], wo_b[l])
        ffn_w = (up_b[l], ffn_conv_w[l], ffn_conv_b[l].reshape(1, 2 * D_FF), down_b[l])
        xs = _merge(z_lat, yb_lat, xs, lat[2], *merge_w, tm=tm_mrg, seq_len=seq)
        xs = _ffn(xs, lat[3], lat[4], lat[5], n2, *ffn_w, tm=tm_lat, seq_len=seq)

        if not last:
            yb_ctx = _attention(z_ctx, [(z_ctx, vt_ctx, ctx_len)], lam_vecs[l], subg,
                                batch=batch, seq_len=ctx_len, tq=tm_ctx, tk=tk, lam_init=lam_init)
            cs = _merge(z_ctx, yb_ctx, cs, cm[2], *merge_w, tm=tm_ctx, seq_len=ctx_len)
            cs = _ffn(cs, cm[3], cm[4], cm[5], n2, *ffn_w, tm=tm_ctx, seq_len=ctx_len)

    return xs.reshape(batch, seq, d)
```

```python
import functools
import math

import jax
import jax.numpy as jnp
import numpy as np
from jax import lax
from jax.experimental import pallas as pl
from jax.experimental.pallas import tpu as pltpu

F32 = jnp.float32
BF16 = jnp.bfloat16

D_MODEL = 1024
GRID_W = 64
A_WIDTH = 512
A_GROUPS = 4
CHUNK = 128
B_HEADS = 8
B_QK_DIM = 64
B_V_DIM = 128
B_WIDTH = B_HEADS * B_V_DIM
ROPE_THETA = 10000.0
C_WIDTH = 512
N_BRANCH = 3
D_FF = 2816
EPS = 1e-6

_REF_A_U = 0
_REF_A_V = 512
_REF_B_Q = 1024
_REF_B_K = 2048
_REF_B_V = 3072
_REF_C_IN = 4096
_REF_GATE = 5632

PROJ_TN = 512
_J_U = 0
_J_V = 1
_J_Q = 2
_J_K = 4
_J_GATE = 6
_J_CIN = 12
_J_VALT = 15
_N_Z_TILES = 15
_N_PROJ_STEPS = 17
Z_COLS = _N_Z_TILES * PROJ_TN

HALO = 16
LANES = 128
SUBLANES = 8
FF_TN = 256
QK_SCALE_LOG2E = (B_QK_DIM ** -0.5) * math.log2(math.e)
SAFE_EXP2_RANGE = 60.0
VMEM_LIMIT = 56 * 1024 * 1024
_NT = (((1,), (1,)), ((), ()))


def _proj_column_order():
    idx = []
    idx += list(range(_REF_A_U, _REF_A_U + 512))
    idx += list(range(_REF_A_V, _REF_A_V + 512))
    for base in (_REF_B_Q, _REF_B_K):
        for h in range(B_HEADS):
            for a in range(2):
                start = base + a * B_HEADS * B_QK_DIM + h * B_QK_DIM
                idx += list(range(start, start + B_QK_DIM))
    idx += list(range(_REF_GATE, _REF_GATE + N_BRANCH * D_MODEL))
    idx += list(range(_REF_C_IN, _REF_C_IN + 3 * C_WIDTH))
    assert len(idx) == Z_COLS
    return np.asarray(idx, dtype=np.int32)


def _rope_tables(n_tokens):
    rows = n_tokens // GRID_W
    row = jnp.repeat(jnp.arange(rows, dtype=F32), GRID_W)
    col = jnp.tile(jnp.arange(GRID_W, dtype=F32), rows)
    n_freq = B_QK_DIM // 4
    inv = ROPE_THETA ** (-jnp.arange(n_freq, dtype=F32) / n_freq)
    ang_r = row[:, None] * inv
    ang_c = col[:, None] * inv
    cos64 = jnp.concatenate([jnp.cos(ang_r), jnp.cos(ang_r), jnp.cos(ang_c), jnp.cos(ang_c)], axis=1)
    sin64 = jnp.concatenate([-jnp.sin(ang_r), jnp.sin(ang_r), -jnp.sin(ang_c), jnp.sin(ang_c)], axis=1)
    return jnp.tile(cos64, (1, 2)), jnp.tile(sin64, (1, 2))


def _mod_kernel(cc_ref, w_ref, b_ref, o_ref):
    cc = cc_ref[...]
    s = cc * jax.nn.sigmoid(cc)
    o_ref[...] = jnp.dot(s, w_ref[...], preferred_element_type=F32,
                         precision=lax.Precision.HIGHEST) + b_ref[...]


def _modulation(cc, ada_w, ada_b):
    depth, d, n = ada_w.shape
    tn = 1536
    rows = cc.shape[0]
    return pl.pallas_call(
        _mod_kernel,
        grid=(depth, n // tn),
        in_specs=[
            pl.BlockSpec((rows, d), lambda l, j: (0, 0)),
            pl.BlockSpec((None, d, tn), lambda l, j: (l, 0, j)),
            pl.BlockSpec((None, 1, tn), lambda l, j: (l, 0, j)),
        ],
        out_specs=pl.BlockSpec((None, rows, tn), lambda l, j: (l, 0, j)),
        out_shape=jax.ShapeDtypeStruct((depth, rows, n), F32),
        compiler_params=pltpu.CompilerParams(
            dimension_semantics=("arbitrary", "arbitrary"), vmem_limit_bytes=VMEM_LIMIT),
        name="modulation",
    )(cc, ada_w, ada_b.reshape(depth, 1, n))


def _norm_modulate(x, gain, shift, scale):
    ms = jnp.mean(x * x, axis=-1, keepdims=True)
    y = x * lax.rsqrt(ms + EPS) * gain
    return y * (1.0 + scale) + shift


def _sigmoid(x):
    return 0.5 * jnp.tanh(0.5 * x) + 0.5


def _proj_kernel(x_ref, sh_ref, sc_ref, ng_ref, w_ref, wvt_ref, qg_ref, kg_ref, lng_ref,
                 cos_ref, sin_ref, gmat_ref, o_ref, vt_ref, h_ref, *, tm, use_rope):
    h_ref[...] = _norm_modulate(x_ref[...], ng_ref[...], sh_ref[...], sc_ref[...]).astype(BF16)

    def qk_epilogue(acc, cols, gain, scale):
        ms = jnp.dot((acc * acc).astype(BF16), gmat_ref[...], preferred_element_type=F32)
        y = acc * lax.rsqrt(ms + EPS) * gain
        if not use_rope:
            o_ref[:, cols] = (y * scale).astype(BF16)
            return
        lane = lax.broadcasted_iota(jnp.int32, (tm, LANES), 1)
        first_half = (lane & 16) == 0
        cos = cos_ref[...]
        sin = sin_ref[...]
        for c in range(PROJ_TN // LANES):
            yc = y[:, c * LANES:(c + 1) * LANES]
            partner = jnp.where(first_half, pltpu.roll(yc, LANES - 16, 1), pltpu.roll(yc, 16, 1))
            lo = cols.start + c * LANES
            o_ref[:, lo:lo + LANES] = ((yc * cos + partner * sin) * scale).astype(BF16)

    for j in range(_N_Z_TILES):
        cols = slice(j * PROJ_TN, (j + 1) * PROJ_TN)
        acc = jnp.dot(h_ref[...], w_ref[:, cols], preferred_element_type=F32)
        if j == _J_U:
            o_ref[:, cols] = jax.nn.gelu(acc).astype(BF16)
        elif j == _J_V:
            g = jax.nn.gelu(acc)
            mu = jnp.mean(g, axis=-1, keepdims=True)
            c = g - mu
            var = jnp.mean(c * c, axis=-1, keepdims=True)
            o_ref[:, cols] = (c * lax.rsqrt(var + EPS) * lng_ref[...]).astype(BF16)
        elif j < _J_K:
            qk_epilogue(acc, cols, qg_ref[...], QK_SCALE_LOG2E)
        elif j < _J_GATE:
            qk_epilogue(acc, cols, kg_ref[...], 1.0)
        elif j < _J_CIN:
            o_ref[:, cols] = _sigmoid(acc).astype(BF16)
        else:
            o_ref[:, cols] = acc.astype(BF16)

    for r in range(B_WIDTH // PROJ_TN):
        rs = slice(r * PROJ_TN, (r + 1) * PROJ_TN)
        vt_ref[rs, :] = lax.dot_general(wvt_ref[rs, :], h_ref[...], _NT,
                                        preferred_element_type=F32).astype(BF16)


def _resident(shape):
    return pl.BlockSpec(shape, lambda *_: (0,) * len(shape), pipeline_mode=pl.Buffered(1))


def _proj(x2d, shift, scale, norm_g, w, wvt, qg, kg, lng, cos, sin, gmat, *, tm, seq_len, use_rope):
    rows, d = x2d.shape
    tps = seq_len // tm
    nb = shift.shape[0]
    pos_tiles = cos.shape[0] // tm

    def bidx(i):
        return (i // tps) % nb

    return pl.pallas_call(
        functools.partial(_proj_kernel, tm=tm, use_rope=use_rope),
        grid=(rows // tm,),
        in_specs=[
            pl.BlockSpec((tm, d), lambda i: (i, 0)),
            pl.BlockSpec((None, 1, d), lambda i: (bidx(i), 0, 0)),
            pl.BlockSpec((None, 1, d), lambda i: (bidx(i), 0, 0)),
            _resident((1, d)),
            _resident((d, Z_COLS)),
            _resident((B_WIDTH, d)),
            _resident((1, PROJ_TN)),
            _resident((1, PROJ_TN)),
            _resident((1, PROJ_TN)),
            pl.BlockSpec((tm, LANES), lambda i: (i % pos_tiles, 0)),
            pl.BlockSpec((tm, LANES), lambda i: (i % pos_tiles, 0)),
            _resident((PROJ_TN, PROJ_TN)),
        ],
        out_specs=[
            pl.BlockSpec((tm, Z_COLS), lambda i: (i, 0)),
            pl.BlockSpec((B_WIDTH, tm), lambda i: (0, i)),
        ],
        out_shape=[
            jax.ShapeDtypeStruct((rows, Z_COLS), BF16),
            jax.ShapeDtypeStruct((B_WIDTH, rows), BF16),
        ],
        scratch_shapes=[pltpu.VMEM((tm, d), BF16)],
        compiler_params=pltpu.CompilerParams(
            dimension_semantics=("arbitrary",), vmem_limit_bytes=VMEM_LIMIT),
        name="proj",
    )(x2d, shift, scale, norm_g, w, wvt, qg, kg, lng, cos, sin, gmat)


def _attn_kernel(*refs, tq, tk, src_rows, lam_init):
    n_src = len(src_rows)
    q_ref = refs[0]
    kv_refs = [(refs[1 + 2 * s], refs[2 + 2 * s]) for s in range(n_src)]
    lam_ref, subg_ref, o_ref, kmax_ref, p1_ref, p2_ref, safe_ref = refs[1 + 2 * n_src:]
    qi = pl.program_id(2)

    lane_k = lax.broadcasted_iota(jnp.int32, (1, LANES), 1)
    map1 = lane_k < B_QK_DIM

    @pl.when(qi == 0)
    def _():
        best1 = jnp.zeros((1, 1), F32)
        best2 = jnp.zeros((1, 1), F32)
        for k_ref, _ in kv_refs:
            kf = k_ref[...].astype(F32)
            sq = kf * kf
            n1 = jnp.sum(jnp.where(map1, sq, 0.0), axis=1, keepdims=True)
            n2 = jnp.sum(jnp.where(map1, 0.0, sq), axis=1, keepdims=True)
            best1 = jnp.maximum(best1, jnp.max(n1, axis=0, keepdims=True))
            best2 = jnp.maximum(best2, jnp.max(n2, axis=0, keepdims=True))
        kmax_ref[0:1, :] = jnp.broadcast_to(best1, (1, LANES))
        kmax_ref[1:2, :] = jnp.broadcast_to(best2, (1, LANES))
        q_norm_cap = lam_ref[4:5, 0:1]
        worst = jnp.max(q_norm_cap * jnp.sqrt(jnp.maximum(best1, best2)))
        safe_ref[0] = (worst < SAFE_EXP2_RANGE).astype(jnp.int32)

    q = q_ref[...]
    zero = jnp.zeros_like(q)
    q1 = jnp.where(map1, q, zero)
    q2 = jnp.where(map1, zero, q)

    sel_row = lax.broadcasted_iota(jnp.int32, (SUBLANES, LANES), 0)
    sel_lane = lax.broadcasted_iota(jnp.int32, (SUBLANES, LANES), 1)
    sel = jnp.where(((sel_row == 0) & (sel_lane < B_QK_DIM)) | ((sel_row == 1) & (sel_lane >= B_QK_DIM)),
                    1.0, 0.0).astype(BF16)
    qf = q.astype(F32)
    qn = lax.dot_general(sel, (qf * qf).astype(BF16), _NT, preferred_element_type=F32)
    bound1 = jnp.sqrt(qn[0:1] * kmax_ref[0:1, 0:1])
    bound2 = jnp.sqrt(qn[1:2] * kmax_ref[1:2, 0:1])
    safe = safe_ref[0] == 1

    def scores_t(qm, k_ref, start, size):
        kb = k_ref[start:start + size, :]
        return lax.dot_general(kb, qm, _NT, preferred_element_type=F32)

    src_offsets = [sum(src_rows[:s]) for s in range(n_src)]

    def blocks(fn, carry):
        for (k_ref, _), rows, offset in zip(kv_refs, src_rows, src_offsets):
            size = min(tk, rows)
            for b in range(rows // size):
                carry = fn(k_ref, b * size, size, offset + b * size, carry)
        return carry

    def fold(p, op):
        acc = p[0:SUBLANES, :]
        for r in range(1, p.shape[0] // SUBLANES):
            acc = op(acc, p[r * SUBLANES:(r + 1) * SUBLANES, :])
        return acc

    def finish(shift1, shift2):
        def step(k_ref, start, size, base, carry):
            l1, l2 = carry
            p1 = jnp.exp2(scores_t(q1, k_ref, start, size) - shift1)
            p2 = jnp.exp2(scores_t(q2, k_ref, start, size) - shift2)
            p1_ref[base:base + size, :] = p1.astype(BF16)
            p2_ref[base:base + size, :] = p2.astype(BF16)
            return l1 + fold(p1, jnp.add), l2 + fold(p2, jnp.add)

        zl = jnp.zeros((SUBLANES, tq), F32)
        l1, l2 = blocks(step, (zl, zl))

        o1 = jnp.zeros((B_V_DIM, tq), F32)
        o2 = jnp.zeros((B_V_DIM, tq), F32)
        for (_, vt_ref), rows, base in zip(kv_refs, src_rows, src_offsets):
            vt = vt_ref[...]
            o1 = o1 + jnp.dot(vt, p1_ref[base:base + rows, :], preferred_element_type=F32)
            o2 = o2 + jnp.dot(vt, p2_ref[base:base + rows, :], preferred_element_type=F32)

        r1 = 1.0 / jnp.sum(l1, axis=0, keepdims=True)
        r2 = 1.0 / jnp.sum(l2, axis=0, keepdims=True)
        lv = lam_ref[...]
        lam = (jnp.exp(jnp.sum(lv[0:1] * lv[1:2], axis=1, keepdims=True))
               - jnp.exp(jnp.sum(lv[2:3] * lv[3:4], axis=1, keepdims=True)) + lam_init)
        a = o1 * r1 - o2 * (lam * r2)
        ms = jnp.mean(a * a, axis=0, keepdims=True)
        y = a * lax.rsqrt(ms + EPS) * subg_ref[...] * (1.0 - lam_init)
        o_ref[...] = y.T.astype(BF16)

    @pl.when(safe)
    def _():
        finish(bound1, bound2)

    @pl.when(jnp.logical_not(safe))
    def _():
        def mx_step(k_ref, start, size, base, carry):
            m1, m2 = carry
            m1 = jnp.maximum(m1, fold(scores_t(q1, k_ref, start, size), jnp.maximum))
            m2 = jnp.maximum(m2, fold(scores_t(q2, k_ref, start, size), jnp.maximum))
            return m1, m2

        neg = jnp.full((SUBLANES, tq), -jnp.inf, F32)
        m1, m2 = blocks(mx_step, (neg, neg))
        finish(jnp.max(m1, axis=0, keepdims=True), jnp.max(m2, axis=0, keepdims=True))


def _attention(zq, kv_sources, lam_vecs, subg, *, batch, seq_len, tq, tk, lam_init):
    rows = zq.shape[0]
    nq = seq_len // tq
    q_col = _J_Q * PROJ_TN // LANES
    k_col = _J_K * PROJ_TN // LANES

    in_specs = [pl.BlockSpec((tq, LANES), lambda b, h, i: (b * nq + i, q_col + h))]
    args = [zq]
    src_rows = []
    for z, vt, n in kv_sources:
        in_specs.append(pl.BlockSpec((n, LANES), lambda b, h, i: (b, k_col + h)))
        in_specs.append(pl.BlockSpec((B_V_DIM, n), lambda b, h, i: (h, b)))
        args += [z, vt]
        src_rows.append(n)
    in_specs.append(pl.BlockSpec(lam_vecs.shape, lambda b, h, i: (0, 0)))
    in_specs.append(pl.BlockSpec((B_V_DIM, 1), lambda b, h, i: (0, 0)))
    args += [lam_vecs, subg]

    return pl.pallas_call(
        functools.partial(_attn_kernel, tq=tq, tk=tk, src_rows=tuple(src_rows), lam_init=lam_init),
        grid=(batch, B_HEADS, nq),
        in_specs=in_specs,
        out_specs=pl.BlockSpec((tq, LANES), lambda b, h, i: (b * nq + i, h)),
        out_shape=jax.ShapeDtypeStruct((rows, B_WIDTH), BF16),
        scratch_shapes=[
            pltpu.VMEM((SUBLANES, LANES), F32),
            pltpu.VMEM((sum(src_rows), tq), BF16),
            pltpu.VMEM((sum(src_rows), tq), BF16),
            pltpu.SMEM((1,), jnp.int32),
        ],
        compiler_params=pltpu.CompilerParams(
            dimension_semantics=("arbitrary", "arbitrary", "arbitrary"), vmem_limit_bytes=VMEM_LIMIT),
        name="attn",
    )(*args)


def _merge_kernel(u_ref, v_ref, yb_ref, ga_ref, gb_ref, gc_ref, cb_ref, cg_ref, cx_ref,
                  pg_ref, px_ref, ng_ref, nx_ref, x_ref, g1_ref,
                  sw_ref, sb_ref, cw_ref, wa_ref, wb_ref, wc_ref, wo_ref,
                  o_ref, ya_ref, *, tm, tps):
    i = pl.program_id(0)
    first = (i % tps) == 0
    last = (i % tps) == (tps - 1)

    for c in range(tm // CHUNK):
        rs = slice(c * CHUNK, (c + 1) * CHUNK)
        for g in range(A_GROUPS):
            cs = slice(g * LANES, (g + 1) * LANES)
            mixed = jnp.dot(sw_ref[g], v_ref[rs, cs], preferred_element_type=F32) + sb_ref[:, cs]
            ya_ref[rs, cs] = (u_ref[rs, cs].astype(F32) * mixed).astype(BF16)

    t = cg_ref[...].astype(F32) * cx_ref[...].astype(F32)
    t_prev_row = pg_ref[HALO - 1:HALO, :].astype(F32) * px_ref[HALO - 1:HALO, :].astype(F32)
    t_next_row = ng_ref[0:1, :].astype(F32) * nx_ref[0:1, :].astype(F32)
    t_prev_row = jnp.where(first, 0.0, t_prev_row)
    t_next_row = jnp.where(last, 0.0, t_next_row)
    row = lax.broadcasted_iota(jnp.int32, (tm, C_WIDTH), 0)
    t_prev = jnp.where(row == 0, t_prev_row, pltpu.roll(t, 1, 0))
    t_next = jnp.where(row == tm - 1, t_next_row, pltpu.roll(t, tm - 1, 0))
    cw = cw_ref[...]
    y_c = cb_ref[...].astype(F32) * (t_prev * cw[0:1] + t * cw[1:2] + t_next * cw[2:3])

    m = ga_ref[...].astype(F32) * jnp.dot(ya_ref[...], wa_ref[...], preferred_element_type=F32)
    m = m + gb_ref[...].astype(F32) * jnp.dot(yb_ref[...], wb_ref[...], preferred_element_type=F32)
    m = m + gc_ref[...].astype(F32) * jnp.dot(y_c.astype(BF16), wc_ref[...], preferred_element_type=F32)
    out = jnp.dot(m.astype(BF16), wo_ref[...], preferred_element_type=F32)
    o_ref[...] = x_ref[...] + g1_ref[...] * out


def _merge(z, yb, x2d, g1, sw, sb, cw, wa, wb, wc, wo, *, tm, seq_len):
    rows, d = x2d.shape
    tps = seq_len // tm
    nb = g1.shape[0]
    hb = tm // HALO
    n_halo = rows // HALO
    gate_col = _J_GATE * PROJ_TN // D_MODEL

    def bidx(i):
        return (i // tps) % nb

    def prev_blk(i):
        return jnp.maximum(i * hb - 1, 0)

    def next_blk(i):
        return jnp.minimum((i + 1) * hb, n_halo - 1)

    const = lambda i: (0, 0)
    return pl.pallas_call(
        functools.partial(_merge_kernel, tm=tm, tps=tps),
        grid=(rows // tm,),
        in_specs=[
            pl.BlockSpec((tm, A_WIDTH), lambda i: (i, _J_U)),
            pl.BlockSpec((tm, A_WIDTH), lambda i: (i, _J_V)),
            pl.BlockSpec((tm, B_WIDTH), lambda i: (i, 0)),
            pl.BlockSpec((tm, d), lambda i: (i, gate_col)),
            pl.BlockSpec((tm, d), lambda i: (i, gate_col + 1)),
            pl.BlockSpec((tm, d), lambda i: (i, gate_col + 2)),
            pl.BlockSpec((tm, C_WIDTH), lambda i: (i, _J_CIN)),
            pl.BlockSpec((tm, C_WIDTH), lambda i: (i, _J_CIN + 1)),
            pl.BlockSpec((tm, C_WIDTH), lambda i: (i, _J_CIN + 2)),
            pl.BlockSpec((HALO, C_WIDTH), lambda i: (prev_blk(i), _J_CIN + 1)),
            pl.BlockSpec((HALO, C_WIDTH), lambda i: (prev_blk(i), _J_CIN + 2)),
            pl.BlockSpec((HALO, C_WIDTH), lambda i: (next_blk(i), _J_CIN + 1)),
            pl.BlockSpec((HALO, C_WIDTH), lambda i: (next_blk(i), _J_CIN + 2)),
            pl.BlockSpec((tm, d), lambda i: (i, 0)),
            pl.BlockSpec((None, 1, d), lambda i: (bidx(i), 0, 0)),
            pl.BlockSpec((A_GROUPS, CHUNK, CHUNK), lambda i: (0, 0, 0)),
            pl.BlockSpec((CHUNK, A_WIDTH), const),
            pl.BlockSpec((3, C_WIDTH), const),
            pl.BlockSpec((A_WIDTH, d), const),
            pl.BlockSpec((B_WIDTH, d), const),
            pl.BlockSpec((C_WIDTH, d), const),
            pl.BlockSpec((d, d), const),
        ],
        out_specs=pl.BlockSpec((tm, d), lambda i: (i, 0)),
        out_shape=jax.ShapeDtypeStruct((rows, d), F32),
        scratch_shapes=[pltpu.VMEM((tm, A_WIDTH), BF16)],
        compiler_params=pltpu.CompilerParams(
            dimension_semantics=("arbitrary",), vmem_limit_bytes=VMEM_LIMIT),
        name="merge",
    )(z, z, yb, z, z, z, z, z, z, z, z, z, z, x2d, g1, sw, sb, cw, wa, wb, wc, wo)


def _ffn_kernel(x_ref, xp_ref, xn_ref, sh_ref, sc_ref, gt_ref, ng_ref,
                wu_ref, cw_ref, cb_ref, wd_ref, o_ref, h_ref, u_ref, a_ref, *, tm, tps):
    i = pl.program_id(0)

    def nm(x):
        return _norm_modulate(x, ng_ref[...], sh_ref[...], sc_ref[...])

    first = (i % tps) == 0
    last = (i % tps) == (tps - 1)
    h_ref[HALO:HALO + tm, :] = nm(x_ref[...]).astype(BF16)
    h_ref[0:HALO, :] = jnp.where(first, 0.0, nm(xp_ref[...])).astype(BF16)
    h_ref[HALO + tm:, :] = jnp.where(last, 0.0, nm(xn_ref[...])).astype(BF16)

    def conv(slot, cols):
        cw = cw_ref[:, cols]
        return (u_ref[slot, HALO - 1:HALO - 1 + tm, :] * cw[0:1] + u_ref[slot, HALO:HALO + tm, :] * cw[1:2]
                + u_ref[slot, HALO + 1:HALO + 1 + tm, :] * cw[2:3] + cb_ref[:, cols])

    for c in range(D_FF // FF_TN):
        g_cols = slice(c * FF_TN, (c + 1) * FF_TN)
        v_cols = slice(D_FF + c * FF_TN, D_FF + (c + 1) * FF_TN)
        g_slot = 2 * (c % 2)
        v_slot = g_slot + 1
        h = h_ref[...]
        u_ref[g_slot] = jnp.dot(h, wu_ref[:, g_cols], preferred_element_type=F32)
        u_ref[v_slot] = jnp.dot(h, wu_ref[:, v_cols], preferred_element_type=F32)
        g = conv(g_slot, g_cols)
        val = conv(v_slot, v_cols)
        a_ref[:, g_cols] = (g * _sigmoid(g) * val).astype(BF16)

    out = jnp.dot(a_ref[...], wd_ref[...], preferred_element_type=F32)
    o_ref[...] = x_ref[...] + gt_ref[...] * out


def _ffn(x2d, shift, scale, gate, norm_g, w_up, cw, cb, w_down, *, tm, seq_len):
    rows, d = x2d.shape
    tps = seq_len // tm
    nb = shift.shape[0]
    hb = tm // HALO
    n_halo = rows // HALO

    def bidx(i):
        return (i // tps) % nb

    vec = pl.BlockSpec((None, 1, d), lambda i: (bidx(i), 0, 0))
    return pl.pallas_call(
        functools.partial(_ffn_kernel, tm=tm, tps=tps),
        grid=(rows // tm,),
        in_specs=[
            pl.BlockSpec((tm, d), lambda i: (i, 0)),
            pl.BlockSpec((HALO, d), lambda i: (jnp.maximum(i * hb - 1, 0), 0)),
            pl.BlockSpec((HALO, d), lambda i: (jnp.minimum((i + 1) * hb, n_halo - 1), 0)),
            vec, vec, vec,
            _resident((1, d)),
            _resident((d, 2 * D_FF)),
            _resident((3, 2 * D_FF)),
            _resident((1, 2 * D_FF)),
            _resident((D_FF, d)),
        ],
        out_specs=pl.BlockSpec((tm, d), lambda i: (i, 0)),
        out_shape=jax.ShapeDtypeStruct((rows, d), F32),
        scratch_shapes=[
            pltpu.VMEM((tm + 2 * HALO, d), BF16),
            pltpu.VMEM((4, tm + 2 * HALO, FF_TN), F32),
            pltpu.VMEM((tm, D_FF), BF16),
        ],
        compiler_params=pltpu.CompilerParams(
            dimension_semantics=("arbitrary",), vmem_limit_bytes=VMEM_LIMIT),
        name="ffn",
    )(x2d, x2d, x2d, shift, scale, gate, norm_g, w_up, cw, cb, w_down)


def kernel(x, c, ctx, c_ctx, ada_w, ada_b, norm1_g, norm2_g, w_in, sgu_ln_g, sgu_w, sgu_b, q_norm_g, k_norm_g, lam_q1, lam_k1, lam_q2, lam_k2, subln_g, conv_w, w_br_a, w_br_b, w_br_c, w_out, ffn_up, ffn_conv_w, ffn_conv_b, ffn_down):
    batch, seq, d = x.shape
    ctx_len = ctx.shape[1]
    depth = ada_w.shape[0]
    assert d == D_MODEL and seq % GRID_W == 0

    tm_lat = min(512, seq)
    tm_mrg = min(512, seq)
    tm_ctx = ctx_len
    tq_lat = min(512, seq)
    tk = 512
    assert seq % tm_lat == 0 and seq % tq_lat == 0 and ctx_len % HALO == 0

    perm = _proj_column_order()
    w_in_b = jnp.take(w_in, perm, axis=2).astype(BF16)
    wvt_b = jnp.swapaxes(w_in[:, :, _REF_B_V:_REF_B_V + B_WIDTH], 1, 2).astype(BF16)
    wa_b, wb_b, wc_b, wo_b = (w.astype(BF16) for w in (w_br_a, w_br_b, w_br_c, w_out))
    up_b = ffn_up.astype(BF16)
    down_b = ffn_down.astype(BF16)
    sw_b = sgu_w.astype(BF16)
    sgu_bias = jnp.repeat(jnp.swapaxes(sgu_b, 1, 2), LANES, axis=2)
    gmat = jnp.asarray(np.kron(np.eye(PROJ_TN // B_QK_DIM), np.full((B_QK_DIM, B_QK_DIM), 1.0 / B_QK_DIM)), BF16)
    cos, sin = _rope_tables(seq)
    cos_ctx = jnp.ones((tm_ctx, LANES), F32)
    sin_ctx = jnp.zeros((tm_ctx, LANES), F32)
    q_norm_cap = 1.01 * math.sqrt(B_QK_DIM) * QK_SCALE_LOG2E * jnp.max(jnp.abs(q_norm_g), axis=1, keepdims=True)
    lam_vecs = jnp.stack([lam_q1, lam_k1, lam_q2, lam_k2, jnp.broadcast_to(q_norm_cap, lam_q1.shape)],
                         axis=1).astype(F32)

    pad = (-(batch + 1)) % 8
    cc = jnp.concatenate([c, c_ctx[None, :], jnp.zeros((pad, d), F32)], axis=0)
    mod = _modulation(cc, ada_w, ada_b)
    mod = mod.reshape(depth, cc.shape[0], 6, 1, d)

    xs = x.reshape(batch * seq, d)
    cs = ctx.reshape(batch * ctx_len, d)

    for l in range(depth):
        last = l == depth - 1
        lam_init = 0.8 - 0.6 * math.exp(-0.3 * l)
        lat = [mod[l, :batch, k] for k in range(6)]
        cm = [mod[l, batch:batch + 1, k] for k in range(6)]
        n1 = norm1_g[l].reshape(1, d)
        n2 = norm2_g[l].reshape(1, d)
        qg = jnp.tile(q_norm_g[l], PROJ_TN // B_QK_DIM).reshape(1, PROJ_TN)
        kg = jnp.tile(k_norm_g[l], PROJ_TN // B_QK_DIM).reshape(1, PROJ_TN)
        lng = sgu_ln_g[l].reshape(1, A_WIDTH)
        subg = subln_g[l].reshape(B_V_DIM, 1)

        z_lat, vt_lat = _proj(xs, lat[0], lat[1], n1, w_in_b[l], wvt_b[l], qg, kg, lng, cos, sin, gmat,
                              tm=tm_lat, seq_len=seq, use_rope=True)
        z_ctx, vt_ctx = _proj(cs, cm[0], cm[1], n1, w_in_b[l], wvt_b[l], qg, kg, lng, cos_ctx, sin_ctx, gmat,
                              tm=tm_ctx, seq_len=ctx_len, use_rope=False)

        yb_lat = _attention(z_lat, [(z_lat, vt_lat, seq), (z_ctx, vt_ctx, ctx_len)], lam_vecs[l], subg,
                            batch=batch, seq_len=seq, tq=tq_lat, tk=tk, lam_init=lam_init)
        merge_w = (sw_b[l], sgu_bias[l], conv_w[l], wa_b[l], wb_b[l], wc_b[l], wo_b[l])
        ffn_w = (up_b[l], ffn_conv_w[l], ffn_conv_b[l].reshape(1, 2 * D_FF), down_b[l])
        xs = _merge(z_lat, yb_lat, xs, lat[2], *merge_w, tm=tm_mrg, seq_len=seq)
        xs = _ffn(xs, lat[3], lat[4], lat[5], n2, *ffn_w, tm=tm_lat, seq_len=seq)

        if not last:
            yb_ctx = _attention(z_ctx, [(z_ctx, vt_ctx, ctx_len)], lam_vecs[l], subg,
                                batch=batch, seq_len=ctx_len, tq=tm_ctx, tk=tk, lam_init=lam_init)
            cs = _merge(z_ctx, yb_ctx, cs, cm[2], *merge_w, tm=tm_ctx, seq_len=ctx_len)
            cs = _ffn(cs, cm[3], cm[4], cm[5], n2, *ffn_w, tm=tm_ctx, seq_len=ctx_len)

    return xs.reshape(batch, seq, d)
```

```python
import functools
import math

import jax
import jax.numpy as jnp
import numpy as np
from jax import lax
from jax.experimental import pallas as pl
from jax.experimental.pallas import tpu as pltpu

F32 = jnp.float32
BF16 = jnp.bfloat16

D_MODEL = 1024
GRID_W = 64
A_WIDTH = 512
A_GROUPS = 4
CHUNK = 128
B_HEADS = 8
B_QK_DIM = 64
B_V_DIM = 128
B_WIDTH = B_HEADS * B_V_DIM
ROPE_THETA = 10000.0
C_WIDTH = 512
N_BRANCH = 3
D_FF = 2816
EPS = 1e-6

_REF_A_U = 0
_REF_A_V = 512
_REF_B_Q = 1024
_REF_B_K = 2048
_REF_B_V = 3072
_REF_C_IN = 4096
_REF_GATE = 5632

PROJ_TN = 512
_J_U = 0
_J_V = 1
_J_Q = 2
_J_K = 4
_J_GATE = 6
_J_CIN = 12
_J_VALT = 15
_N_Z_TILES = 15
_N_PROJ_STEPS = 17
Z_COLS = _N_Z_TILES * PROJ_TN

HALO = 16
LANES = 128
SUBLANES = 8
FF_TN = 256
QK_SCALE_LOG2E = (B_QK_DIM ** -0.5) * math.log2(math.e)
SAFE_EXP2_RANGE = 60.0
VMEM_LIMIT = 56 * 1024 * 1024
_NT = (((1,), (1,)), ((), ()))


def _proj_column_order():
    idx = []
    idx += list(range(_REF_A_U, _REF_A_U + 512))
    idx += list(range(_REF_A_V, _REF_A_V + 512))
    for base in (_REF_B_Q, _REF_B_K):
        for h in range(B_HEADS):
            for a in range(2):
                start = base + a * B_HEADS * B_QK_DIM + h * B_QK_DIM
                idx += list(range(start, start + B_QK_DIM))
    idx += list(range(_REF_GATE, _REF_GATE + N_BRANCH * D_MODEL))
    idx += list(range(_REF_C_IN, _REF_C_IN + 3 * C_WIDTH))
    assert len(idx) == Z_COLS
    return np.asarray(idx, dtype=np.int32)


def _rope_tables(n_tokens):
    rows = n_tokens // GRID_W
    row = jnp.repeat(jnp.arange(rows, dtype=F32), GRID_W)
    col = jnp.tile(jnp.arange(GRID_W, dtype=F32), rows)
    n_freq = B_QK_DIM // 4
    inv = ROPE_THETA ** (-jnp.arange(n_freq, dtype=F32) / n_freq)
    ang_r = row[:, None] * inv
    ang_c = col[:, None] * inv
    cos64 = jnp.concatenate([jnp.cos(ang_r), jnp.cos(ang_r), jnp.cos(ang_c), jnp.cos(ang_c)], axis=1)
    sin64 = jnp.concatenate([-jnp.sin(ang_r), jnp.sin(ang_r), -jnp.sin(ang_c), jnp.sin(ang_c)], axis=1)
    return jnp.tile(cos64, (1, 2)), jnp.tile(sin64, (1, 2))


def _mod_kernel(cc_ref, w_ref, b_ref, o_ref):
    cc = cc_ref[...]
    s = cc * jax.nn.sigmoid(cc)
    o_ref[...] = jnp.dot(s, w_ref[...], preferred_element_type=F32,
                         precision=lax.Precision.HIGHEST) + b_ref[...]


def _modulation(cc, ada_w, ada_b):
    depth, d, n = ada_w.shape
    tn = 1536
    rows = cc.shape[0]
    return pl.pallas_call(
        _mod_kernel,
        grid=(depth, n // tn),
        in_specs=[
            pl.BlockSpec((rows, d), lambda l, j: (0, 0)),
            pl.BlockSpec((None, d, tn), lambda l, j: (l, 0, j)),
            pl.BlockSpec((None, 1, tn), lambda l, j: (l, 0, j)),
        ],
        out_specs=pl.BlockSpec((None, rows, tn), lambda l, j: (l, 0, j)),
        out_shape=jax.ShapeDtypeStruct((depth, rows, n), F32),
        compiler_params=pltpu.CompilerParams(
            dimension_semantics=("arbitrary", "arbitrary"), vmem_limit_bytes=VMEM_LIMIT),
        name="modulation",
    )(cc, ada_w, ada_b.reshape(depth, 1, n))


def _norm_modulate(x, gain, shift, scale):
    ms = jnp.mean(x * x, axis=-1, keepdims=True)
    y = x * lax.rsqrt(ms + EPS) * gain
    return y * (1.0 + scale) + shift


def _sigmoid(x):
    return 0.5 * jnp.tanh(0.5 * x) + 0.5


def _proj_kernel(x_ref, sh_ref, sc_ref, ng_ref, w_ref, wvt_ref, qg_ref, kg_ref, lng_ref,
                 cos_ref, sin_ref, gmat_ref, o_ref, vt_ref, h_ref, *, tm, use_rope):
    h_ref[...] = _norm_modulate(x_ref[...], ng_ref[...], sh_ref[...], sc_ref[...]).astype(BF16)

    def qk_epilogue(acc, cols, gain, scale):
        ms = jnp.dot((acc * acc).astype(BF16), gmat_ref[...], preferred_element_type=F32)
        y = acc * lax.rsqrt(ms + EPS) * gain
        if not use_rope:
            o_ref[:, cols] = (y * scale).astype(BF16)
            return
        lane = lax.broadcasted_iota(jnp.int32, (tm, LANES), 1)
        first_half = (lane & 16) == 0
        cos = cos_ref[...]
        sin = sin_ref[...]
        for c in range(PROJ_TN // LANES):
            yc = y[:, c * LANES:(c + 1) * LANES]
            partner = jnp.where(first_half, pltpu.roll(yc, LANES - 16, 1), pltpu.roll(yc, 16, 1))
            lo = cols.start + c * LANES
            o_ref[:, lo:lo + LANES] = ((yc * cos + partner * sin) * scale).astype(BF16)

    for j in range(_N_Z_TILES):
        cols = slice(j * PROJ_TN, (j + 1) * PROJ_TN)
        acc = jnp.dot(h_ref[...], w_ref[:, cols], preferred_element_type=F32)
        if j == _J_U:
            o_ref[:, cols] = jax.nn.gelu(acc).astype(BF16)
        elif j == _J_V:
            g = jax.nn.gelu(acc)
            mu = jnp.mean(g, axis=-1, keepdims=True)
            c = g - mu
            var = jnp.mean(c * c, axis=-1, keepdims=True)
            o_ref[:, cols] = (c * lax.rsqrt(var + EPS) * lng_ref[...]).astype(BF16)
        elif j < _J_K:
            qk_epilogue(acc, cols, qg_ref[...], QK_SCALE_LOG2E)
        elif j < _J_GATE:
            qk_epilogue(acc, cols, kg_ref[...], 1.0)
        elif j < _J_CIN:
            o_ref[:, cols] = _sigmoid(acc).astype(BF16)
        else:
            o_ref[:, cols] = acc.astype(BF16)

    for r in range(B_WIDTH // PROJ_TN):
        rs = slice(r * PROJ_TN, (r + 1) * PROJ_TN)
        vt_ref[rs, :] = lax.dot_general(wvt_ref[rs, :], h_ref[...], _NT,
                                        preferred_element_type=F32).astype(BF16)


def _resident(shape, layer=None):
    if layer is None:
        return pl.BlockSpec(shape, lambda *_: (0,) * len(shape), pipeline_mode=pl.Buffered(1))
    return pl.BlockSpec((None,) + tuple(shape), lambda *_: (layer,) + (0,) * len(shape),
                        pipeline_mode=pl.Buffered(1))


def _proj(x2d, shift, scale, norm_g, w, wvt, qg, kg, lng, cos, sin, gmat, *, tm, seq_len, use_rope, layer):
    rows, d = x2d.shape
    tps = seq_len // tm
    nb = shift.shape[0]
    pos_tiles = cos.shape[0] // tm

    def bidx(i):
        return (i // tps) % nb

    return pl.pallas_call(
        functools.partial(_proj_kernel, tm=tm, use_rope=use_rope),
        grid=(rows // tm,),
        in_specs=[
            pl.BlockSpec((tm, d), lambda i: (i, 0)),
            pl.BlockSpec((None, 1, d), lambda i: (bidx(i), 0, 0)),
            pl.BlockSpec((None, 1, d), lambda i: (bidx(i), 0, 0)),
            _resident((1, d)),
            _resident((d, Z_COLS), layer),
            _resident((B_WIDTH, d), layer),
            _resident((1, PROJ_TN)),
            _resident((1, PROJ_TN)),
            _resident((1, PROJ_TN)),
            pl.BlockSpec((tm, LANES), lambda i: (i % pos_tiles, 0)),
            pl.BlockSpec((tm, LANES), lambda i: (i % pos_tiles, 0)),
            _resident((PROJ_TN, PROJ_TN)),
        ],
        out_specs=[
            pl.BlockSpec((tm, Z_COLS), lambda i: (i, 0)),
            pl.BlockSpec((B_WIDTH, tm), lambda i: (0, i)),
        ],
        out_shape=[
            jax.ShapeDtypeStruct((rows, Z_COLS), BF16),
            jax.ShapeDtypeStruct((B_WIDTH, rows), BF16),
        ],
        scratch_shapes=[pltpu.VMEM((tm, d), BF16)],
        compiler_params=pltpu.CompilerParams(
            dimension_semantics=("arbitrary",), vmem_limit_bytes=VMEM_LIMIT),
        name="proj",
    )(x2d, shift, scale, norm_g, w, wvt, qg, kg, lng, cos, sin, gmat)


def _attn_kernel(*refs, tq, tk, src_rows, lam_init, layer):
    n_src = len(src_rows)
    q_ref = refs[0]
    kv_refs = [(refs[1 + 2 * s], refs[2 + 2 * s]) for s in range(n_src)]
    cap_ref, lam_ref, subg_ref, o_ref, p1_ref, p2_ref = refs[1 + 2 * n_src:]

    lane_k = lax.broadcasted_iota(jnp.int32, (1, LANES), 1)
    map1 = lane_k < B_QK_DIM

    q = q_ref[...]
    zero = jnp.zeros_like(q)
    q1 = jnp.where(map1, q, zero)
    q2 = jnp.where(map1, zero, q)

    cap = cap_ref[layer]
    safe = cap < SAFE_EXP2_RANGE

    def scores_t(qm, k_ref, start, size):
        kb = k_ref[start:start + size, :]
        return lax.dot_general(kb, qm, _NT, preferred_element_type=F32)

    src_offsets = [sum(src_rows[:s]) for s in range(n_src)]

    def blocks(fn, carry):
        for (k_ref, _), rows, offset in zip(kv_refs, src_rows, src_offsets):
            size = min(tk, rows)
            for b in range(rows // size):
                carry = fn(k_ref, b * size, size, offset + b * size, carry)
        return carry

    def fold(p, op):
        acc = p[0:SUBLANES, :]
        for r in range(1, p.shape[0] // SUBLANES):
            acc = op(acc, p[r * SUBLANES:(r + 1) * SUBLANES, :])
        return acc

    def finish(shift1, shift2):
        def step(k_ref, start, size, base, carry):
            l1, l2 = carry
            p1 = jnp.exp2(scores_t(q1, k_ref, start, size) - shift1)
            p2 = jnp.exp2(scores_t(q2, k_ref, start, size) - shift2)
            p1_ref[base:base + size, :] = p1.astype(BF16)
            p2_ref[base:base + size, :] = p2.astype(BF16)
            return l1 + fold(p1, jnp.add), l2 + fold(p2, jnp.add)

        zl = jnp.zeros((SUBLANES, tq), F32)
        l1, l2 = blocks(step, (zl, zl))

        o1 = jnp.zeros((B_V_DIM, tq), F32)
        o2 = jnp.zeros((B_V_DIM, tq), F32)
        for (_, vt_ref), rows, base in zip(kv_refs, src_rows, src_offsets):
            vt = vt_ref[...]
            o1 = o1 + jnp.dot(vt, p1_ref[base:base + rows, :], preferred_element_type=F32)
            o2 = o2 + jnp.dot(vt, p2_ref[base:base + rows, :], preferred_element_type=F32)

        r1 = 1.0 / jnp.sum(l1, axis=0, keepdims=True)
        r2 = 1.0 / jnp.sum(l2, axis=0, keepdims=True)
        lv = lam_ref[...]
        lam = (jnp.exp(jnp.sum(lv[0:1] * lv[1:2], axis=1, keepdims=True))
               - jnp.exp(jnp.sum(lv[2:3] * lv[3:4], axis=1, keepdims=True)) + lam_init)
        a = o1 * r1 - o2 * (lam * r2)
        ms = jnp.mean(a * a, axis=0, keepdims=True)
        y = a * lax.rsqrt(ms + EPS) * subg_ref[...] * (1.0 - lam_init)
        o_ref[...] = y.T.astype(BF16)

    @pl.when(safe)
    def _():
        finish(cap, cap)

    @pl.when(jnp.logical_not(safe))
    def _():
        def mx_step(k_ref, start, size, base, carry):
            m1, m2 = carry
            m1 = jnp.maximum(m1, fold(scores_t(q1, k_ref, start, size), jnp.maximum))
            m2 = jnp.maximum(m2, fold(scores_t(q2, k_ref, start, size), jnp.maximum))
            return m1, m2

        neg = jnp.full((SUBLANES, tq), -jnp.inf, F32)
        m1, m2 = blocks(mx_step, (neg, neg))
        finish(jnp.max(m1, axis=0, keepdims=True), jnp.max(m2, axis=0, keepdims=True))


def _attention(zq, kv_sources, score_cap, lam_vecs, subg, *, batch, seq_len, tq, tk, lam_init, layer):
    rows = zq.shape[0]
    nq = seq_len // tq
    q_col = _J_Q * PROJ_TN // LANES
    k_col = _J_K * PROJ_TN // LANES

    in_specs = [pl.BlockSpec((tq, LANES), lambda b, h, i: (b * nq + i, q_col + h))]
    args = [zq]
    src_rows = []
    for z, vt, n in kv_sources:
        in_specs.append(pl.BlockSpec((n, LANES), lambda b, h, i: (b, k_col + h)))
        in_specs.append(pl.BlockSpec((B_V_DIM, n), lambda b, h, i: (h, b)))
        args += [z, vt]
        src_rows.append(n)
    in_specs.append(pl.BlockSpec(memory_space=pltpu.SMEM))
    in_specs.append(pl.BlockSpec(lam_vecs.shape, lambda b, h, i: (0, 0)))
    in_specs.append(pl.BlockSpec((B_V_DIM, 1), lambda b, h, i: (0, 0)))
    args += [score_cap, lam_vecs, subg]

    return pl.pallas_call(
        functools.partial(_attn_kernel, tq=tq, tk=tk, src_rows=tuple(src_rows), lam_init=lam_init,
                          layer=layer),
        grid=(batch, B_HEADS, nq),
        in_specs=in_specs,
        out_specs=pl.BlockSpec((tq, LANES), lambda b, h, i: (b * nq + i, h)),
        out_shape=jax.ShapeDtypeStruct((rows, B_WIDTH), BF16),
        scratch_shapes=[
            pltpu.VMEM((sum(src_rows), tq), BF16),
            pltpu.VMEM((sum(src_rows), tq), BF16),
        ],
        compiler_params=pltpu.CompilerParams(
            dimension_semantics=("arbitrary", "arbitrary", "arbitrary"), vmem_limit_bytes=VMEM_LIMIT),
        name="attn",
    )(*args)


def _merge_kernel(u_ref, v_ref, yb_ref, ga_ref, gb_ref, gc_ref, cb_ref, cg_ref, cx_ref,
                  pg_ref, px_ref, ng_ref, nx_ref, x_ref, g1_ref,
                  sw_ref, sb_ref, cw_ref, wa_ref, wb_ref, wc_ref, wo_ref,
                  o_ref, ya_ref, *, tm, tps):
    i = pl.program_id(0)
    first = (i % tps) == 0
    last = (i % tps) == (tps - 1)

    for c in range(tm // CHUNK):
        rs = slice(c * CHUNK, (c + 1) * CHUNK)
        for g in range(A_GROUPS):
            cs = slice(g * LANES, (g + 1) * LANES)
            mixed = jnp.dot(sw_ref[g], v_ref[rs, cs], preferred_element_type=F32) + sb_ref[:, cs]
            ya_ref[rs, cs] = (u_ref[rs, cs].astype(F32) * mixed).astype(BF16)

    t = cg_ref[...].astype(F32) * cx_ref[...].astype(F32)
    t_prev_row = pg_ref[HALO - 1:HALO, :].astype(F32) * px_ref[HALO - 1:HALO, :].astype(F32)
    t_next_row = ng_ref[0:1, :].astype(F32) * nx_ref[0:1, :].astype(F32)
    t_prev_row = jnp.where(first, 0.0, t_prev_row)
    t_next_row = jnp.where(last, 0.0, t_next_row)
    row = lax.broadcasted_iota(jnp.int32, (tm, C_WIDTH), 0)
    t_prev = jnp.where(row == 0, t_prev_row, pltpu.roll(t, 1, 0))
    t_next = jnp.where(row == tm - 1, t_next_row, pltpu.roll(t, tm - 1, 0))
    cw = cw_ref[...]
    y_c = cb_ref[...].astype(F32) * (t_prev * cw[0:1] + t * cw[1:2] + t_next * cw[2:3])

    m = ga_ref[...].astype(F32) * jnp.dot(ya_ref[...], wa_ref[...], preferred_element_type=F32)
    m = m + gb_ref[...].astype(F32) * jnp.dot(yb_ref[...], wb_ref[...], preferred_element_type=F32)
    m = m + gc_ref[...].astype(F32) * jnp.dot(y_c.astype(BF16), wc_ref[...], preferred_element_type=F32)
    out = jnp.dot(m.astype(BF16), wo_ref[...], preferred_element_type=F32)
    o_ref[...] = x_ref[...] + g1_ref[...] * out


def _merge(z, yb, x2d, g1, sw, sb, cw, wa, wb, wc, wo, *, tm, seq_len, layer):
    rows, d = x2d.shape
    tps = seq_len // tm
    nb = g1.shape[0]
    hb = tm // HALO
    n_halo = rows // HALO
    gate_col = _J_GATE * PROJ_TN // D_MODEL

    def bidx(i):
        return (i // tps) % nb

    def prev_blk(i):
        return jnp.maximum(i * hb - 1, 0)

    def next_blk(i):
        return jnp.minimum((i + 1) * hb, n_halo - 1)

    return pl.pallas_call(
        functools.partial(_merge_kernel, tm=tm, tps=tps),
        grid=(rows // tm,),
        in_specs=[
            pl.BlockSpec((tm, A_WIDTH), lambda i: (i, _J_U)),
            pl.BlockSpec((tm, A_WIDTH), lambda i: (i, _J_V)),
            pl.BlockSpec((tm, B_WIDTH), lambda i: (i, 0)),
            pl.BlockSpec((tm, d), lambda i: (i, gate_col)),
            pl.BlockSpec((tm, d), lambda i: (i, gate_col + 1)),
            pl.BlockSpec((tm, d), lambda i: (i, gate_col + 2)),
            pl.BlockSpec((tm, C_WIDTH), lambda i: (i, _J_CIN)),
            pl.BlockSpec((tm, C_WIDTH), lambda i: (i, _J_CIN + 1)),
            pl.BlockSpec((tm, C_WIDTH), lambda i: (i, _J_CIN + 2)),
            pl.BlockSpec((HALO, C_WIDTH), lambda i: (prev_blk(i), _J_CIN + 1)),
            pl.BlockSpec((HALO, C_WIDTH), lambda i: (prev_blk(i), _J_CIN + 2)),
            pl.BlockSpec((HALO, C_WIDTH), lambda i: (next_blk(i), _J_CIN + 1)),
            pl.BlockSpec((HALO, C_WIDTH), lambda i: (next_blk(i), _J_CIN + 2)),
            pl.BlockSpec((tm, d), lambda i: (i, 0)),
            pl.BlockSpec((None, 1, d), lambda i: (bidx(i), 0, 0)),
            _resident((A_GROUPS, CHUNK, CHUNK)),
            _resident((CHUNK, A_WIDTH)),
            _resident((3, C_WIDTH)),
            _resident((A_WIDTH, d), layer),
            _resident((B_WIDTH, d), layer),
            _resident((C_WIDTH, d), layer),
            _resident((d, d), layer),
        ],
        out_specs=pl.BlockSpec((tm, d), lambda i: (i, 0)),
        out_shape=jax.ShapeDtypeStruct((rows, d), F32),
        scratch_shapes=[pltpu.VMEM((tm, A_WIDTH), BF16)],
        compiler_params=pltpu.CompilerParams(
            dimension_semantics=("arbitrary",), vmem_limit_bytes=VMEM_LIMIT),
        name="merge",
    )(z, z, yb, z, z, z, z, z, z, z, z, z, z, x2d, g1, sw, sb, cw, wa, wb, wc, wo)


def _ffn_kernel(x_ref, xp_ref, xn_ref, sh_ref, sc_ref, gt_ref, ng_ref,
                wu_ref, cw_ref, cb_ref, wd_ref, o_ref, h_ref, u_ref, a_ref, *, tm, tps):
    i = pl.program_id(0)

    def nm(x):
        return _norm_modulate(x, ng_ref[...], sh_ref[...], sc_ref[...])

    first = (i % tps) == 0
    last = (i % tps) == (tps - 1)
    h_ref[HALO:HALO + tm, :] = nm(x_ref[...]).astype(BF16)
    h_ref[0:HALO, :] = jnp.where(first, 0.0, nm(xp_ref[...])).astype(BF16)
    h_ref[HALO + tm:, :] = jnp.where(last, 0.0, nm(xn_ref[...])).astype(BF16)

    def conv(slot, cols):
        cw = cw_ref[:, cols]
        return (u_ref[slot, HALO - 1:HALO - 1 + tm, :] * cw[0:1] + u_ref[slot, HALO:HALO + tm, :] * cw[1:2]
                + u_ref[slot, HALO + 1:HALO + 1 + tm, :] * cw[2:3] + cb_ref[:, cols])

    for c in range(D_FF // FF_TN):
        g_cols = slice(c * FF_TN, (c + 1) * FF_TN)
        v_cols = slice(D_FF + c * FF_TN, D_FF + (c + 1) * FF_TN)
        g_slot = 2 * (c % 2)
        v_slot = g_slot + 1
        h = h_ref[...]
        u_ref[g_slot] = jnp.dot(h, wu_ref[:, g_cols], preferred_element_type=F32)
        u_ref[v_slot] = jnp.dot(h, wu_ref[:, v_cols], preferred_element_type=F32)
        g = conv(g_slot, g_cols)
        val = conv(v_slot, v_cols)
        a_ref[:, g_cols] = (g * _sigmoid(g) * val).astype(BF16)

    out = jnp.dot(a_ref[...], wd_ref[...], preferred_element_type=F32)
    o_ref[...] = x_ref[...] + gt_ref[...] * out


def _ffn(x2d, shift, scale, gate, norm_g, w_up, cw, cb, w_down, *, tm, seq_len, layer):
    rows, d = x2d.shape
    tps = seq_len // tm
    nb = shift.shape[0]
    hb = tm // HALO
    n_halo = rows // HALO

    def bidx(i):
        return (i // tps) % nb

    vec = pl.BlockSpec((None, 1, d), lambda i: (bidx(i), 0, 0))
    return pl.pallas_call(
        functools.partial(_ffn_kernel, tm=tm, tps=tps),
        grid=(rows // tm,),
        in_specs=[
            pl.BlockSpec((tm, d), lambda i: (i, 0)),
            pl.BlockSpec((HALO, d), lambda i: (jnp.maximum(i * hb - 1, 0), 0)),
            pl.BlockSpec((HALO, d), lambda i: (jnp.minimum((i + 1) * hb, n_halo - 1), 0)),
            vec, vec, vec,
            _resident((1, d)),
            _resident((d, 2 * D_FF), layer),
            _resident((3, 2 * D_FF)),
            _resident((1, 2 * D_FF)),
            _resident((D_FF, d), layer),
        ],
        out_specs=pl.BlockSpec((tm, d), lambda i: (i, 0)),
        out_shape=jax.ShapeDtypeStruct((rows, d), F32),
        scratch_shapes=[
            pltpu.VMEM((tm + 2 * HALO, d), BF16),
            pltpu.VMEM((4, tm + 2 * HALO, FF_TN), F32),
            pltpu.VMEM((tm, D_FF), BF16),
        ],
        compiler_params=pltpu.CompilerParams(
            dimension_semantics=("arbitrary",), vmem_limit_bytes=VMEM_LIMIT),
        name="ffn",
    )(x2d, x2d, x2d, shift, scale, gate, norm_g, w_up, cw, cb, w_down)


def kernel(x, c, ctx, c_ctx, ada_w, ada_b, norm1_g, norm2_g, w_in, sgu_ln_g, sgu_w, sgu_b, q_norm_g, k_norm_g, lam_q1, lam_k1, lam_q2, lam_k2, subln_g, conv_w, w_br_a, w_br_b, w_br_c, w_out, ffn_up, ffn_conv_w, ffn_conv_b, ffn_down):
    batch, seq, d = x.shape
    ctx_len = ctx.shape[1]
    depth = ada_w.shape[0]
    assert d == D_MODEL and seq % GRID_W == 0

    tm_lat = min(512, seq)
    tm_mrg = min(512, seq)
    tm_ctx = ctx_len
    tq_lat = min(1024, seq)
    tk = 512
    assert seq % tm_lat == 0 and seq % tq_lat == 0 and ctx_len % HALO == 0

    perm = _proj_column_order()
    w_in_b = jnp.take(w_in, perm, axis=2).astype(BF16)
    wvt_b = jnp.swapaxes(w_in[:, :, _REF_B_V:_REF_B_V + B_WIDTH], 1, 2).astype(BF16)
    wa_b, wb_b, wc_b, wo_b = (w.astype(BF16) for w in (w_br_a, w_br_b, w_br_c, w_out))
    up_b = ffn_up.astype(BF16)
    down_b = ffn_down.astype(BF16)
    sw_b = sgu_w.astype(BF16)
    sgu_bias = jnp.repeat(jnp.swapaxes(sgu_b, 1, 2), LANES, axis=2)
    gmat = jnp.asarray(np.kron(np.eye(PROJ_TN // B_QK_DIM), np.full((B_QK_DIM, B_QK_DIM), 1.0 / B_QK_DIM)), BF16)
    cos, sin = _rope_tables(seq)
    cos_ctx = jnp.ones((tm_ctx, LANES), F32)
    sin_ctx = jnp.zeros((tm_ctx, LANES), F32)
    lam_vecs = jnp.stack([lam_q1, lam_k1, lam_q2, lam_k2], axis=1).astype(F32)
    score_cap = (1.02 * B_QK_DIM * QK_SCALE_LOG2E
                 * jnp.max(jnp.abs(q_norm_g), axis=1) * jnp.max(jnp.abs(k_norm_g), axis=1)).astype(F32)

    pad = (-(batch + 1)) % 8
    cc = jnp.concatenate([c, c_ctx[None, :], jnp.zeros((pad, d), F32)], axis=0)
    mod = _modulation(cc, ada_w, ada_b)
    mod = mod.reshape(depth, cc.shape[0], 6, 1, d)

    xs = x.reshape(batch * seq, d)
    cs = ctx.reshape(batch * ctx_len, d)

    for l in range(depth):
        last = l == depth - 1
        lam_init = 0.8 - 0.6 * math.exp(-0.3 * l)
        lat = [mod[l, :batch, k] for k in range(6)]
        cm = [mod[l, batch:batch + 1, k] for k in range(6)]
        n1 = norm1_g[l].reshape(1, d)
        n2 = norm2_g[l].reshape(1, d)
        qg = jnp.tile(q_norm_g[l], PROJ_TN // B_QK_DIM).reshape(1, PROJ_TN)
        kg = jnp.tile(k_norm_g[l], PROJ_TN // B_QK_DIM).reshape(1, PROJ_TN)
        lng = sgu_ln_g[l].reshape(1, A_WIDTH)
        subg = subln_g[l].reshape(B_V_DIM, 1)

        z_lat, vt_lat = _proj(xs, lat[0], lat[1], n1, w_in_b, wvt_b, qg, kg, lng, cos, sin, gmat,
                              tm=tm_lat, seq_len=seq, use_rope=True, layer=l)
        z_ctx, vt_ctx = _proj(cs, cm[0], cm[1], n1, w_in_b, wvt_b, qg, kg, lng, cos_ctx, sin_ctx, gmat,
                              tm=tm_ctx, seq_len=ctx_len, use_rope=False, layer=l)

        yb_lat = _attention(z_lat, [(z_lat, vt_lat, seq), (z_ctx, vt_ctx, ctx_len)], score_cap, lam_vecs[l], subg,
                            batch=batch, seq_len=seq, tq=tq_lat, tk=tk, lam_init=lam_init, layer=l)
        merge_w = (sw_b[l], sgu_bias[l], conv_w[l], wa_b, wb_b, wc_b, wo_b)
        ffn_w = (up_b, ffn_conv_w[l], ffn_conv_b[l].reshape(1, 2 * D_FF), down_b)
        xs = _merge(z_lat, yb_lat, xs, lat[2], *merge_w, tm=tm_mrg, seq_len=seq, layer=l)
        xs = _ffn(xs, lat[3], lat[4], lat[5], n2, *ffn_w, tm=tm_lat, seq_len=seq, layer=l)

        if not last:
            yb_ctx = _attention(z_ctx, [(z_ctx, vt_ctx, ctx_len)], score_cap, lam_vecs[l], subg,
                                batch=batch, seq_len=ctx_len, tq=tm_ctx, tk=tk, lam_init=lam_init, layer=l)
            cs = _merge(z_ctx, yb_ctx, cs, cm[2], *merge_w, tm=tm_ctx, seq_len=ctx_len, layer=l)
            cs = _ffn(cs, cm[3], cm[4], cm[5], n2, *ffn_w, tm=tm_ctx, seq_len=ctx_len, layer=l)

    return xs.reshape(batch, seq, d)
```

```python
import functools
import math

import jax
import jax.numpy as jnp
import numpy as np
from jax import lax
from jax.experimental import pallas as pl
from jax.experimental.pallas import tpu as pltpu

F32 = jnp.float32
BF16 = jnp.bfloat16

D_MODEL = 1024
GRID_W = 64
A_WIDTH = 512
A_GROUPS = 4
CHUNK = 128
B_HEADS = 8
B_QK_DIM = 64
B_V_DIM = 128
B_WIDTH = B_HEADS * B_V_DIM
ROPE_THETA = 10000.0
C_WIDTH = 512
N_BRANCH = 3
D_FF = 2816
EPS = 1e-6

_REF_A_U = 0
_REF_A_V = 512
_REF_B_Q = 1024
_REF_B_K = 2048
_REF_B_V = 3072
_REF_C_IN = 4096
_REF_GATE = 5632

PROJ_TN = 512
_J_U = 0
_J_V = 1
_J_Q = 2
_J_K = 4
_J_GATE = 6
_J_CIN = 12
_J_VALT = 15
_N_Z_TILES = 15
_N_PROJ_STEPS = 17
Z_COLS = _N_Z_TILES * PROJ_TN

HALO = 16
LANES = 128
SUBLANES = 8
MXU_DIM = 256
FF_TN = 256
QK_SCALE_LOG2E = (B_QK_DIM ** -0.5) * math.log2(math.e)
SAFE_EXP2_RANGE = 60.0
VMEM_LIMIT = 56 * 1024 * 1024
_NT = (((1,), (1,)), ((), ()))


def _proj_column_order():
    idx = []
    idx += list(range(_REF_A_U, _REF_A_U + 512))
    idx += list(range(_REF_A_V, _REF_A_V + 512))
    for base in (_REF_B_Q, _REF_B_K):
        for h in range(B_HEADS):
            for a in range(2):
                start = base + a * B_HEADS * B_QK_DIM + h * B_QK_DIM
                idx += list(range(start, start + B_QK_DIM))
    idx += list(range(_REF_GATE, _REF_GATE + N_BRANCH * D_MODEL))
    idx += list(range(_REF_C_IN, _REF_C_IN + 3 * C_WIDTH))
    assert len(idx) == Z_COLS
    return np.asarray(idx, dtype=np.int32)


def _rope_tables(n_tokens):
    rows = n_tokens // GRID_W
    row = jnp.repeat(jnp.arange(rows, dtype=F32), GRID_W)
    col = jnp.tile(jnp.arange(GRID_W, dtype=F32), rows)
    n_freq = B_QK_DIM // 4
    inv = ROPE_THETA ** (-jnp.arange(n_freq, dtype=F32) / n_freq)
    ang_r = row[:, None] * inv
    ang_c = col[:, None] * inv
    cos64 = jnp.concatenate([jnp.cos(ang_r), jnp.cos(ang_r), jnp.cos(ang_c), jnp.cos(ang_c)], axis=1)
    sin64 = jnp.concatenate([-jnp.sin(ang_r), jnp.sin(ang_r), -jnp.sin(ang_c), jnp.sin(ang_c)], axis=1)
    return jnp.tile(cos64, (1, 2)), jnp.tile(sin64, (1, 2))


def _mod_kernel(cc_ref, w_ref, b_ref, o_ref):
    cc = cc_ref[...]
    s = cc * jax.nn.sigmoid(cc)
    o_ref[...] = jnp.dot(s, w_ref[...], preferred_element_type=F32,
                         precision=lax.Precision.HIGHEST) + b_ref[...]


def _modulation(cc, ada_w, ada_b):
    depth, d, n = ada_w.shape
    tn = 1536
    rows = cc.shape[0]
    return pl.pallas_call(
        _mod_kernel,
        grid=(depth, n // tn),
        in_specs=[
            pl.BlockSpec((rows, d), lambda l, j: (0, 0)),
            pl.BlockSpec((None, d, tn), lambda l, j: (l, 0, j)),
            pl.BlockSpec((None, 1, tn), lambda l, j: (l, 0, j)),
        ],
        out_specs=pl.BlockSpec((None, rows, tn), lambda l, j: (l, 0, j)),
        out_shape=jax.ShapeDtypeStruct((depth, rows, n), F32),
        compiler_params=pltpu.CompilerParams(
            dimension_semantics=("arbitrary", "arbitrary"), vmem_limit_bytes=VMEM_LIMIT),
        name="modulation",
    )(cc, ada_w, ada_b.reshape(depth, 1, n))


def _norm_modulate(x, gain, shift, scale):
    ms = jnp.mean(x * x, axis=-1, keepdims=True)
    y = x * lax.rsqrt(ms + EPS) * gain
    return y * (1.0 + scale) + shift


def _sigmoid(x):
    return 0.5 * jnp.tanh(0.5 * x) + 0.5


def _proj_kernel(x_ref, sh_ref, sc_ref, ng_ref, w_ref, wv_ref, qg_ref, kg_ref, lng_ref,
                 cos_ref, sin_ref, gmat_ref, o_ref, vt_ref, h_ref, wvt_ref, *, tm, use_rope):
    @pl.when(pl.program_id(0) == 0)
    def _():
        for c in range(0, B_WIDTH, MXU_DIM):
            wvt_ref[c:c + MXU_DIM, :] = wv_ref[:, c:c + MXU_DIM].astype(F32).T.astype(BF16)

    h_ref[...] = _norm_modulate(x_ref[...], ng_ref[...], sh_ref[...], sc_ref[...]).astype(BF16)

    def qk_epilogue(acc, cols, gain, scale):
        sq = (acc * acc).astype(BF16)
        ms = jnp.concatenate(
            [jnp.dot(sq[:, c:c + MXU_DIM], gmat_ref[...], preferred_element_type=F32)
             for c in range(0, PROJ_TN, MXU_DIM)], axis=1)
        y = acc * lax.rsqrt(ms + EPS) * gain
        if not use_rope:
            o_ref[:, cols] = (y * scale).astype(BF16)
            return
        lane = lax.broadcasted_iota(jnp.int32, (tm, LANES), 1)
        first_half = (lane & 16) == 0
        cos = cos_ref[...]
        sin = sin_ref[...]
        for c in range(PROJ_TN // LANES):
            yc = y[:, c * LANES:(c + 1) * LANES]
            partner = jnp.where(first_half, pltpu.roll(yc, LANES - 16, 1), pltpu.roll(yc, 16, 1))
            lo = cols.start + c * LANES
            o_ref[:, lo:lo + LANES] = ((yc * cos + partner * sin) * scale).astype(BF16)

    for j in range(_N_Z_TILES):
        cols = slice(j * PROJ_TN, (j + 1) * PROJ_TN)
        acc = jnp.dot(h_ref[...], w_ref[:, cols], preferred_element_type=F32)
        if j == _J_U:
            o_ref[:, cols] = jax.nn.gelu(acc).astype(BF16)
        elif j == _J_V:
            g = jax.nn.gelu(acc)
            mu = jnp.mean(g, axis=-1, keepdims=True)
            c = g - mu
            var = jnp.mean(c * c, axis=-1, keepdims=True)
            o_ref[:, cols] = (c * lax.rsqrt(var + EPS) * lng_ref[...]).astype(BF16)
        elif j < _J_K:
            qk_epilogue(acc, cols, qg_ref[...], QK_SCALE_LOG2E)
        elif j < _J_GATE:
            qk_epilogue(acc, cols, kg_ref[...], 1.0)
        elif j < _J_CIN:
            o_ref[:, cols] = _sigmoid(acc).astype(BF16)
        else:
            o_ref[:, cols] = acc.astype(BF16)

    for r in range(B_WIDTH // PROJ_TN):
        rs = slice(r * PROJ_TN, (r + 1) * PROJ_TN)
        vt_ref[rs, :] = lax.dot_general(wvt_ref[rs, :], h_ref[...], _NT,
                                        preferred_element_type=F32).astype(BF16)


def _resident(shape, layer=None):
    if layer is None:
        return pl.BlockSpec(shape, lambda *_: (0,) * len(shape), pipeline_mode=pl.Buffered(1))
    return pl.BlockSpec((None,) + tuple(shape), lambda *_: (layer,) + (0,) * len(shape),
                        pipeline_mode=pl.Buffered(1))


def _proj(x2d, shift, scale, norm_g, w, wv, qg, kg, lng, cos, sin, gmat, *, tm, seq_len, use_rope, layer):
    rows, d = x2d.shape
    tps = seq_len // tm
    nb = shift.shape[0]
    pos_tiles = cos.shape[0] // tm

    def bidx(i):
        return (i // tps) % nb

    return pl.pallas_call(
        functools.partial(_proj_kernel, tm=tm, use_rope=use_rope),
        grid=(rows // tm,),
        in_specs=[
            pl.BlockSpec((tm, d), lambda i: (i, 0)),
            pl.BlockSpec((None, 1, d), lambda i: (bidx(i), 0, 0)),
            pl.BlockSpec((None, 1, d), lambda i: (bidx(i), 0, 0)),
            _resident((1, d)),
            _resident((d, Z_COLS), layer),
            _resident((d, B_WIDTH), layer),
            _resident((1, PROJ_TN)),
            _resident((1, PROJ_TN)),
            _resident((1, PROJ_TN)),
            pl.BlockSpec((tm, LANES), lambda i: (i % pos_tiles, 0)),
            pl.BlockSpec((tm, LANES), lambda i: (i % pos_tiles, 0)),
            _resident((MXU_DIM, MXU_DIM)),
        ],
        out_specs=[
            pl.BlockSpec((tm, Z_COLS), lambda i: (i, 0)),
            pl.BlockSpec((B_WIDTH, tm), lambda i: (0, i)),
        ],
        out_shape=[
            jax.ShapeDtypeStruct((rows, Z_COLS), BF16),
            jax.ShapeDtypeStruct((B_WIDTH, rows), BF16),
        ],
        scratch_shapes=[pltpu.VMEM((tm, d), BF16), pltpu.VMEM((B_WIDTH, d), BF16)],
        compiler_params=pltpu.CompilerParams(
            dimension_semantics=("arbitrary",), vmem_limit_bytes=VMEM_LIMIT),
        name="proj",
    )(x2d, shift, scale, norm_g, w, wv, qg, kg, lng, cos, sin, gmat)


def _attn_kernel(*refs, tq, tk, src_rows, lam_init, layer):
    n_src = len(src_rows)
    q_ref = refs[0]
    kv_refs = [(refs[1 + 2 * s], refs[2 + 2 * s]) for s in range(n_src)]
    cap_ref, lam_ref, subg_ref, o_ref, p1_ref, p2_ref = refs[1 + 2 * n_src:]

    lane_k = lax.broadcasted_iota(jnp.int32, (1, LANES), 1)
    map1 = lane_k < B_QK_DIM

    q = q_ref[...]
    zero = jnp.zeros_like(q)
    q1 = jnp.where(map1, q, zero)
    q2 = jnp.where(map1, zero, q)

    cap = cap_ref[layer]
    safe = cap < SAFE_EXP2_RANGE

    def scores_t(qm, k_ref, start, size):
        kb = k_ref[start:start + size, :]
        return lax.dot_general(kb, qm, _NT, preferred_element_type=F32)

    src_offsets = [sum(src_rows[:s]) for s in range(n_src)]

    def blocks(fn, carry):
        for (k_ref, _), rows, offset in zip(kv_refs, src_rows, src_offsets):
            size = min(tk, rows)
            for b in range(rows // size):
                carry = fn(k_ref, b * size, size, offset + b * size, carry)
        return carry

    def fold(p, op):
        acc = p[0:SUBLANES, :]
        for r in range(1, p.shape[0] // SUBLANES):
            acc = op(acc, p[r * SUBLANES:(r + 1) * SUBLANES, :])
        return acc

    def finish(shift1, shift2):
        def step(k_ref, start, size, base, carry):
            l1, l2 = carry
            p1 = jnp.exp2(scores_t(q1, k_ref, start, size) - shift1)
            p2 = jnp.exp2(scores_t(q2, k_ref, start, size) - shift2)
            p1_ref[base:base + size, :] = p1.astype(BF16)
            p2_ref[base:base + size, :] = p2.astype(BF16)
            return l1 + fold(p1, jnp.add), l2 + fold(p2, jnp.add)

        zl = jnp.zeros((SUBLANES, tq), F32)
        l1, l2 = blocks(step, (zl, zl))

        o1 = jnp.zeros((B_V_DIM, tq), F32)
        o2 = jnp.zeros((B_V_DIM, tq), F32)
        for (_, vt_ref), rows, base in zip(kv_refs, src_rows, src_offsets):
            vt = vt_ref[...]
            o1 = o1 + jnp.dot(vt, p1_ref[base:base + rows, :], preferred_element_type=F32)
            o2 = o2 + jnp.dot(vt, p2_ref[base:base + rows, :], preferred_element_type=F32)

        r1 = 1.0 / jnp.sum(l1, axis=0, keepdims=True)
        r2 = 1.0 / jnp.sum(l2, axis=0, keepdims=True)
        lv = lam_ref[...]
        lam = (jnp.exp(jnp.sum(lv[0:1] * lv[1:2], axis=1, keepdims=True))
               - jnp.exp(jnp.sum(lv[2:3] * lv[3:4], axis=1, keepdims=True)) + lam_init)
        a = o1 * r1 - o2 * (lam * r2)
        ms = jnp.mean(a * a, axis=0, keepdims=True)
        y = a * lax.rsqrt(ms + EPS) * subg_ref[...] * (1.0 - lam_init)
        o_ref[...] = y.T.astype(BF16)

    @pl.when(safe)
    def _():
        finish(cap, cap)

    @pl.when(jnp.logical_not(safe))
    def _():
        def mx_step(k_ref, start, size, base, carry):
            m1, m2 = carry
            m1 = jnp.maximum(m1, fold(scores_t(q1, k_ref, start, size), jnp.maximum))
            m2 = jnp.maximum(m2, fold(scores_t(q2, k_ref, start, size), jnp.maximum))
            return m1, m2

        neg = jnp.full((SUBLANES, tq), -jnp.inf, F32)
        m1, m2 = blocks(mx_step, (neg, neg))
        finish(jnp.max(m1, axis=0, keepdims=True), jnp.max(m2, axis=0, keepdims=True))


def _attention(zq, kv_sources, score_cap, lam_vecs, subg, *, batch, seq_len, tq, tk, lam_init, layer):
    rows = zq.shape[0]
    nq = seq_len // tq
    q_col = _J_Q * PROJ_TN // LANES
    k_col = _J_K * PROJ_TN // LANES

    in_specs = [pl.BlockSpec((tq, LANES), lambda b, h, i: (b * nq + i, q_col + h))]
    args = [zq]
    src_rows = []
    for z, vt, n in kv_sources:
        in_specs.append(pl.BlockSpec((n, LANES), lambda b, h, i: (b, k_col + h)))
        in_specs.append(pl.BlockSpec((B_V_DIM, n), lambda b, h, i: (h, b)))
        args += [z, vt]
        src_rows.append(n)
    in_specs.append(pl.BlockSpec(memory_space=pltpu.SMEM))
    in_specs.append(pl.BlockSpec(lam_vecs.shape, lambda b, h, i: (0, 0)))
    in_specs.append(pl.BlockSpec((B_V_DIM, 1), lambda b, h, i: (0, 0)))
    args += [score_cap, lam_vecs, subg]

    return pl.pallas_call(
        functools.partial(_attn_kernel, tq=tq, tk=tk, src_rows=tuple(src_rows), lam_init=lam_init,
                          layer=layer),
        grid=(batch, B_HEADS, nq),
        in_specs=in_specs,
        out_specs=pl.BlockSpec((tq, LANES), lambda b, h, i: (b * nq + i, h)),
        out_shape=jax.ShapeDtypeStruct((rows, B_WIDTH), BF16),
        scratch_shapes=[
            pltpu.VMEM((sum(src_rows), tq), BF16),
            pltpu.VMEM((sum(src_rows), tq), BF16),
        ],
        compiler_params=pltpu.CompilerParams(
            dimension_semantics=("arbitrary", "arbitrary", "arbitrary"), vmem_limit_bytes=VMEM_LIMIT),
        name="attn",
    )(*args)


def _merge_kernel(u_ref, v_ref, yb_ref, ga_ref, gb_ref, gc_ref, cb_ref, cg_ref, cx_ref,
                  pg_ref, px_ref, ng_ref, nx_ref, x_ref, g1_ref,
                  sw_ref, sb_ref, cw_ref, wa_ref, wb_ref, wc_ref, wo_ref,
                  o_ref, ya_ref, *, tm, tps):
    i = pl.program_id(0)
    first = (i % tps) == 0
    last = (i % tps) == (tps - 1)

    for c in range(tm // CHUNK):
        rs = slice(c * CHUNK, (c + 1) * CHUNK)
        for g in range(A_GROUPS):
            cs = slice(g * LANES, (g + 1) * LANES)
            mixed = jnp.dot(sw_ref[g], v_ref[rs, cs], preferred_element_type=F32) + sb_ref[:, cs]
            ya_ref[rs, cs] = (u_ref[rs, cs].astype(F32) * mixed).astype(BF16)

    t = cg_ref[...].astype(F32) * cx_ref[...].astype(F32)
    t_prev_row = pg_ref[HALO - 1:HALO, :].astype(F32) * px_ref[HALO - 1:HALO, :].astype(F32)
    t_next_row = ng_ref[0:1, :].astype(F32) * nx_ref[0:1, :].astype(F32)
    t_prev_row = jnp.where(first, 0.0, t_prev_row)
    t_next_row = jnp.where(last, 0.0, t_next_row)
    row = lax.broadcasted_iota(jnp.int32, (tm, C_WIDTH), 0)
    t_prev = jnp.where(row == 0, t_prev_row, pltpu.roll(t, 1, 0))
    t_next = jnp.where(row == tm - 1, t_next_row, pltpu.roll(t, tm - 1, 0))
    cw = cw_ref[...]
    y_c = cb_ref[...].astype(F32) * (t_prev * cw[0:1] + t * cw[1:2] + t_next * cw[2:3])

    m = ga_ref[...].astype(F32) * jnp.dot(ya_ref[...], wa_ref[...], preferred_element_type=F32)
    m = m + gb_ref[...].astype(F32) * jnp.dot(yb_ref[...], wb_ref[...], preferred_element_type=F32)
    m = m + gc_ref[...].astype(F32) * jnp.dot(y_c.astype(BF16), wc_ref[...], preferred_element_type=F32)
    out = jnp.dot(m.astype(BF16), wo_ref[...], preferred_element_type=F32)
    o_ref[...] = x_ref[...] + g1_ref[...] * out


def _merge(z, yb, x2d, g1, sw, sb, cw, wa, wb, wc, wo, *, tm, seq_len, layer):
    rows, d = x2d.shape
    tps = seq_len // tm
    nb = g1.shape[0]
    hb = tm // HALO
    n_halo = rows // HALO
    gate_col = _J_GATE * PROJ_TN // D_MODEL

    def bidx(i):
        return (i // tps) % nb

    def prev_blk(i):
        return jnp.maximum(i * hb - 1, 0)

    def next_blk(i):
        return jnp.minimum((i + 1) * hb, n_halo - 1)

    return pl.pallas_call(
        functools.partial(_merge_kernel, tm=tm, tps=tps),
        grid=(rows // tm,),
        in_specs=[
            pl.BlockSpec((tm, A_WIDTH), lambda i: (i, _J_U)),
            pl.BlockSpec((tm, A_WIDTH), lambda i: (i, _J_V)),
            pl.BlockSpec((tm, B_WIDTH), lambda i: (i, 0)),
            pl.BlockSpec((tm, d), lambda i: (i, gate_col)),
            pl.BlockSpec((tm, d), lambda i: (i, gate_col + 1)),
            pl.BlockSpec((tm, d), lambda i: (i, gate_col + 2)),
            pl.BlockSpec((tm, C_WIDTH), lambda i: (i, _J_CIN)),
            pl.BlockSpec((tm, C_WIDTH), lambda i: (i, _J_CIN + 1)),
            pl.BlockSpec((tm, C_WIDTH), lambda i: (i, _J_CIN + 2)),
            pl.BlockSpec((HALO, C_WIDTH), lambda i: (prev_blk(i), _J_CIN + 1)),
            pl.BlockSpec((HALO, C_WIDTH), lambda i: (prev_blk(i), _J_CIN + 2)),
            pl.BlockSpec((HALO, C_WIDTH), lambda i: (next_blk(i), _J_CIN + 1)),
            pl.BlockSpec((HALO, C_WIDTH), lambda i: (next_blk(i), _J_CIN + 2)),
            pl.BlockSpec((tm, d), lambda i: (i, 0)),
            pl.BlockSpec((None, 1, d), lambda i: (bidx(i), 0, 0)),
            _resident((A_GROUPS, CHUNK, CHUNK)),
            _resident((CHUNK, A_WIDTH)),
            _resident((3, C_WIDTH)),
            _resident((A_WIDTH, d), layer),
            _resident((B_WIDTH, d), layer),
            _resident((C_WIDTH, d), layer),
            _resident((d, d), layer),
        ],
        out_specs=pl.BlockSpec((tm, d), lambda i: (i, 0)),
        out_shape=jax.ShapeDtypeStruct((rows, d), F32),
        scratch_shapes=[pltpu.VMEM((tm, A_WIDTH), BF16)],
        compiler_params=pltpu.CompilerParams(
            dimension_semantics=("arbitrary",), vmem_limit_bytes=VMEM_LIMIT),
        name="merge",
    )(z, z, yb, z, z, z, z, z, z, z, z, z, z, x2d, g1, sw, sb, cw, wa, wb, wc, wo)


def _ffn_kernel(x_ref, xp_ref, xn_ref, sh_ref, sc_ref, gt_ref, ng_ref,
                wu_ref, cw_ref, cb_ref, wd_ref, o_ref, h_ref, u_ref, a_ref, *, tm, tps):
    i = pl.program_id(0)

    def nm(x):
        return _norm_modulate(x, ng_ref[...], sh_ref[...], sc_ref[...])

    first = (i % tps) == 0
    last = (i % tps) == (tps - 1)
    h_ref[HALO:HALO + tm, :] = nm(x_ref[...]).astype(BF16)
    h_ref[0:HALO, :] = jnp.where(first, 0.0, nm(xp_ref[...])).astype(BF16)
    h_ref[HALO + tm:, :] = jnp.where(last, 0.0, nm(xn_ref[...])).astype(BF16)

    def conv(slot, cols):
        cw = cw_ref[:, cols]
        return (u_ref[slot, HALO - 1:HALO - 1 + tm, :] * cw[0:1] + u_ref[slot, HALO:HALO + tm, :] * cw[1:2]
                + u_ref[slot, HALO + 1:HALO + 1 + tm, :] * cw[2:3] + cb_ref[:, cols])

    for c in range(D_FF // FF_TN):
        g_cols = slice(c * FF_TN, (c + 1) * FF_TN)
        v_cols = slice(D_FF + c * FF_TN, D_FF + (c + 1) * FF_TN)
        g_slot = 2 * (c % 2)
        v_slot = g_slot + 1
        h = h_ref[...]
        u_ref[g_slot] = jnp.dot(h, wu_ref[:, g_cols], preferred_element_type=F32)
        u_ref[v_slot] = jnp.dot(h, wu_ref[:, v_cols], preferred_element_type=F32)
        g = conv(g_slot, g_cols)
        val = conv(v_slot, v_cols)
        a_ref[:, g_cols] = (g * _sigmoid(g) * val).astype(BF16)

    out = jnp.dot(a_ref[...], wd_ref[...], preferred_element_type=F32)
    o_ref[...] = x_ref[...] + gt_ref[...] * out


def _ffn(x2d, shift, scale, gate, norm_g, w_up, cw, cb, w_down, *, tm, seq_len, layer):
    rows, d = x2d.shape
    tps = seq_len // tm
    nb = shift.shape[0]
    hb = tm // HALO
    n_halo = rows // HALO

    def bidx(i):
        return (i // tps) % nb

    vec = pl.BlockSpec((None, 1, d), lambda i: (bidx(i), 0, 0))
    return pl.pallas_call(
        functools.partial(_ffn_kernel, tm=tm, tps=tps),
        grid=(rows // tm,),
        in_specs=[
            pl.BlockSpec((tm, d), lambda i: (i, 0)),
            pl.BlockSpec((HALO, d), lambda i: (jnp.maximum(i * hb - 1, 0), 0)),
            pl.BlockSpec((HALO, d), lambda i: (jnp.minimum((i + 1) * hb, n_halo - 1), 0)),
            vec, vec, vec,
            _resident((1, d)),
            _resident((d, 2 * D_FF), layer),
            _resident((3, 2 * D_FF)),
            _resident((1, 2 * D_FF)),
            _resident((D_FF, d), layer),
        ],
        out_specs=pl.BlockSpec((tm, d), lambda i: (i, 0)),
        out_shape=jax.ShapeDtypeStruct((rows, d), F32),
        scratch_shapes=[
            pltpu.VMEM((tm + 2 * HALO, d), BF16),
            pltpu.VMEM((4, tm + 2 * HALO, FF_TN), F32),
            pltpu.VMEM((tm, D_FF), BF16),
        ],
        compiler_params=pltpu.CompilerParams(
            dimension_semantics=("arbitrary",), vmem_limit_bytes=VMEM_LIMIT),
        name="ffn",
    )(x2d, x2d, x2d, shift, scale, gate, norm_g, w_up, cw, cb, w_down)


def kernel(x, c, ctx, c_ctx, ada_w, ada_b, norm1_g, norm2_g, w_in, sgu_ln_g, sgu_w, sgu_b, q_norm_g, k_norm_g, lam_q1, lam_k1, lam_q2, lam_k2, subln_g, conv_w, w_br_a, w_br_b, w_br_c, w_out, ffn_up, ffn_conv_w, ffn_conv_b, ffn_down):
    batch, seq, d = x.shape
    ctx_len = ctx.shape[1]
    depth = ada_w.shape[0]
    assert d == D_MODEL and seq % GRID_W == 0

    tm_lat = min(512, seq)
    tm_mrg = min(512, seq)
    tm_ctx = ctx_len
    tq_lat = min(1024, seq)
    tk = 512
    assert seq % tm_lat == 0 and seq % tq_lat == 0 and ctx_len % HALO == 0

    perm = _proj_column_order()
    w_in_b = jnp.take(w_in, perm, axis=2).astype(BF16)
    wv_b = w_in[:, :, _REF_B_V:_REF_B_V + B_WIDTH].astype(BF16)
    wa_b, wb_b, wc_b, wo_b = (w.astype(BF16) for w in (w_br_a, w_br_b, w_br_c, w_out))
    up_b = ffn_up.astype(BF16)
    down_b = ffn_down.astype(BF16)
    sw_b = sgu_w.astype(BF16)
    sgu_bias = jnp.repeat(jnp.swapaxes(sgu_b, 1, 2), LANES, axis=2)
    gmat = jnp.asarray(np.kron(np.eye(MXU_DIM // B_QK_DIM), np.full((B_QK_DIM, B_QK_DIM), 1.0 / B_QK_DIM)), BF16)
    cos, sin = _rope_tables(seq)
    cos_ctx = jnp.ones((tm_ctx, LANES), F32)
    sin_ctx = jnp.zeros((tm_ctx, LANES), F32)
    lam_vecs = jnp.stack([lam_q1, lam_k1, lam_q2, lam_k2], axis=1).astype(F32)
    score_cap = (1.02 * B_QK_DIM * QK_SCALE_LOG2E
                 * jnp.max(jnp.abs(q_norm_g), axis=1) * jnp.max(jnp.abs(k_norm_g), axis=1)).astype(F32)

    pad = (-(batch + 1)) % 8
    cc = jnp.concatenate([c, c_ctx[None, :], jnp.zeros((pad, d), F32)], axis=0)
    mod = _modulation(cc, ada_w, ada_b)
    mod = mod.reshape(depth, cc.shape[0], 6, 1, d)

    xs = x.reshape(batch * seq, d)
    cs = ctx.reshape(batch * ctx_len, d)

    for l in range(depth):
        last = l == depth - 1
        lam_init = 0.8 - 0.6 * math.exp(-0.3 * l)
        lat = [mod[l, :batch, k] for k in range(6)]
        cm = [mod[l, batch:batch + 1, k] for k in range(6)]
        n1 = norm1_g[l].reshape(1, d)
        n2 = norm2_g[l].reshape(1, d)
        qg = jnp.tile(q_norm_g[l], PROJ_TN // B_QK_DIM).reshape(1, PROJ_TN)
        kg = jnp.tile(k_norm_g[l], PROJ_TN // B_QK_DIM).reshape(1, PROJ_TN)
        lng = sgu_ln_g[l].reshape(1, A_WIDTH)
        subg = subln_g[l].reshape(B_V_DIM, 1)

        z_lat, vt_lat = _proj(xs, lat[0], lat[1], n1, w_in_b, wv_b, qg, kg, lng, cos, sin, gmat,
                              tm=tm_lat, seq_len=seq, use_rope=True, layer=l)
        z_ctx, vt_ctx = _proj(cs, cm[0], cm[1], n1, w_in_b, wv_b, qg, kg, lng, cos_ctx, sin_ctx, gmat,
                              tm=tm_ctx, seq_len=ctx_len, use_rope=False, layer=l)

        yb_lat = _attention(z_lat, [(z_lat, vt_lat, seq), (z_ctx, vt_ctx, ctx_len)], score_cap, lam_vecs[l], subg,
                            batch=batch, seq_len=seq, tq=tq_lat, tk=tk, lam_init=lam_init, layer=l)
        merge_w = (sw_b[l], sgu_bias[l], conv_w[l], wa_b, wb_b, wc_b, wo_b)
        ffn_w = (up_b, ffn_conv_w[l], ffn_conv_b[l].reshape(1, 2 * D_FF), down_b)
        xs = _merge(z_lat, yb_lat, xs, lat[2], *merge_w, tm=tm_mrg, seq_len=seq, layer=l)
        xs = _ffn(xs, lat[3], lat[4], lat[5], n2, *ffn_w, tm=tm_lat, seq_len=seq, layer=l)

        if not last:
            yb_ctx = _attention(z_ctx, [(z_ctx, vt_ctx, ctx_len)], score_cap, lam_vecs[l], subg,
                                batch=batch, seq_len=ctx_len, tq=tm_ctx, tk=tk, lam_init=lam_init, layer=l)
            cs = _merge(z_ctx, yb_ctx, cs, cm[2], *merge_w, tm=tm_ctx, seq_len=ctx_len, layer=l)
            cs = _ffn(cs, cm[3], cm[4], cm[5], n2, *ffn_w, tm=tm_ctx, seq_len=ctx_len, layer=l)

    return xs.reshape(batch, seq, d)
```

```python
import functools
import math

import jax
import jax.numpy as jnp
import numpy as np
from jax import lax
from jax.experimental import pallas as pl
from jax.experimental.pallas import tpu as pltpu

F32 = jnp.float32
BF16 = jnp.bfloat16

D_MODEL = 1024
GRID_W = 64
A_WIDTH = 512
A_GROUPS = 4
CHUNK = 128
B_HEADS = 8
B_QK_DIM = 64
B_V_DIM = 128
B_WIDTH = B_HEADS * B_V_DIM
ROPE_THETA = 10000.0
C_WIDTH = 512
N_BRANCH = 3
D_FF = 2816
EPS = 1e-6

_REF_A_U = 0
_REF_A_V = 512
_REF_B_Q = 1024
_REF_B_K = 2048
_REF_B_V = 3072
_REF_C_IN = 4096
_REF_GATE = 5632

PROJ_TN = 512
_J_U = 0
_J_V = 1
_J_Q = 2
_J_K = 4
_J_GATE = 6
_J_CIN = 12
_J_VALT = 15
_N_Z_TILES = 15
_N_PROJ_STEPS = 17
Z_COLS = _N_Z_TILES * PROJ_TN

HALO = 16
LANES = 128
SUBLANES = 8
MXU_DIM = 256
FF_TN = 256
MERGE_ROWS = 256
QK_SCALE_LOG2E = (B_QK_DIM ** -0.5) * math.log2(math.e)
SAFE_EXP2_RANGE = 60.0
VMEM_LIMIT = 56 * 1024 * 1024
_NT = (((1,), (1,)), ((), ()))


def _z_weights(w_in):
    def head_major(w):
        lead = w.shape[:-1]
        return jnp.swapaxes(w.reshape(*lead, 2, B_HEADS, B_QK_DIM), -3, -2).reshape(*lead, 2 * B_HEADS * B_QK_DIM)

    parts = [
        w_in[..., _REF_A_U:_REF_B_Q],
        head_major(w_in[..., _REF_B_Q:_REF_B_K]),
        head_major(w_in[..., _REF_B_K:_REF_B_V]),
        w_in[..., _REF_GATE:_REF_GATE + N_BRANCH * D_MODEL],
        w_in[..., _REF_C_IN:_REF_GATE],
    ]
    w = jnp.concatenate([p.astype(BF16) for p in parts], axis=-1)
    assert w.shape[-1] == Z_COLS
    return w


def _rope_tables(n_tokens):
    rows = n_tokens // GRID_W
    row = jnp.repeat(jnp.arange(rows, dtype=F32), GRID_W)
    col = jnp.tile(jnp.arange(GRID_W, dtype=F32), rows)
    n_freq = B_QK_DIM // 4
    inv = ROPE_THETA ** (-jnp.arange(n_freq, dtype=F32) / n_freq)
    ang_r = row[:, None] * inv
    ang_c = col[:, None] * inv
    cos64 = jnp.concatenate([jnp.cos(ang_r), jnp.cos(ang_r), jnp.cos(ang_c), jnp.cos(ang_c)], axis=1)
    sin64 = jnp.concatenate([-jnp.sin(ang_r), jnp.sin(ang_r), -jnp.sin(ang_c), jnp.sin(ang_c)], axis=1)
    return jnp.tile(cos64, (1, 2)), jnp.tile(sin64, (1, 2))


def _mod_kernel(cc_ref, w_ref, b_ref, o_ref):
    cc = cc_ref[...]
    s = cc * jax.nn.sigmoid(cc)
    o_ref[...] = jnp.dot(s, w_ref[...], preferred_element_type=F32,
                         precision=lax.Precision.HIGHEST) + b_ref[...]


def _modulation(cc, ada_w, ada_b):
    depth, d, n = ada_w.shape
    tn = 1536
    rows = cc.shape[0]
    return pl.pallas_call(
        _mod_kernel,
        grid=(depth, n // tn),
        in_specs=[
            pl.BlockSpec((rows, d), lambda l, j: (0, 0)),
            pl.BlockSpec((None, d, tn), lambda l, j: (l, 0, j)),
            pl.BlockSpec((None, 1, tn), lambda l, j: (l, 0, j)),
        ],
        out_specs=pl.BlockSpec((None, rows, tn), lambda l, j: (l, 0, j)),
        out_shape=jax.ShapeDtypeStruct((depth, rows, n), F32),
        compiler_params=pltpu.CompilerParams(
            dimension_semantics=("arbitrary", "arbitrary"), vmem_limit_bytes=VMEM_LIMIT),
        name="modulation",
    )(cc, ada_w, ada_b.reshape(depth, 1, n))


def _norm_modulate(x, gain, shift, scale):
    ms = jnp.mean(x * x, axis=-1, keepdims=True)
    y = x * lax.rsqrt(ms + EPS) * gain
    return y * (1.0 + scale) + shift


def _sigmoid(x):
    return 0.5 * jnp.tanh(0.5 * x) + 0.5


def _proj_kernel(x_ref, sh_ref, sc_ref, ng_ref, w_ref, wv_ref, qg_ref, kg_ref, lng_ref,
                 cos_ref, sin_ref, gmat_ref, o_ref, vt_ref, h_ref, wvt_ref, *, tm, use_rope):
    @pl.when(pl.program_id(0) == 0)
    def _():
        for c in range(0, B_WIDTH, MXU_DIM):
            wvt_ref[c:c + MXU_DIM, :] = wv_ref[:, c:c + MXU_DIM].astype(F32).T.astype(BF16)

    h_ref[...] = _norm_modulate(x_ref[...], ng_ref[...], sh_ref[...], sc_ref[...]).astype(BF16)

    def qk_epilogue(acc, cols, gain, scale):
        sq = (acc * acc).astype(BF16)
        ms = jnp.concatenate(
            [jnp.dot(sq[:, c:c + MXU_DIM], gmat_ref[...], preferred_element_type=F32)
             for c in range(0, PROJ_TN, MXU_DIM)], axis=1)
        y = acc * lax.rsqrt(ms + EPS) * gain
        if not use_rope:
            o_ref[:, cols] = (y * scale).astype(BF16)
            return
        lane = lax.broadcasted_iota(jnp.int32, (tm, LANES), 1)
        first_half = (lane & 16) == 0
        cos = cos_ref[...]
        sin = sin_ref[...]
        for c in range(PROJ_TN // LANES):
            yc = y[:, c * LANES:(c + 1) * LANES]
            partner = jnp.where(first_half, pltpu.roll(yc, LANES - 16, 1), pltpu.roll(yc, 16, 1))
            lo = cols.start + c * LANES
            o_ref[:, lo:lo + LANES] = ((yc * cos + partner * sin) * scale).astype(BF16)

    for j in range(_N_Z_TILES):
        cols = slice(j * PROJ_TN, (j + 1) * PROJ_TN)
        acc = jnp.dot(h_ref[...], w_ref[:, cols], preferred_element_type=F32)
        if j == _J_U:
            o_ref[:, cols] = jax.nn.gelu(acc).astype(BF16)
        elif j == _J_V:
            g = jax.nn.gelu(acc)
            mu = jnp.mean(g, axis=-1, keepdims=True)
            c = g - mu
            var = jnp.mean(c * c, axis=-1, keepdims=True)
            o_ref[:, cols] = (c * lax.rsqrt(var + EPS) * lng_ref[...]).astype(BF16)
        elif j < _J_K:
            qk_epilogue(acc, cols, qg_ref[...], QK_SCALE_LOG2E)
        elif j < _J_GATE:
            qk_epilogue(acc, cols, kg_ref[...], 1.0)
        elif j < _J_CIN:
            o_ref[:, cols] = _sigmoid(acc).astype(BF16)
        else:
            o_ref[:, cols] = acc.astype(BF16)

    for r in range(B_WIDTH // PROJ_TN):
        rs = slice(r * PROJ_TN, (r + 1) * PROJ_TN)
        vt_ref[rs, :] = lax.dot_general(wvt_ref[rs, :], h_ref[...], _NT,
                                        preferred_element_type=F32).astype(BF16)


def _resident(shape, layer=None):
    if layer is None:
        return pl.BlockSpec(shape, lambda *_: (0,) * len(shape), pipeline_mode=pl.Buffered(1))
    return pl.BlockSpec((None,) + tuple(shape), lambda *_: (layer,) + (0,) * len(shape),
                        pipeline_mode=pl.Buffered(1))


def _proj(x2d, shift, scale, norm_g, w, wv, qg, kg, lng, cos, sin, gmat, *, tm, seq_len, use_rope, layer):
    rows, d = x2d.shape
    tps = seq_len // tm
    nb = shift.shape[0]
    pos_tiles = cos.shape[0] // tm

    def bidx(i):
        return (i // tps) % nb

    return pl.pallas_call(
        functools.partial(_proj_kernel, tm=tm, use_rope=use_rope),
        grid=(rows // tm,),
        in_specs=[
            pl.BlockSpec((tm, d), lambda i: (i, 0)),
            pl.BlockSpec((None, 1, d), lambda i: (bidx(i), 0, 0)),
            pl.BlockSpec((None, 1, d), lambda i: (bidx(i), 0, 0)),
            _resident((1, d)),
            _resident((d, Z_COLS), layer),
            _resident((d, B_WIDTH), layer),
            _resident((1, PROJ_TN)),
            _resident((1, PROJ_TN)),
            _resident((1, PROJ_TN)),
            pl.BlockSpec((tm, LANES), lambda i: (i % pos_tiles, 0)),
            pl.BlockSpec((tm, LANES), lambda i: (i % pos_tiles, 0)),
            _resident((MXU_DIM, MXU_DIM)),
        ],
        out_specs=[
            pl.BlockSpec((tm, Z_COLS), lambda i: (i, 0)),
            pl.BlockSpec((B_WIDTH, tm), lambda i: (0, i)),
        ],
        out_shape=[
            jax.ShapeDtypeStruct((rows, Z_COLS), BF16),
            jax.ShapeDtypeStruct((B_WIDTH, rows), BF16),
        ],
        scratch_shapes=[pltpu.VMEM((tm, d), BF16), pltpu.VMEM((B_WIDTH, d), BF16)],
        compiler_params=pltpu.CompilerParams(
            dimension_semantics=("arbitrary",), vmem_limit_bytes=VMEM_LIMIT),
        name="proj",
    )(x2d, shift, scale, norm_g, w, wv, qg, kg, lng, cos, sin, gmat)


def _attn_kernel(*refs, tq, tk, src_rows, lam_init, layer):
    n_src = len(src_rows)
    q_ref = refs[0]
    kv_refs = [(refs[1 + 2 * s], refs[2 + 2 * s]) for s in range(n_src)]
    cap_ref, lam_ref, subg_ref, o_ref, p1_ref, p2_ref = refs[1 + 2 * n_src:]

    lane_k = lax.broadcasted_iota(jnp.int32, (1, LANES), 1)
    map1 = lane_k < B_QK_DIM

    q = q_ref[...]
    zero = jnp.zeros_like(q)
    q1 = jnp.where(map1, q, zero)
    q2 = jnp.where(map1, zero, q)

    cap = cap_ref[layer]
    safe = cap < SAFE_EXP2_RANGE

    def scores_t(qm, k_ref, start, size):
        kb = k_ref[start:start + size, :]
        return lax.dot_general(kb, qm, _NT, preferred_element_type=F32)

    src_offsets = [sum(src_rows[:s]) for s in range(n_src)]

    def blocks(fn, carry):
        for (k_ref, _), rows, offset in zip(kv_refs, src_rows, src_offsets):
            size = min(tk, rows)
            for b in range(rows // size):
                carry = fn(k_ref, b * size, size, offset + b * size, carry)
        return carry

    def fold(p, op):
        acc = p[0:SUBLANES, :]
        for r in range(1, p.shape[0] // SUBLANES):
            acc = op(acc, p[r * SUBLANES:(r + 1) * SUBLANES, :])
        return acc

    def finish(shift1, shift2):
        def step(k_ref, start, size, base, carry):
            l1, l2 = carry
            p1 = jnp.exp2(scores_t(q1, k_ref, start, size) - shift1)
            p2 = jnp.exp2(scores_t(q2, k_ref, start, size) - shift2)
            p1_ref[base:base + size, :] = p1.astype(BF16)
            p2_ref[base:base + size, :] = p2.astype(BF16)
            return l1 + fold(p1, jnp.add), l2 + fold(p2, jnp.add)

        zl = jnp.zeros((SUBLANES, tq), F32)
        l1, l2 = blocks(step, (zl, zl))

        o1 = jnp.zeros((B_V_DIM, tq), F32)
        o2 = jnp.zeros((B_V_DIM, tq), F32)
        for (_, vt_ref), rows, base in zip(kv_refs, src_rows, src_offsets):
            vt = vt_ref[...]
            o1 = o1 + jnp.dot(vt, p1_ref[base:base + rows, :], preferred_element_type=F32)
            o2 = o2 + jnp.dot(vt, p2_ref[base:base + rows, :], preferred_element_type=F32)

        r1 = 1.0 / jnp.sum(l1, axis=0, keepdims=True)
        r2 = 1.0 / jnp.sum(l2, axis=0, keepdims=True)
        lv = lam_ref[...]
        lam = (jnp.exp(jnp.sum(lv[0:1] * lv[1:2], axis=1, keepdims=True))
               - jnp.exp(jnp.sum(lv[2:3] * lv[3:4], axis=1, keepdims=True)) + lam_init)
        a = o1 * r1 - o2 * (lam * r2)
        ms = jnp.mean(a * a, axis=0, keepdims=True)
        y = a * lax.rsqrt(ms + EPS) * subg_ref[...] * (1.0 - lam_init)
        o_ref[...] = y.T.astype(BF16)

    @pl.when(safe)
    def _():
        finish(cap, cap)

    @pl.when(jnp.logical_not(safe))
    def _():
        def mx_step(k_ref, start, size, base, carry):
            m1, m2 = carry
            m1 = jnp.maximum(m1, fold(scores_t(q1, k_ref, start, size), jnp.maximum))
            m2 = jnp.maximum(m2, fold(scores_t(q2, k_ref, start, size), jnp.maximum))
            return m1, m2

        neg = jnp.full((SUBLANES, tq), -jnp.inf, F32)
        m1, m2 = blocks(mx_step, (neg, neg))
        finish(jnp.max(m1, axis=0, keepdims=True), jnp.max(m2, axis=0, keepdims=True))


def _attention(zq, kv_sources, score_cap, lam_vecs, subg, *, batch, seq_len, tq, tk, lam_init, layer):
    rows = zq.shape[0]
    nq = seq_len // tq
    q_col = _J_Q * PROJ_TN // LANES
    k_col = _J_K * PROJ_TN // LANES

    in_specs = [pl.BlockSpec((tq, LANES), lambda b, h, i: (b * nq + i, q_col + h))]
    args = [zq]
    src_rows = []
    for z, vt, n in kv_sources:
        in_specs.append(pl.BlockSpec((n, LANES), lambda b, h, i: (b, k_col + h)))
        in_specs.append(pl.BlockSpec((B_V_DIM, n), lambda b, h, i: (h, b)))
        args += [z, vt]
        src_rows.append(n)
    in_specs.append(pl.BlockSpec(memory_space=pltpu.SMEM))
    in_specs.append(pl.BlockSpec(lam_vecs.shape, lambda b, h, i: (0, 0)))
    in_specs.append(pl.BlockSpec((B_V_DIM, 1), lambda b, h, i: (0, 0)))
    args += [score_cap, lam_vecs, subg]

    return pl.pallas_call(
        functools.partial(_attn_kernel, tq=tq, tk=tk, src_rows=tuple(src_rows), lam_init=lam_init,
                          layer=layer),
        grid=(batch, B_HEADS, nq),
        in_specs=in_specs,
        out_specs=pl.BlockSpec((tq, LANES), lambda b, h, i: (b * nq + i, h)),
        out_shape=jax.ShapeDtypeStruct((rows, B_WIDTH), BF16),
        scratch_shapes=[
            pltpu.VMEM((sum(src_rows), tq), BF16),
            pltpu.VMEM((sum(src_rows), tq), BF16),
        ],
        compiler_params=pltpu.CompilerParams(
            dimension_semantics=("arbitrary", "arbitrary", "arbitrary"), vmem_limit_bytes=VMEM_LIMIT),
        name="attn",
    )(*args)


def _merge_kernel(u_ref, v_ref, yb_ref, ga_ref, gb_ref, gc_ref, cb_ref, cg_ref, cx_ref,
                  pg_ref, px_ref, ng_ref, nx_ref, x_ref, g1_ref,
                  sw_ref, sb_ref, cw_ref, wa_ref, wb_ref, wc_ref, wo_ref,
                  o_ref, ya_ref, yc_ref, *, tm, tps):
    i = pl.program_id(0)
    first = (i % tps) == 0
    last = (i % tps) == (tps - 1)

    for c in range(tm // CHUNK):
        rs = slice(c * CHUNK, (c + 1) * CHUNK)
        for g in range(A_GROUPS):
            cs = slice(g * LANES, (g + 1) * LANES)
            mixed = jnp.dot(sw_ref[g], v_ref[rs, cs], preferred_element_type=F32) + sb_ref[:, cs]
            ya_ref[rs, cs] = (u_ref[rs, cs].astype(F32) * mixed).astype(BF16)

    t = cg_ref[...].astype(F32) * cx_ref[...].astype(F32)
    t_prev_row = pg_ref[HALO - 1:HALO, :].astype(F32) * px_ref[HALO - 1:HALO, :].astype(F32)
    t_next_row = ng_ref[0:1, :].astype(F32) * nx_ref[0:1, :].astype(F32)
    t_prev_row = jnp.where(first, 0.0, t_prev_row)
    t_next_row = jnp.where(last, 0.0, t_next_row)
    row = lax.broadcasted_iota(jnp.int32, (tm, C_WIDTH), 0)
    t_prev = jnp.where(row == 0, t_prev_row, pltpu.roll(t, 1, 0))
    t_next = jnp.where(row == tm - 1, t_next_row, pltpu.roll(t, tm - 1, 0))
    cw = cw_ref[...]
    y_c = cb_ref[...].astype(F32) * (t_prev * cw[0:1] + t * cw[1:2] + t_next * cw[2:3])
    yc_ref[...] = y_c.astype(BF16)

    for r in range(0, tm, MERGE_ROWS):
        rows = slice(r, r + MERGE_ROWS)
        m = ga_ref[rows, :].astype(F32) * jnp.dot(ya_ref[rows, :], wa_ref[...], preferred_element_type=F32)
        m = m + gb_ref[rows, :].astype(F32) * jnp.dot(yb_ref[rows, :], wb_ref[...], preferred_element_type=F32)
        m = m + gc_ref[rows, :].astype(F32) * jnp.dot(yc_ref[rows, :], wc_ref[...], preferred_element_type=F32)
        out = jnp.dot(m.astype(BF16), wo_ref[...], preferred_element_type=F32)
        o_ref[rows, :] = x_ref[rows, :] + g1_ref[...] * out


def _merge(z, yb, x2d, g1, sw, sb, cw, wa, wb, wc, wo, *, tm, seq_len, layer):
    rows, d = x2d.shape
    tps = seq_len // tm
    nb = g1.shape[0]
    hb = tm // HALO
    n_halo = rows // HALO
    gate_col = _J_GATE * PROJ_TN // D_MODEL

    def bidx(i):
        return (i // tps) % nb

    def prev_blk(i):
        return jnp.maximum(i * hb - 1, 0)

    def next_blk(i):
        return jnp.minimum((i + 1) * hb, n_halo - 1)

    return pl.pallas_call(
        functools.partial(_merge_kernel, tm=tm, tps=tps),
        grid=(rows // tm,),
        in_specs=[
            pl.BlockSpec((tm, A_WIDTH), lambda i: (i, _J_U)),
            pl.BlockSpec((tm, A_WIDTH), lambda i: (i, _J_V)),
            pl.BlockSpec((tm, B_WIDTH), lambda i: (i, 0)),
            pl.BlockSpec((tm, d), lambda i: (i, gate_col)),
            pl.BlockSpec((tm, d), lambda i: (i, gate_col + 1)),
            pl.BlockSpec((tm, d), lambda i: (i, gate_col + 2)),
            pl.BlockSpec((tm, C_WIDTH), lambda i: (i, _J_CIN)),
            pl.BlockSpec((tm, C_WIDTH), lambda i: (i, _J_CIN + 1)),
            pl.BlockSpec((tm, C_WIDTH), lambda i: (i, _J_CIN + 2)),
            pl.BlockSpec((HALO, C_WIDTH), lambda i: (prev_blk(i), _J_CIN + 1)),
            pl.BlockSpec((HALO, C_WIDTH), lambda i: (prev_blk(i), _J_CIN + 2)),
            pl.BlockSpec((HALO, C_WIDTH), lambda i: (next_blk(i), _J_CIN + 1)),
            pl.BlockSpec((HALO, C_WIDTH), lambda i: (next_blk(i), _J_CIN + 2)),
            pl.BlockSpec((tm, d), lambda i: (i, 0)),
            pl.BlockSpec((None, 1, d), lambda i: (bidx(i), 0, 0)),
            _resident((A_GROUPS, CHUNK, CHUNK)),
            _resident((CHUNK, A_WIDTH)),
            _resident((3, C_WIDTH)),
            _resident((A_WIDTH, d), layer),
            _resident((B_WIDTH, d), layer),
            _resident((C_WIDTH, d), layer),
            _resident((d, d), layer),
        ],
        out_specs=pl.BlockSpec((tm, d), lambda i: (i, 0)),
        out_shape=jax.ShapeDtypeStruct((rows, d), F32),
        scratch_shapes=[pltpu.VMEM((tm, A_WIDTH), BF16), pltpu.VMEM((tm, C_WIDTH), BF16)],
        compiler_params=pltpu.CompilerParams(
            dimension_semantics=("arbitrary",), vmem_limit_bytes=VMEM_LIMIT),
        name="merge",
    )(z, z, yb, z, z, z, z, z, z, z, z, z, z, x2d, g1, sw, sb, cw, wa, wb, wc, wo)


def _ffn_kernel(x_ref, xp_ref, xn_ref, sh_ref, sc_ref, gt_ref, ng_ref,
                wu_ref, cw_ref, cb_ref, wd_ref, o_ref, h_ref, u_ref, a_ref, *, tm, tps):
    i = pl.program_id(0)

    def nm(x):
        return _norm_modulate(x, ng_ref[...], sh_ref[...], sc_ref[...])

    first = (i % tps) == 0
    last = (i % tps) == (tps - 1)
    h_ref[HALO:HALO + tm, :] = nm(x_ref[...]).astype(BF16)
    h_ref[0:HALO, :] = jnp.where(first, 0.0, nm(xp_ref[...])).astype(BF16)
    h_ref[HALO + tm:, :] = jnp.where(last, 0.0, nm(xn_ref[...])).astype(BF16)

    def conv(slot, cols):
        cw = cw_ref[:, cols]
        return (u_ref[slot, HALO - 1:HALO - 1 + tm, :] * cw[0:1] + u_ref[slot, HALO:HALO + tm, :] * cw[1:2]
                + u_ref[slot, HALO + 1:HALO + 1 + tm, :] * cw[2:3] + cb_ref[:, cols])

    for c in range(D_FF // FF_TN):
        g_cols = slice(c * FF_TN, (c + 1) * FF_TN)
        v_cols = slice(D_FF + c * FF_TN, D_FF + (c + 1) * FF_TN)
        g_slot = 2 * (c % 2)
        v_slot = g_slot + 1
        h = h_ref[...]
        u_ref[g_slot] = jnp.dot(h, wu_ref[:, g_cols], preferred_element_type=F32)
        u_ref[v_slot] = jnp.dot(h, wu_ref[:, v_cols], preferred_element_type=F32)
        g = conv(g_slot, g_cols)
        val = conv(v_slot, v_cols)
        a_ref[:, g_cols] = (g * _sigmoid(g) * val).astype(BF16)

    out = jnp.dot(a_ref[...], wd_ref[...], preferred_element_type=F32)
    o_ref[...] = x_ref[...] + gt_ref[...] * out


def _ffn(x2d, shift, scale, gate, norm_g, w_up, cw, cb, w_down, *, tm, seq_len, layer):
    rows, d = x2d.shape
    tps = seq_len // tm
    nb = shift.shape[0]
    hb = tm // HALO
    n_halo = rows // HALO

    def bidx(i):
        return (i // tps) % nb

    vec = pl.BlockSpec((None, 1, d), lambda i: (bidx(i), 0, 0))
    return pl.pallas_call(
        functools.partial(_ffn_kernel, tm=tm, tps=tps),
        grid=(rows // tm,),
        in_specs=[
            pl.BlockSpec((tm, d), lambda i: (i, 0)),
            pl.BlockSpec((HALO, d), lambda i: (jnp.maximum(i * hb - 1, 0), 0)),
            pl.BlockSpec((HALO, d), lambda i: (jnp.minimum((i + 1) * hb, n_halo - 1), 0)),
            vec, vec, vec,
            _resident((1, d)),
            _resident((d, 2 * D_FF), layer),
            _resident((3, 2 * D_FF)),
            _resident((1, 2 * D_FF)),
            _resident((D_FF, d), layer),
        ],
        out_specs=pl.BlockSpec((tm, d), lambda i: (i, 0)),
        out_shape=jax.ShapeDtypeStruct((rows, d), F32),
        scratch_shapes=[
            pltpu.VMEM((tm + 2 * HALO, d), BF16),
            pltpu.VMEM((4, tm + 2 * HALO, FF_TN), F32),
            pltpu.VMEM((tm, D_FF), BF16),
        ],
        compiler_params=pltpu.CompilerParams(
            dimension_semantics=("arbitrary",), vmem_limit_bytes=VMEM_LIMIT),
        name="ffn",
    )(x2d, x2d, x2d, shift, scale, gate, norm_g, w_up, cw, cb, w_down)


def kernel(x, c, ctx, c_ctx, ada_w, ada_b, norm1_g, norm2_g, w_in, sgu_ln_g, sgu_w, sgu_b, q_norm_g, k_norm_g, lam_q1, lam_k1, lam_q2, lam_k2, subln_g, conv_w, w_br_a, w_br_b, w_br_c, w_out, ffn_up, ffn_conv_w, ffn_conv_b, ffn_down):
    batch, seq, d = x.shape
    ctx_len = ctx.shape[1]
    depth = ada_w.shape[0]
    assert d == D_MODEL and seq % GRID_W == 0

    tm_lat = min(512, seq)
    tm_mrg = min(512, seq)
    tm_ctx = ctx_len
    tq_lat = min(1024, seq)
    tk = 512
    assert seq % tm_lat == 0 and seq % tq_lat == 0 and ctx_len % HALO == 0

    w_in_b = _z_weights(w_in)
    wv_b = w_in[:, :, _REF_B_V:_REF_B_V + B_WIDTH].astype(BF16)
    wa_b, wb_b, wc_b, wo_b = (w.astype(BF16) for w in (w_br_a, w_br_b, w_br_c, w_out))
    up_b = ffn_up.astype(BF16)
    down_b = ffn_down.astype(BF16)
    sw_b = sgu_w.astype(BF16)
    sgu_bias = jnp.repeat(jnp.swapaxes(sgu_b, 1, 2), LANES, axis=2)
    gmat = jnp.asarray(np.kron(np.eye(MXU_DIM // B_QK_DIM), np.full((B_QK_DIM, B_QK_DIM), 1.0 / B_QK_DIM)), BF16)
    cos, sin = _rope_tables(seq)
    cos_ctx = jnp.ones((tm_ctx, LANES), F32)
    sin_ctx = jnp.zeros((tm_ctx, LANES), F32)
    lam_vecs = jnp.stack([lam_q1, lam_k1, lam_q2, lam_k2], axis=1).astype(F32)
    score_cap = (1.02 * B_QK_DIM * QK_SCALE_LOG2E
                 * jnp.max(jnp.abs(q_norm_g), axis=1) * jnp.max(jnp.abs(k_norm_g), axis=1)).astype(F32)

    pad = (-(batch + 1)) % 8
    cc = jnp.concatenate([c, c_ctx[None, :], jnp.zeros((pad, d), F32)], axis=0)
    mod = _modulation(cc, ada_w, ada_b)
    mod = mod.reshape(depth, cc.shape[0], 6, 1, d)

    xs = x.reshape(batch * seq, d)
    cs = ctx.reshape(batch * ctx_len, d)

    for l in range(depth):
        last = l == depth - 1
        lam_init = 0.8 - 0.6 * math.exp(-0.3 * l)
        lat = [mod[l, :batch, k] for k in range(6)]
        cm = [mod[l, batch:batch + 1, k] for k in range(6)]
        n1 = norm1_g[l].reshape(1, d)
        n2 = norm2_g[l].reshape(1, d)
        qg = jnp.tile(q_norm_g[l], PROJ_TN // B_QK_DIM).reshape(1, PROJ_TN)
        kg = jnp.tile(k_norm_g[l], PROJ_TN // B_QK_DIM).reshape(1, PROJ_TN)
        lng = sgu_ln_g[l].reshape(1, A_WIDTH)
        subg = subln_g[l].reshape(B_V_DIM, 1)

        z_lat, vt_lat = _proj(xs, lat[0], lat[1], n1, w_in_b, wv_b, qg, kg, lng, cos, sin, gmat,
                              tm=tm_lat, seq_len=seq, use_rope=True, layer=l)
        z_ctx, vt_ctx = _proj(cs, cm[0], cm[1], n1, w_in_b, wv_b, qg, kg, lng, cos_ctx, sin_ctx, gmat,
                              tm=tm_ctx, seq_len=ctx_len, use_rope=False, layer=l)

        yb_lat = _attention(z_lat, [(z_lat, vt_lat, seq), (z_ctx, vt_ctx, ctx_len)], score_cap, lam_vecs[l], subg,
                            batch=batch, seq_len=seq, tq=tq_lat, tk=tk, lam_init=lam_init, layer=l)
        merge_w = (sw_b[l], sgu_bias[l], conv_w[l], wa_b, wb_b, wc_b, wo_b)
        ffn_w = (up_b, ffn_conv_w[l], ffn_conv_b[l].reshape(1, 2 * D_FF), down_b)
        xs = _merge(z_lat, yb_lat, xs, lat[2], *merge_w, tm=tm_mrg, seq_len=seq, layer=l)
        xs = _ffn(xs, lat[3], lat[4], lat[5], n2, *ffn_w, tm=tm_lat, seq_len=seq, layer=l)

        if not last:
            yb_ctx = _attention(z_ctx, [(z_ctx, vt_ctx, ctx_len)], score_cap, lam_vecs[l], subg,
                                batch=batch, seq_len=ctx_len, tq=tm_ctx, tk=tk, lam_init=lam_init, layer=l)
            cs = _merge(z_ctx, yb_ctx, cs, cm[2], *merge_w, tm=tm_ctx, seq_len=ctx_len, layer=l)
            cs = _ffn(cs, cm[3], cm[4], cm[5], n2, *ffn_w, tm=tm_ctx, seq_len=ctx_len, layer=l)

    return xs.reshape(batch, seq, d)
```

```python
import functools
import math

import jax
import jax.numpy as jnp
import numpy as np
from jax import lax
from jax.experimental import pallas as pl
from jax.experimental.pallas import tpu as pltpu

F32 = jnp.float32
BF16 = jnp.bfloat16

D_MODEL = 1024
GRID_W = 64
A_WIDTH = 512
A_GROUPS = 4
CHUNK = 128
B_HEADS = 8
B_QK_DIM = 64
B_V_DIM = 128
B_WIDTH = B_HEADS * B_V_DIM
ROPE_THETA = 10000.0
C_WIDTH = 512
N_BRANCH = 3
D_FF = 2816
EPS = 1e-6

_REF_A_U = 0
_REF_A_V = 512
_REF_B_Q = 1024
_REF_B_K = 2048
_REF_B_V = 3072
_REF_C_IN = 4096
_REF_GATE = 5632

PROJ_TN = 512
_J_U = 0
_J_V = 1
_J_Q = 2
_J_K = 4
_J_GATE = 6
_J_CIN = 12
_J_VALT = 15
_N_Z_TILES = 15
_N_PROJ_STEPS = 17
Z_COLS = _N_Z_TILES * PROJ_TN

HALO = 16
LANES = 128
SUBLANES = 8
MXU_DIM = 256
FF_TN = 256
MERGE_ROWS = 256
QK_SCALE_LOG2E = (B_QK_DIM ** -0.5) * math.log2(math.e)
SAFE_EXP2_RANGE = 60.0
VMEM_LIMIT = 56 * 1024 * 1024
_NT = (((1,), (1,)), ((), ()))


def _z_weights(w_in):
    def head_major(w):
        lead = w.shape[:-1]
        return jnp.swapaxes(w.reshape(*lead, 2, B_HEADS, B_QK_DIM), -3, -2).reshape(*lead, 2 * B_HEADS * B_QK_DIM)

    parts = [
        w_in[..., _REF_A_U:_REF_B_Q],
        head_major(w_in[..., _REF_B_Q:_REF_B_K]),
        head_major(w_in[..., _REF_B_K:_REF_B_V]),
        w_in[..., _REF_GATE:_REF_GATE + N_BRANCH * D_MODEL],
        w_in[..., _REF_C_IN:_REF_GATE],
    ]
    w = jnp.concatenate([p.astype(BF16) for p in parts], axis=-1)
    assert w.shape[-1] == Z_COLS
    return w


def _rope_tables(n_tokens):
    rows = n_tokens // GRID_W
    row = jnp.repeat(jnp.arange(rows, dtype=F32), GRID_W)
    col = jnp.tile(jnp.arange(GRID_W, dtype=F32), rows)
    n_freq = B_QK_DIM // 4
    inv = ROPE_THETA ** (-jnp.arange(n_freq, dtype=F32) / n_freq)
    ang_r = row[:, None] * inv
    ang_c = col[:, None] * inv
    cos64 = jnp.concatenate([jnp.cos(ang_r), jnp.cos(ang_r), jnp.cos(ang_c), jnp.cos(ang_c)], axis=1)
    sin64 = jnp.concatenate([-jnp.sin(ang_r), jnp.sin(ang_r), -jnp.sin(ang_c), jnp.sin(ang_c)], axis=1)
    return jnp.tile(cos64, (1, 2)), jnp.tile(sin64, (1, 2))


def _mod_kernel(cc_ref, w_ref, b_ref, o_ref):
    cc = cc_ref[...]
    s = cc * jax.nn.sigmoid(cc)
    o_ref[...] = jnp.dot(s, w_ref[...], preferred_element_type=F32,
                         precision=lax.Precision.HIGHEST) + b_ref[...]


def _modulation(cc, ada_w, ada_b):
    depth, d, n = ada_w.shape
    tn = 1536
    rows = cc.shape[0]
    return pl.pallas_call(
        _mod_kernel,
        grid=(depth, n // tn),
        in_specs=[
            pl.BlockSpec((rows, d), lambda l, j: (0, 0)),
            pl.BlockSpec((None, d, tn), lambda l, j: (l, 0, j)),
            pl.BlockSpec((None, 1, tn), lambda l, j: (l, 0, j)),
        ],
        out_specs=pl.BlockSpec((None, rows, tn), lambda l, j: (l, 0, j)),
        out_shape=jax.ShapeDtypeStruct((depth, rows, n), F32),
        compiler_params=pltpu.CompilerParams(
            dimension_semantics=("arbitrary", "arbitrary"), vmem_limit_bytes=VMEM_LIMIT),
        name="modulation",
    )(cc, ada_w, ada_b.reshape(depth, 1, n))


def _norm_modulate(x, gain, shift, scale):
    ms = jnp.mean(x * x, axis=-1, keepdims=True)
    y = x * lax.rsqrt(ms + EPS) * gain
    return y * (1.0 + scale) + shift


def _sigmoid(x):
    return 0.5 * jnp.tanh(0.5 * x) + 0.5


def _proj_kernel(x_ref, sh_ref, sc_ref, ng_ref, w_ref, wv_ref, qg_ref, kg_ref, lng_ref,
                 cos_ref, sin_ref, gmat_ref, o_ref, vt_ref, h_ref, wvt_ref, *, tm, use_rope):
    @pl.when(pl.program_id(0) == 0)
    def _():
        for c in range(0, B_WIDTH, MXU_DIM):
            wvt_ref[c:c + MXU_DIM, :] = wv_ref[:, c:c + MXU_DIM].astype(F32).T.astype(BF16)

    h_ref[...] = _norm_modulate(x_ref[...], ng_ref[...], sh_ref[...], sc_ref[...]).astype(BF16)

    def qk_epilogue(acc, cols, gain, scale):
        sq = (acc * acc).astype(BF16)
        ms = jnp.concatenate(
            [jnp.dot(sq[:, c:c + MXU_DIM], gmat_ref[...], preferred_element_type=F32)
             for c in range(0, PROJ_TN, MXU_DIM)], axis=1)
        y = acc * lax.rsqrt(ms + EPS) * gain
        if not use_rope:
            o_ref[:, cols] = (y * scale).astype(BF16)
            return
        lane = lax.broadcasted_iota(jnp.int32, (tm, LANES), 1)
        first_half = (lane & 16) == 0
        cos = cos_ref[...]
        sin = sin_ref[...]
        for c in range(PROJ_TN // LANES):
            yc = y[:, c * LANES:(c + 1) * LANES]
            partner = jnp.where(first_half, pltpu.roll(yc, LANES - 16, 1), pltpu.roll(yc, 16, 1))
            lo = cols.start + c * LANES
            o_ref[:, lo:lo + LANES] = ((yc * cos + partner * sin) * scale).astype(BF16)

    for j in range(_N_Z_TILES):
        cols = slice(j * PROJ_TN, (j + 1) * PROJ_TN)
        acc = jnp.dot(h_ref[...], w_ref[:, cols], preferred_element_type=F32)
        if j == _J_U:
            o_ref[:, cols] = jax.nn.gelu(acc).astype(BF16)
        elif j == _J_V:
            g = jax.nn.gelu(acc)
            mu = jnp.mean(g, axis=-1, keepdims=True)
            c = g - mu
            var = jnp.mean(c * c, axis=-1, keepdims=True)
            o_ref[:, cols] = (c * lax.rsqrt(var + EPS) * lng_ref[...]).astype(BF16)
        elif j < _J_K:
            qk_epilogue(acc, cols, qg_ref[...], QK_SCALE_LOG2E)
        elif j < _J_GATE:
            qk_epilogue(acc, cols, kg_ref[...], 1.0)
        elif j < _J_CIN:
            o_ref[:, cols] = _sigmoid(acc).astype(BF16)
        else:
            o_ref[:, cols] = acc.astype(BF16)

    for r in range(B_WIDTH // PROJ_TN):
        rs = slice(r * PROJ_TN, (r + 1) * PROJ_TN)
        vt_ref[rs, :] = lax.dot_general(wvt_ref[rs, :], h_ref[...], _NT,
                                        preferred_element_type=F32).astype(BF16)


def _resident(shape, layer=None):
    if layer is None:
        return pl.BlockSpec(shape, lambda *_: (0,) * len(shape), pipeline_mode=pl.Buffered(1))
    return pl.BlockSpec((None,) + tuple(shape), lambda *_: (layer,) + (0,) * len(shape),
                        pipeline_mode=pl.Buffered(1))


def _proj(x2d, shift, scale, norm_g, w, wv, qg, kg, lng, cos, sin, gmat, *, tm, seq_len, use_rope, layer):
    rows, d = x2d.shape
    tps = seq_len // tm
    nb = shift.shape[0]
    pos_tiles = cos.shape[0] // tm

    def bidx(i):
        return (i // tps) % nb

    return pl.pallas_call(
        functools.partial(_proj_kernel, tm=tm, use_rope=use_rope),
        grid=(rows // tm,),
        in_specs=[
            pl.BlockSpec((tm, d), lambda i: (i, 0)),
            pl.BlockSpec((None, 1, d), lambda i: (bidx(i), 0, 0)),
            pl.BlockSpec((None, 1, d), lambda i: (bidx(i), 0, 0)),
            _resident((1, d)),
            _resident((d, Z_COLS), layer),
            _resident((d, B_WIDTH), layer),
            _resident((1, PROJ_TN)),
            _resident((1, PROJ_TN)),
            _resident((1, PROJ_TN)),
            pl.BlockSpec((tm, LANES), lambda i: (i % pos_tiles, 0)),
            pl.BlockSpec((tm, LANES), lambda i: (i % pos_tiles, 0)),
            _resident((MXU_DIM, MXU_DIM)),
        ],
        out_specs=[
            pl.BlockSpec((tm, Z_COLS), lambda i: (i, 0)),
            pl.BlockSpec((B_WIDTH, tm), lambda i: (0, i)),
        ],
        out_shape=[
            jax.ShapeDtypeStruct((rows, Z_COLS), BF16),
            jax.ShapeDtypeStruct((B_WIDTH, rows), BF16),
        ],
        scratch_shapes=[pltpu.VMEM((tm, d), BF16), pltpu.VMEM((B_WIDTH, d), BF16)],
        compiler_params=pltpu.CompilerParams(
            dimension_semantics=("arbitrary",), vmem_limit_bytes=VMEM_LIMIT),
        name="proj",
    )(x2d, shift, scale, norm_g, w, wv, qg, kg, lng, cos, sin, gmat)


def _attn_kernel(*refs, tq, tk, src_rows, lam_init, layer):
    n_src = len(src_rows)
    q_ref = refs[0]
    kv_refs = [(refs[1 + 2 * s], refs[2 + 2 * s]) for s in range(n_src)]
    cap_ref, lam_ref, subg_ref, o_ref, p1_ref, p2_ref = refs[1 + 2 * n_src:]

    lane_k = lax.broadcasted_iota(jnp.int32, (1, LANES), 1)
    map1 = lane_k < B_QK_DIM

    q = q_ref[...]
    zero = jnp.zeros_like(q)
    q1 = jnp.where(map1, q, zero)
    q2 = jnp.where(map1, zero, q)

    cap = cap_ref[layer]
    safe = cap < SAFE_EXP2_RANGE

    def scores_t(qm, k_ref, start, size):
        kb = k_ref[start:start + size, :]
        return lax.dot_general(kb, qm, _NT, preferred_element_type=F32)

    src_offsets = [sum(src_rows[:s]) for s in range(n_src)]

    def blocks(fn, carry):
        for (k_ref, _), rows, offset in zip(kv_refs, src_rows, src_offsets):
            size = min(tk, rows)
            for b in range(rows // size):
                carry = fn(k_ref, b * size, size, offset + b * size, carry)
        return carry

    def fold(p, op):
        acc = p[0:SUBLANES, :]
        for r in range(1, p.shape[0] // SUBLANES):
            acc = op(acc, p[r * SUBLANES:(r + 1) * SUBLANES, :])
        return acc

    def finish(shift1, shift2):
        def step(k_ref, start, size, base, carry):
            l1, l2 = carry
            p1 = jnp.exp2(scores_t(q1, k_ref, start, size) - shift1)
            p2 = jnp.exp2(scores_t(q2, k_ref, start, size) - shift2)
            p1_ref[base:base + size, :] = p1.astype(BF16)
            p2_ref[base:base + size, :] = p2.astype(BF16)
            return l1 + fold(p1, jnp.add), l2 + fold(p2, jnp.add)

        zl = jnp.zeros((SUBLANES, tq), F32)
        l1, l2 = blocks(step, (zl, zl))

        lv = lam_ref[...]
        lam = (jnp.exp(jnp.sum(lv[0:1] * lv[1:2], axis=1, keepdims=True))
               - jnp.exp(jnp.sum(lv[2:3] * lv[3:4], axis=1, keepdims=True)) + lam_init)
        r1 = (1.0 / jnp.sum(l1, axis=0, keepdims=True)).astype(BF16)
        r2 = (lam / jnp.sum(l2, axis=0, keepdims=True)).astype(BF16)

        a = jnp.zeros((B_V_DIM, tq), F32)
        for (_, vt_ref), rows, base in zip(kv_refs, src_rows, src_offsets):
            size = min(tk, rows)
            for b in range(base, base + rows, size):
                p1_ref[b:b + size, :] = p1_ref[b:b + size, :] * r1 - p2_ref[b:b + size, :] * r2
            a = a + jnp.dot(vt_ref[...], p1_ref[base:base + rows, :], preferred_element_type=F32)

        ms = jnp.mean(a * a, axis=0, keepdims=True)
        y = a * lax.rsqrt(ms + EPS) * subg_ref[...] * (1.0 - lam_init)
        o_ref[...] = y.T.astype(BF16)

    @pl.when(safe)
    def _():
        finish(cap, cap)

    @pl.when(jnp.logical_not(safe))
    def _():
        def mx_step(k_ref, start, size, base, carry):
            m1, m2 = carry
            m1 = jnp.maximum(m1, fold(scores_t(q1, k_ref, start, size), jnp.maximum))
            m2 = jnp.maximum(m2, fold(scores_t(q2, k_ref, start, size), jnp.maximum))
            return m1, m2

        neg = jnp.full((SUBLANES, tq), -jnp.inf, F32)
        m1, m2 = blocks(mx_step, (neg, neg))
        finish(jnp.max(m1, axis=0, keepdims=True), jnp.max(m2, axis=0, keepdims=True))


def _attention(zq, kv_sources, score_cap, lam_vecs, subg, *, batch, seq_len, tq, tk, lam_init, layer):
    rows = zq.shape[0]
    nq = seq_len // tq
    q_col = _J_Q * PROJ_TN // LANES
    k_col = _J_K * PROJ_TN // LANES

    in_specs = [pl.BlockSpec((tq, LANES), lambda b, h, i: (b * nq + i, q_col + h))]
    args = [zq]
    src_rows = []
    for z, vt, n in kv_sources:
        in_specs.append(pl.BlockSpec((n, LANES), lambda b, h, i: (b, k_col + h)))
        in_specs.append(pl.BlockSpec((B_V_DIM, n), lambda b, h, i: (h, b)))
        args += [z, vt]
        src_rows.append(n)
    in_specs.append(pl.BlockSpec(memory_space=pltpu.SMEM))
    in_specs.append(pl.BlockSpec(lam_vecs.shape, lambda b, h, i: (0, 0)))
    in_specs.append(pl.BlockSpec((B_V_DIM, 1), lambda b, h, i: (0, 0)))
    args += [score_cap, lam_vecs, subg]

    return pl.pallas_call(
        functools.partial(_attn_kernel, tq=tq, tk=tk, src_rows=tuple(src_rows), lam_init=lam_init,
                          layer=layer),
        grid=(batch, B_HEADS, nq),
        in_specs=in_specs,
        out_specs=pl.BlockSpec((tq, LANES), lambda b, h, i: (b * nq + i, h)),
        out_shape=jax.ShapeDtypeStruct((rows, B_WIDTH), BF16),
        scratch_shapes=[
            pltpu.VMEM((sum(src_rows), tq), BF16),
            pltpu.VMEM((sum(src_rows), tq), BF16),
        ],
        compiler_params=pltpu.CompilerParams(
            dimension_semantics=("arbitrary", "arbitrary", "arbitrary"), vmem_limit_bytes=VMEM_LIMIT),
        name="attn",
    )(*args)


def _merge_kernel(u_ref, v_ref, yb_ref, ga_ref, gb_ref, gc_ref, cb_ref, cg_ref, cx_ref,
                  pg_ref, px_ref, ng_ref, nx_ref, x_ref, g1_ref,
                  sw_ref, sb_ref, cw_ref, wa_ref, wb_ref, wc_ref, wo_ref,
                  o_ref, ya_ref, yc_ref, *, tm, tps):
    i = pl.program_id(0)
    first = (i % tps) == 0
    last = (i % tps) == (tps - 1)

    for c in range(tm // CHUNK):
        rs = slice(c * CHUNK, (c + 1) * CHUNK)
        for g in range(A_GROUPS):
            cs = slice(g * LANES, (g + 1) * LANES)
            mixed = jnp.dot(sw_ref[g], v_ref[rs, cs], preferred_element_type=F32) + sb_ref[:, cs]
            ya_ref[rs, cs] = (u_ref[rs, cs].astype(F32) * mixed).astype(BF16)

    t = cg_ref[...].astype(F32) * cx_ref[...].astype(F32)
    t_prev_row = pg_ref[HALO - 1:HALO, :].astype(F32) * px_ref[HALO - 1:HALO, :].astype(F32)
    t_next_row = ng_ref[0:1, :].astype(F32) * nx_ref[0:1, :].astype(F32)
    t_prev_row = jnp.where(first, 0.0, t_prev_row)
    t_next_row = jnp.where(last, 0.0, t_next_row)
    row = lax.broadcasted_iota(jnp.int32, (tm, C_WIDTH), 0)
    t_prev = jnp.where(row == 0, t_prev_row, pltpu.roll(t, 1, 0))
    t_next = jnp.where(row == tm - 1, t_next_row, pltpu.roll(t, tm - 1, 0))
    cw = cw_ref[...]
    y_c = cb_ref[...].astype(F32) * (t_prev * cw[0:1] + t * cw[1:2] + t_next * cw[2:3])
    yc_ref[...] = y_c.astype(BF16)

    for r in range(0, tm, MERGE_ROWS):
        rows = slice(r, r + MERGE_ROWS)
        m = ga_ref[rows, :].astype(F32) * jnp.dot(ya_ref[rows, :], wa_ref[...], preferred_element_type=F32)
        m = m + gb_ref[rows, :].astype(F32) * jnp.dot(yb_ref[rows, :], wb_ref[...], preferred_element_type=F32)
        m = m + gc_ref[rows, :].astype(F32) * jnp.dot(yc_ref[rows, :], wc_ref[...], preferred_element_type=F32)
        out = jnp.dot(m.astype(BF16), wo_ref[...], preferred_element_type=F32)
        o_ref[rows, :] = x_ref[rows, :] + g1_ref[...] * out


def _merge(z, yb, x2d, g1, sw, sb, cw, wa, wb, wc, wo, *, tm, seq_len, layer):
    rows, d = x2d.shape
    tps = seq_len // tm
    nb = g1.shape[0]
    hb = tm // HALO
    n_halo = rows // HALO
    gate_col = _J_GATE * PROJ_TN // D_MODEL

    def bidx(i):
        return (i // tps) % nb

    def prev_blk(i):
        return jnp.maximum(i * hb - 1, 0)

    def next_blk(i):
        return jnp.minimum((i + 1) * hb, n_halo - 1)

    return pl.pallas_call(
        functools.partial(_merge_kernel, tm=tm, tps=tps),
        grid=(rows // tm,),
        in_specs=[
            pl.BlockSpec((tm, A_WIDTH), lambda i: (i, _J_U)),
            pl.BlockSpec((tm, A_WIDTH), lambda i: (i, _J_V)),
            pl.BlockSpec((tm, B_WIDTH), lambda i: (i, 0)),
            pl.BlockSpec((tm, d), lambda i: (i, gate_col)),
            pl.BlockSpec((tm, d), lambda i: (i, gate_col + 1)),
            pl.BlockSpec((tm, d), lambda i: (i, gate_col + 2)),
            pl.BlockSpec((tm, C_WIDTH), lambda i: (i, _J_CIN)),
            pl.BlockSpec((tm, C_WIDTH), lambda i: (i, _J_CIN + 1)),
            pl.BlockSpec((tm, C_WIDTH), lambda i: (i, _J_CIN + 2)),
            pl.BlockSpec((HALO, C_WIDTH), lambda i: (prev_blk(i), _J_CIN + 1)),
            pl.BlockSpec((HALO, C_WIDTH), lambda i: (prev_blk(i), _J_CIN + 2)),
            pl.BlockSpec((HALO, C_WIDTH), lambda i: (next_blk(i), _J_CIN + 1)),
            pl.BlockSpec((HALO, C_WIDTH), lambda i: (next_blk(i), _J_CIN + 2)),
            pl.BlockSpec((tm, d), lambda i: (i, 0)),
            pl.BlockSpec((None, 1, d), lambda i: (bidx(i), 0, 0)),
            _resident((A_GROUPS, CHUNK, CHUNK)),
            _resident((CHUNK, A_WIDTH)),
            _resident((3, C_WIDTH)),
            _resident((A_WIDTH, d), layer),
            _resident((B_WIDTH, d), layer),
            _resident((C_WIDTH, d), layer),
            _resident((d, d), layer),
        ],
        out_specs=pl.BlockSpec((tm, d), lambda i: (i, 0)),
        out_shape=jax.ShapeDtypeStruct((rows, d), F32),
        scratch_shapes=[pltpu.VMEM((tm, A_WIDTH), BF16), pltpu.VMEM((tm, C_WIDTH), BF16)],
        compiler_params=pltpu.CompilerParams(
            dimension_semantics=("arbitrary",), vmem_limit_bytes=VMEM_LIMIT),
        name="merge",
    )(z, z, yb, z, z, z, z, z, z, z, z, z, z, x2d, g1, sw, sb, cw, wa, wb, wc, wo)


def _ffn_kernel(x_ref, xp_ref, xn_ref, sh_ref, sc_ref, gt_ref, ng_ref,
                wu_ref, cw_ref, cb_ref, wd_ref, o_ref, h_ref, u_ref, a_ref, *, tm, tps):
    i = pl.program_id(0)

    def nm(x):
        return _norm_modulate(x, ng_ref[...], sh_ref[...], sc_ref[...])

    first = (i % tps) == 0
    last = (i % tps) == (tps - 1)
    h_ref[HALO:HALO + tm, :] = nm(x_ref[...]).astype(BF16)
    h_ref[0:HALO, :] = jnp.where(first, 0.0, nm(xp_ref[...])).astype(BF16)
    h_ref[HALO + tm:, :] = jnp.where(last, 0.0, nm(xn_ref[...])).astype(BF16)

    def conv(slot, cols):
        cw = cw_ref[:, cols]
        return (u_ref[slot, HALO - 1:HALO - 1 + tm, :] * cw[0:1] + u_ref[slot, HALO:HALO + tm, :] * cw[1:2]
                + u_ref[slot, HALO + 1:HALO + 1 + tm, :] * cw[2:3] + cb_ref[:, cols])

    for c in range(D_FF // FF_TN):
        g_cols = slice(c * FF_TN, (c + 1) * FF_TN)
        v_cols = slice(D_FF + c * FF_TN, D_FF + (c + 1) * FF_TN)
        g_slot = 2 * (c % 2)
        v_slot = g_slot + 1
        h = h_ref[...]
        u_ref[g_slot] = jnp.dot(h, wu_ref[:, g_cols], preferred_element_type=F32)
        u_ref[v_slot] = jnp.dot(h, wu_ref[:, v_cols], preferred_element_type=F32)
        g = conv(g_slot, g_cols)
        val = conv(v_slot, v_cols)
        a_ref[:, g_cols] = (g * _sigmoid(g) * val).astype(BF16)

    out = jnp.dot(a_ref[...], wd_ref[...], preferred_element_type=F32)
    o_ref[...] = x_ref[...] + gt_ref[...] * out


def _ffn(x2d, shift, scale, gate, norm_g, w_up, cw, cb, w_down, *, tm, seq_len, layer):
    rows, d = x2d.shape
    tps = seq_len // tm
    nb = shift.shape[0]
    hb = tm // HALO
    n_halo = rows // HALO

    def bidx(i):
        return (i // tps) % nb

    vec = pl.BlockSpec((None, 1, d), lambda i: (bidx(i), 0, 0))
    return pl.pallas_call(
        functools.partial(_ffn_kernel, tm=tm, tps=tps),
        grid=(rows // tm,),
        in_specs=[
            pl.BlockSpec((tm, d), lambda i: (i, 0)),
            pl.BlockSpec((HALO, d), lambda i: (jnp.maximum(i * hb - 1, 0), 0)),
            pl.BlockSpec((HALO, d), lambda i: (jnp.minimum((i + 1) * hb, n_halo - 1), 0)),
            vec, vec, vec,
            _resident((1, d)),
            _resident((d, 2 * D_FF), layer),
            _resident((3, 2 * D_FF)),
            _resident((1, 2 * D_FF)),
            _resident((D_FF, d), layer),
        ],
        out_specs=pl.BlockSpec((tm, d), lambda i: (i, 0)),
        out_shape=jax.ShapeDtypeStruct((rows, d), F32),
        scratch_shapes=[
            pltpu.VMEM((tm + 2 * HALO, d), BF16),
            pltpu.VMEM((4, tm + 2 * HALO, FF_TN), F32),
            pltpu.VMEM((tm, D_FF), BF16),
        ],
        compiler_params=pltpu.CompilerParams(
            dimension_semantics=("arbitrary",), vmem_limit_bytes=VMEM_LIMIT),
        name="ffn",
    )(x2d, x2d, x2d, shift, scale, gate, norm_g, w_up, cw, cb, w_down)


def kernel(x, c, ctx, c_ctx, ada_w, ada_b, norm1_g, norm2_g, w_in, sgu_ln_g, sgu_w, sgu_b, q_norm_g, k_norm_g, lam_q1, lam_k1, lam_q2, lam_k2, subln_g, conv_w, w_br_a, w_br_b, w_br_c, w_out, ffn_up, ffn_conv_w, ffn_conv_b, ffn_down):
    batch, seq, d = x.shape
    ctx_len = ctx.shape[1]
    depth = ada_w.shape[0]
    assert d == D_MODEL and seq % GRID_W == 0

    tm_lat = min(512, seq)
    tm_mrg = min(512, seq)
    tm_ctx = ctx_len
    tq_lat = min(1024, seq)
    tk = 512
    assert seq % tm_lat == 0 and seq % tq_lat == 0 and ctx_len % HALO == 0

    w_in_b = _z_weights(w_in)
    wv_b = w_in[:, :, _REF_B_V:_REF_B_V + B_WIDTH].astype(BF16)
    wa_b, wb_b, wc_b, wo_b = (w.astype(BF16) for w in (w_br_a, w_br_b, w_br_c, w_out))
    up_b = ffn_up.astype(BF16)
    down_b = ffn_down.astype(BF16)
    sw_b = sgu_w.astype(BF16)
    sgu_bias = jnp.repeat(jnp.swapaxes(sgu_b, 1, 2), LANES, axis=2)
    gmat = jnp.asarray(np.kron(np.eye(MXU_DIM // B_QK_DIM), np.full((B_QK_DIM, B_QK_DIM), 1.0 / B_QK_DIM)), BF16)
    cos, sin = _rope_tables(seq)
    cos_ctx = jnp.ones((tm_ctx, LANES), F32)
    sin_ctx = jnp.zeros((tm_ctx, LANES), F32)
    lam_vecs = jnp.stack([lam_q1, lam_k1, lam_q2, lam_k2], axis=1).astype(F32)
    score_cap = (1.02 * B_QK_DIM * QK_SCALE_LOG2E
                 * jnp.max(jnp.abs(q_norm_g), axis=1) * jnp.max(jnp.abs(k_norm_g), axis=1)).astype(F32)

    pad = (-(batch + 1)) % 8
    cc = jnp.concatenate([c, c_ctx[None, :], jnp.zeros((pad, d), F32)], axis=0)
    mod = _modulation(cc, ada_w, ada_b)
    mod = mod.reshape(depth, cc.shape[0], 6, 1, d)

    xs = x.reshape(batch * seq, d)
    cs = ctx.reshape(batch * ctx_len, d)

    for l in range(depth):
        last = l == depth - 1
        lam_init = 0.8 - 0.6 * math.exp(-0.3 * l)
        lat = [mod[l, :batch, k] for k in range(6)]
        cm = [mod[l, batch:batch + 1, k] for k in range(6)]
        n1 = norm1_g[l].reshape(1, d)
        n2 = norm2_g[l].reshape(1, d)
        qg = jnp.tile(q_norm_g[l], PROJ_TN // B_QK_DIM).reshape(1, PROJ_TN)
        kg = jnp.tile(k_norm_g[l], PROJ_TN // B_QK_DIM).reshape(1, PROJ_TN)
        lng = sgu_ln_g[l].reshape(1, A_WIDTH)
        subg = subln_g[l].reshape(B_V_DIM, 1)

        z_lat, vt_lat = _proj(xs, lat[0], lat[1], n1, w_in_b, wv_b, qg, kg, lng, cos, sin, gmat,
                              tm=tm_lat, seq_len=seq, use_rope=True, layer=l)
        z_ctx, vt_ctx = _proj(cs, cm[0], cm[1], n1, w_in_b, wv_b, qg, kg, lng, cos_ctx, sin_ctx, gmat,
                              tm=tm_ctx, seq_len=ctx_len, use_rope=False, layer=l)

        yb_lat = _attention(z_lat, [(z_lat, vt_lat, seq), (z_ctx, vt_ctx, ctx_len)], score_cap, lam_vecs[l], subg,
                            batch=batch, seq_len=seq, tq=tq_lat, tk=tk, lam_init=lam_init, layer=l)
        merge_w = (sw_b[l], sgu_bias[l], conv_w[l], wa_b, wb_b, wc_b, wo_b)
        ffn_w = (up_b, ffn_conv_w[l], ffn_conv_b[l].reshape(1, 2 * D_FF), down_b)
        xs = _merge(z_lat, yb_lat, xs, lat[2], *merge_w, tm=tm_mrg, seq_len=seq, layer=l)
        xs = _ffn(xs, lat[3], lat[4], lat[5], n2, *ffn_w, tm=tm_lat, seq_len=seq, layer=l)

        if not last:
            yb_ctx = _attention(z_ctx, [(z_ctx, vt_ctx, ctx_len)], score_cap, lam_vecs[l], subg,
                                batch=batch, seq_len=ctx_len, tq=tm_ctx, tk=tk, lam_init=lam_init, layer=l)
            cs = _merge(z_ctx, yb_ctx, cs, cm[2], *merge_w, tm=tm_ctx, seq_len=ctx_len, layer=l)
            cs = _ffn(cs, cm[3], cm[4], cm[5], n2, *ffn_w, tm=tm_ctx, seq_len=ctx_len, layer=l)

    return xs.reshape(batch, seq, d)
```

```python
import functools
import math

import jax
import jax.numpy as jnp
import numpy as np
from jax import lax
from jax.experimental import pallas as pl
from jax.experimental.pallas import tpu as pltpu

F32 = jnp.float32
BF16 = jnp.bfloat16

D_MODEL = 1024
GRID_W = 64
A_WIDTH = 512
A_GROUPS = 4
CHUNK = 128
B_HEADS = 8
B_QK_DIM = 64
B_V_DIM = 128
B_WIDTH = B_HEADS * B_V_DIM
ROPE_THETA = 10000.0
C_WIDTH = 512
N_BRANCH = 3
D_FF = 2816
EPS = 1e-6

_REF_A_U = 0
_REF_A_V = 512
_REF_B_Q = 1024
_REF_B_K = 2048
_REF_B_V = 3072
_REF_C_IN = 4096
_REF_GATE = 5632

PROJ_TN = 512
_J_U = 0
_J_V = 1
_J_Q = 2
_J_K = 4
_J_GATE = 6
_J_CIN = 12
_J_VALT = 15
_N_Z_TILES = 15
_N_PROJ_STEPS = 17
Z_COLS = _N_Z_TILES * PROJ_TN

HALO = 16
LANES = 128
SUBLANES = 8
MXU_DIM = 256
FF_TN = 256
MERGE_ROWS = 256
QK_SCALE_LOG2E = (B_QK_DIM ** -0.5) * math.log2(math.e)
SAFE_EXP2_RANGE = 60.0
VMEM_LIMIT = 56 * 1024 * 1024
_NT = (((1,), (1,)), ((), ()))


def _z_weights(w_in):
    def head_major(w):
        lead = w.shape[:-1]
        return jnp.swapaxes(w.reshape(*lead, 2, B_HEADS, B_QK_DIM), -3, -2).reshape(*lead, 2 * B_HEADS * B_QK_DIM)

    parts = [
        w_in[..., _REF_A_U:_REF_B_Q],
        head_major(w_in[..., _REF_B_Q:_REF_B_K]),
        head_major(w_in[..., _REF_B_K:_REF_B_V]),
        w_in[..., _REF_GATE:_REF_GATE + N_BRANCH * D_MODEL],
        w_in[..., _REF_C_IN:_REF_GATE],
    ]
    w = jnp.concatenate([p.astype(BF16) for p in parts], axis=-1)
    assert w.shape[-1] == Z_COLS
    return w


def _rope_tables(n_tokens):
    rows = n_tokens // GRID_W
    row = jnp.repeat(jnp.arange(rows, dtype=F32), GRID_W)
    col = jnp.tile(jnp.arange(GRID_W, dtype=F32), rows)
    n_freq = B_QK_DIM // 4
    inv = ROPE_THETA ** (-jnp.arange(n_freq, dtype=F32) / n_freq)
    ang_r = row[:, None] * inv
    ang_c = col[:, None] * inv
    cos64 = jnp.concatenate([jnp.cos(ang_r), jnp.cos(ang_r), jnp.cos(ang_c), jnp.cos(ang_c)], axis=1)
    sin64 = jnp.concatenate([-jnp.sin(ang_r), jnp.sin(ang_r), -jnp.sin(ang_c), jnp.sin(ang_c)], axis=1)
    return jnp.tile(cos64, (1, 2)), jnp.tile(sin64, (1, 2))


def _mod_kernel(cc_ref, w_ref, b_ref, o_ref):
    cc = cc_ref[...]
    s = cc * jax.nn.sigmoid(cc)
    o_ref[...] = jnp.dot(s, w_ref[...], preferred_element_type=F32,
                         precision=lax.Precision.HIGHEST) + b_ref[...]


def _modulation(cc, ada_w, ada_b):
    depth, d, n = ada_w.shape
    tn = 1536
    rows = cc.shape[0]
    return pl.pallas_call(
        _mod_kernel,
        grid=(depth, n // tn),
        in_specs=[
            pl.BlockSpec((rows, d), lambda l, j: (0, 0)),
            pl.BlockSpec((None, d, tn), lambda l, j: (l, 0, j)),
            pl.BlockSpec((None, 1, tn), lambda l, j: (l, 0, j)),
        ],
        out_specs=pl.BlockSpec((None, rows, tn), lambda l, j: (l, 0, j)),
        out_shape=jax.ShapeDtypeStruct((depth, rows, n), F32),
        compiler_params=pltpu.CompilerParams(
            dimension_semantics=("arbitrary", "arbitrary"), vmem_limit_bytes=VMEM_LIMIT),
        name="modulation",
    )(cc, ada_w, ada_b.reshape(depth, 1, n))


def _norm_modulate(x, gain, shift, scale):
    ms = jnp.mean(x * x, axis=-1, keepdims=True)
    y = x * lax.rsqrt(ms + EPS) * gain
    return y * (1.0 + scale) + shift


def _sigmoid(x):
    return 0.5 * jnp.tanh(0.5 * x) + 0.5


def _proj_kernel(x_ref, sh_ref, sc_ref, ng_ref, w_ref, wv_ref, qg_ref, kg_ref, lng_ref,
                 cos_ref, sin_ref, gmat_ref, o_ref, vt_ref, h_ref, wvt_ref, *, tm, use_rope):
    @pl.when(pl.program_id(0) == 0)
    def _():
        for c in range(0, B_WIDTH, MXU_DIM):
            wvt_ref[c:c + MXU_DIM, :] = wv_ref[:, c:c + MXU_DIM].astype(F32).T.astype(BF16)

    h_ref[...] = _norm_modulate(x_ref[...], ng_ref[...], sh_ref[...], sc_ref[...]).astype(BF16)

    def qk_epilogue(acc, cols, gain, scale):
        sq = (acc * acc).astype(BF16)
        ms = jnp.concatenate(
            [jnp.dot(sq[:, c:c + MXU_DIM], gmat_ref[...], preferred_element_type=F32)
             for c in range(0, PROJ_TN, MXU_DIM)], axis=1)
        y = acc * lax.rsqrt(ms + EPS) * gain
        if not use_rope:
            o_ref[:, cols] = (y * scale).astype(BF16)
            return
        lane = lax.broadcasted_iota(jnp.int32, (tm, LANES), 1)
        first_half = (lane & 16) == 0
        cos = cos_ref[...]
        sin = sin_ref[...]
        for c in range(PROJ_TN // LANES):
            yc = y[:, c * LANES:(c + 1) * LANES]
            partner = jnp.where(first_half, pltpu.roll(yc, LANES - 16, 1), pltpu.roll(yc, 16, 1))
            lo = cols.start + c * LANES
            o_ref[:, lo:lo + LANES] = ((yc * cos + partner * sin) * scale).astype(BF16)

    for j in range(_N_Z_TILES):
        cols = slice(j * PROJ_TN, (j + 1) * PROJ_TN)
        acc = jnp.dot(h_ref[...], w_ref[:, cols], preferred_element_type=F32)
        if j == _J_U:
            o_ref[:, cols] = jax.nn.gelu(acc).astype(BF16)
        elif j == _J_V:
            g = jax.nn.gelu(acc)
            mu = jnp.mean(g, axis=-1, keepdims=True)
            c = g - mu
            var = jnp.mean(c * c, axis=-1, keepdims=True)
            o_ref[:, cols] = (c * lax.rsqrt(var + EPS) * lng_ref[...]).astype(BF16)
        elif j < _J_K:
            qk_epilogue(acc, cols, qg_ref[...], QK_SCALE_LOG2E)
        elif j < _J_GATE:
            qk_epilogue(acc, cols, kg_ref[...], 1.0)
        elif j < _J_CIN:
            o_ref[:, cols] = _sigmoid(acc).astype(BF16)
        else:
            o_ref[:, cols] = acc.astype(BF16)

    for r in range(B_WIDTH // PROJ_TN):
        rs = slice(r * PROJ_TN, (r + 1) * PROJ_TN)
        vt_ref[rs, :] = lax.dot_general(wvt_ref[rs, :], h_ref[...], _NT,
                                        preferred_element_type=F32).astype(BF16)


def _resident(shape, layer=None):
    if layer is None:
        return pl.BlockSpec(shape, lambda *_: (0,) * len(shape), pipeline_mode=pl.Buffered(1))
    return pl.BlockSpec((None,) + tuple(shape), lambda *_: (layer,) + (0,) * len(shape),
                        pipeline_mode=pl.Buffered(1))


def _proj(x2d, shift, scale, norm_g, w, wv, qg, kg, lng, cos, sin, gmat, *, tm, seq_len, use_rope, layer):
    rows, d = x2d.shape
    tps = seq_len // tm
    nb = shift.shape[0]
    pos_tiles = cos.shape[0] // tm

    def bidx(i):
        return (i // tps) % nb

    return pl.pallas_call(
        functools.partial(_proj_kernel, tm=tm, use_rope=use_rope),
        grid=(rows // tm,),
        in_specs=[
            pl.BlockSpec((tm, d), lambda i: (i, 0)),
            pl.BlockSpec((None, 1, d), lambda i: (bidx(i), 0, 0)),
            pl.BlockSpec((None, 1, d), lambda i: (bidx(i), 0, 0)),
            _resident((1, d)),
            _resident((d, Z_COLS), layer),
            _resident((d, B_WIDTH), layer),
            _resident((1, PROJ_TN)),
            _resident((1, PROJ_TN)),
            _resident((1, PROJ_TN)),
            pl.BlockSpec((tm, LANES), lambda i: (i % pos_tiles, 0)),
            pl.BlockSpec((tm, LANES), lambda i: (i % pos_tiles, 0)),
            _resident((MXU_DIM, MXU_DIM)),
        ],
        out_specs=[
            pl.BlockSpec((tm, Z_COLS), lambda i: (i, 0)),
            pl.BlockSpec((B_WIDTH, tm), lambda i: (0, i)),
        ],
        out_shape=[
            jax.ShapeDtypeStruct((rows, Z_COLS), BF16),
            jax.ShapeDtypeStruct((B_WIDTH, rows), BF16),
        ],
        scratch_shapes=[pltpu.VMEM((tm, d), BF16), pltpu.VMEM((B_WIDTH, d), BF16)],
        compiler_params=pltpu.CompilerParams(
            dimension_semantics=("arbitrary",), vmem_limit_bytes=VMEM_LIMIT),
        name="proj",
    )(x2d, shift, scale, norm_g, w, wv, qg, kg, lng, cos, sin, gmat)


def _attn_kernel(*refs, tq, tk, src_rows, lam_init, layer):
    n_src = len(src_rows)
    q_ref = refs[0]
    kv_refs = [(refs[1 + 2 * s], refs[2 + 2 * s]) for s in range(n_src)]
    cap_ref, lam_ref, subg_ref, o_ref, p1_ref, p2_ref = refs[1 + 2 * n_src:]

    lane_k = lax.broadcasted_iota(jnp.int32, (1, LANES), 1)
    map1 = lane_k < B_QK_DIM

    q = q_ref[...]
    zero = jnp.zeros_like(q)
    q1 = jnp.where(map1, q, zero)
    q2 = jnp.where(map1, zero, q)

    cap = cap_ref[layer]
    safe = cap < SAFE_EXP2_RANGE

    def scores_t(qm, k_ref, start, size):
        kb = k_ref[start:start + size, :]
        return lax.dot_general(kb, qm, _NT, preferred_element_type=F32)

    src_offsets = [sum(src_rows[:s]) for s in range(n_src)]

    def blocks(fn, carry):
        for (k_ref, _), rows, offset in zip(kv_refs, src_rows, src_offsets):
            size = min(tk, rows)
            for b in range(rows // size):
                carry = fn(k_ref, b * size, size, offset + b * size, carry)
        return carry

    def fold(p, op):
        acc = p[0:SUBLANES, :]
        for r in range(1, p.shape[0] // SUBLANES):
            acc = op(acc, p[r * SUBLANES:(r + 1) * SUBLANES, :])
        return acc

    def finish(shift1, shift2):
        def step(k_ref, start, size, base, carry):
            l1, l2 = carry
            p1 = jnp.exp2(scores_t(q1, k_ref, start, size) - shift1)
            p2 = jnp.exp2(scores_t(q2, k_ref, start, size) - shift2)
            p1_ref[base:base + size, :] = p1.astype(BF16)
            p2_ref[base:base + size, :] = p2.astype(BF16)
            return l1 + fold(p1, jnp.add), l2 + fold(p2, jnp.add)

        zl = jnp.zeros((SUBLANES, tq), F32)
        l1, l2 = blocks(step, (zl, zl))

        lv = lam_ref[...]
        lam = (jnp.exp(jnp.sum(lv[0:1] * lv[1:2], axis=1, keepdims=True))
               - jnp.exp(jnp.sum(lv[2:3] * lv[3:4], axis=1, keepdims=True)) + lam_init)
        r1 = (1.0 / jnp.sum(l1, axis=0, keepdims=True)).astype(BF16)
        r2 = (lam / jnp.sum(l2, axis=0, keepdims=True)).astype(BF16)

        parts = []
        for qs in range(0, tq, MXU_DIM):
            acc = jnp.zeros((B_V_DIM, MXU_DIM), F32)
            r1q = r1[:, qs:qs + MXU_DIM]
            r2q = r2[:, qs:qs + MXU_DIM]
            for (_, vt_ref), rows, base in zip(kv_refs, src_rows, src_offsets):
                for ks in range(0, rows, MXU_DIM):
                    keys = slice(base + ks, base + ks + MXU_DIM)
                    a_tile = p1_ref[keys, qs:qs + MXU_DIM] * r1q - p2_ref[keys, qs:qs + MXU_DIM] * r2q
                    acc = acc + jnp.dot(vt_ref[:, ks:ks + MXU_DIM], a_tile, preferred_element_type=F32)
            parts.append(acc)
        a = jnp.concatenate(parts, axis=1)

        ms = jnp.mean(a * a, axis=0, keepdims=True)
        y = a * lax.rsqrt(ms + EPS) * subg_ref[...] * (1.0 - lam_init)
        o_ref[...] = y.T.astype(BF16)

    @pl.when(safe)
    def _():
        finish(cap, cap)

    @pl.when(jnp.logical_not(safe))
    def _():
        def mx_step(k_ref, start, size, base, carry):
            m1, m2 = carry
            m1 = jnp.maximum(m1, fold(scores_t(q1, k_ref, start, size), jnp.maximum))
            m2 = jnp.maximum(m2, fold(scores_t(q2, k_ref, start, size), jnp.maximum))
            return m1, m2

        neg = jnp.full((SUBLANES, tq), -jnp.inf, F32)
        m1, m2 = blocks(mx_step, (neg, neg))
        finish(jnp.max(m1, axis=0, keepdims=True), jnp.max(m2, axis=0, keepdims=True))


def _attention(zq, kv_sources, score_cap, lam_vecs, subg, *, batch, seq_len, tq, tk, lam_init, layer):
    rows = zq.shape[0]
    nq = seq_len // tq
    q_col = _J_Q * PROJ_TN // LANES
    k_col = _J_K * PROJ_TN // LANES

    in_specs = [pl.BlockSpec((tq, LANES), lambda b, h, i: (b * nq + i, q_col + h))]
    args = [zq]
    src_rows = []
    for z, vt, n in kv_sources:
        in_specs.append(pl.BlockSpec((n, LANES), lambda b, h, i: (b, k_col + h)))
        in_specs.append(pl.BlockSpec((B_V_DIM, n), lambda b, h, i: (h, b)))
        args += [z, vt]
        src_rows.append(n)
    in_specs.append(pl.BlockSpec(memory_space=pltpu.SMEM))
    in_specs.append(pl.BlockSpec(lam_vecs.shape, lambda b, h, i: (0, 0)))
    in_specs.append(pl.BlockSpec((B_V_DIM, 1), lambda b, h, i: (0, 0)))
    args += [score_cap, lam_vecs, subg]

    return pl.pallas_call(
        functools.partial(_attn_kernel, tq=tq, tk=tk, src_rows=tuple(src_rows), lam_init=lam_init,
                          layer=layer),
        grid=(batch, B_HEADS, nq),
        in_specs=in_specs,
        out_specs=pl.BlockSpec((tq, LANES), lambda b, h, i: (b * nq + i, h)),
        out_shape=jax.ShapeDtypeStruct((rows, B_WIDTH), BF16),
        scratch_shapes=[
            pltpu.VMEM((sum(src_rows), tq), BF16),
            pltpu.VMEM((sum(src_rows), tq), BF16),
        ],
        compiler_params=pltpu.CompilerParams(
            dimension_semantics=("arbitrary", "arbitrary", "arbitrary"), vmem_limit_bytes=VMEM_LIMIT),
        name="attn",
    )(*args)


def _merge_kernel(u_ref, v_ref, yb_ref, ga_ref, gb_ref, gc_ref, cb_ref, cg_ref, cx_ref,
                  pg_ref, px_ref, ng_ref, nx_ref, x_ref, g1_ref,
                  sw_ref, sb_ref, cw_ref, wa_ref, wb_ref, wc_ref, wo_ref,
                  o_ref, ya_ref, yc_ref, *, tm, tps):
    i = pl.program_id(0)
    first = (i % tps) == 0
    last = (i % tps) == (tps - 1)

    for c in range(tm // CHUNK):
        rs = slice(c * CHUNK, (c + 1) * CHUNK)
        for g in range(A_GROUPS):
            cs = slice(g * LANES, (g + 1) * LANES)
            mixed = jnp.dot(sw_ref[g], v_ref[rs, cs], preferred_element_type=F32) + sb_ref[:, cs]
            ya_ref[rs, cs] = (u_ref[rs, cs].astype(F32) * mixed).astype(BF16)

    t = cg_ref[...].astype(F32) * cx_ref[...].astype(F32)
    t_prev_row = pg_ref[HALO - 1:HALO, :].astype(F32) * px_ref[HALO - 1:HALO, :].astype(F32)
    t_next_row = ng_ref[0:1, :].astype(F32) * nx_ref[0:1, :].astype(F32)
    t_prev_row = jnp.where(first, 0.0, t_prev_row)
    t_next_row = jnp.where(last, 0.0, t_next_row)
    row = lax.broadcasted_iota(jnp.int32, (tm, C_WIDTH), 0)
    t_prev = jnp.where(row == 0, t_prev_row, pltpu.roll(t, 1, 0))
    t_next = jnp.where(row == tm - 1, t_next_row, pltpu.roll(t, tm - 1, 0))
    cw = cw_ref[...]
    y_c = cb_ref[...].astype(F32) * (t_prev * cw[0:1] + t * cw[1:2] + t_next * cw[2:3])
    yc_ref[...] = y_c.astype(BF16)

    for r in range(0, tm, MERGE_ROWS):
        rows = slice(r, r + MERGE_ROWS)
        m = ga_ref[rows, :].astype(F32) * jnp.dot(ya_ref[rows, :], wa_ref[...], preferred_element_type=F32)
        m = m + gb_ref[rows, :].astype(F32) * jnp.dot(yb_ref[rows, :], wb_ref[...], preferred_element_type=F32)
        m = m + gc_ref[rows, :].astype(F32) * jnp.dot(yc_ref[rows, :], wc_ref[...], preferred_element_type=F32)
        out = jnp.dot(m.astype(BF16), wo_ref[...], preferred_element_type=F32)
        o_ref[rows, :] = x_ref[rows, :] + g1_ref[...] * out


def _merge(z, yb, x2d, g1, sw, sb, cw, wa, wb, wc, wo, *, tm, seq_len, layer):
    rows, d = x2d.shape
    tps = seq_len // tm
    nb = g1.shape[0]
    hb = tm // HALO
    n_halo = rows // HALO
    gate_col = _J_GATE * PROJ_TN // D_MODEL

    def bidx(i):
        return (i // tps) % nb

    def prev_blk(i):
        return jnp.maximum(i * hb - 1, 0)

    def next_blk(i):
        return jnp.minimum((i + 1) * hb, n_halo - 1)

    return pl.pallas_call(
        functools.partial(_merge_kernel, tm=tm, tps=tps),
        grid=(rows // tm,),
        in_specs=[
            pl.BlockSpec((tm, A_WIDTH), lambda i: (i, _J_U)),
            pl.BlockSpec((tm, A_WIDTH), lambda i: (i, _J_V)),
            pl.BlockSpec((tm, B_WIDTH), lambda i: (i, 0)),
            pl.BlockSpec((tm, d), lambda i: (i, gate_col)),
            pl.BlockSpec((tm, d), lambda i: (i, gate_col + 1)),
            pl.BlockSpec((tm, d), lambda i: (i, gate_col + 2)),
            pl.BlockSpec((tm, C_WIDTH), lambda i: (i, _J_CIN)),
            pl.BlockSpec((tm, C_WIDTH), lambda i: (i, _J_CIN + 1)),
            pl.BlockSpec((tm, C_WIDTH), lambda i: (i, _J_CIN + 2)),
            pl.BlockSpec((HALO, C_WIDTH), lambda i: (prev_blk(i), _J_CIN + 1)),
            pl.BlockSpec((HALO, C_WIDTH), lambda i: (prev_blk(i), _J_CIN + 2)),
            pl.BlockSpec((HALO, C_WIDTH), lambda i: (next_blk(i), _J_CIN + 1)),
            pl.BlockSpec((HALO, C_WIDTH), lambda i: (next_blk(i), _J_CIN + 2)),
            pl.BlockSpec((tm, d), lambda i: (i, 0)),
            pl.BlockSpec((None, 1, d), lambda i: (bidx(i), 0, 0)),
            _resident((A_GROUPS, CHUNK, CHUNK)),
            _resident((CHUNK, A_WIDTH)),
            _resident((3, C_WIDTH)),
            _resident((A_WIDTH, d), layer),
            _resident((B_WIDTH, d), layer),
            _resident((C_WIDTH, d), layer),
            _resident((d, d), layer),
        ],
        out_specs=pl.BlockSpec((tm, d), lambda i: (i, 0)),
        out_shape=jax.ShapeDtypeStruct((rows, d), F32),
        scratch_shapes=[pltpu.VMEM((tm, A_WIDTH), BF16), pltpu.VMEM((tm, C_WIDTH), BF16)],
        compiler_params=pltpu.CompilerParams(
            dimension_semantics=("arbitrary",), vmem_limit_bytes=VMEM_LIMIT),
        name="merge",
    )(z, z, yb, z, z, z, z, z, z, z, z, z, z, x2d, g1, sw, sb, cw, wa, wb, wc, wo)


def _ffn_kernel(x_ref, xp_ref, xn_ref, sh_ref, sc_ref, gt_ref, ng_ref,
                wu_ref, cw_ref, cb_ref, wd_ref, o_ref, h_ref, u_ref, a_ref, *, tm, tps):
    i = pl.program_id(0)

    def nm(x):
        return _norm_modulate(x, ng_ref[...], sh_ref[...], sc_ref[...])

    first = (i % tps) == 0
    last = (i % tps) == (tps - 1)
    h_ref[HALO:HALO + tm, :] = nm(x_ref[...]).astype(BF16)
    h_ref[0:HALO, :] = jnp.where(first, 0.0, nm(xp_ref[...])).astype(BF16)
    h_ref[HALO + tm:, :] = jnp.where(last, 0.0, nm(xn_ref[...])).astype(BF16)

    def conv(slot, cols):
        cw = cw_ref[:, cols]
        return (u_ref[slot, HALO - 1:HALO - 1 + tm, :] * cw[0:1] + u_ref[slot, HALO:HALO + tm, :] * cw[1:2]
                + u_ref[slot, HALO + 1:HALO + 1 + tm, :] * cw[2:3] + cb_ref[:, cols])

    for c in range(D_FF // FF_TN):
        g_cols = slice(c * FF_TN, (c + 1) * FF_TN)
        v_cols = slice(D_FF + c * FF_TN, D_FF + (c + 1) * FF_TN)
        g_slot = 2 * (c % 2)
        v_slot = g_slot + 1
        h = h_ref[...]
        u_ref[g_slot] = jnp.dot(h, wu_ref[:, g_cols], preferred_element_type=F32)
        u_ref[v_slot] = jnp.dot(h, wu_ref[:, v_cols], preferred_element_type=F32)
        g = conv(g_slot, g_cols)
        val = conv(v_slot, v_cols)
        a_ref[:, g_cols] = (g * _sigmoid(g) * val).astype(BF16)

    out = jnp.dot(a_ref[...], wd_ref[...], preferred_element_type=F32)
    o_ref[...] = x_ref[...] + gt_ref[...] * out


def _ffn(x2d, shift, scale, gate, norm_g, w_up, cw, cb, w_down, *, tm, seq_len, layer):
    rows, d = x2d.shape
    tps = seq_len // tm
    nb = shift.shape[0]
    hb = tm // HALO
    n_halo = rows // HALO

    def bidx(i):
        return (i // tps) % nb

    vec = pl.BlockSpec((None, 1, d), lambda i: (bidx(i), 0, 0))
    return pl.pallas_call(
        functools.partial(_ffn_kernel, tm=tm, tps=tps),
        grid=(rows // tm,),
        in_specs=[
            pl.BlockSpec((tm, d), lambda i: (i, 0)),
            pl.BlockSpec((HALO, d), lambda i: (jnp.maximum(i * hb - 1, 0), 0)),
            pl.BlockSpec((HALO, d), lambda i: (jnp.minimum((i + 1) * hb, n_halo - 1), 0)),
            vec, vec, vec,
            _resident((1, d)),
            _resident((d, 2 * D_FF), layer),
            _resident((3, 2 * D_FF)),
            _resident((1, 2 * D_FF)),
            _resident((D_FF, d), layer),
        ],
        out_specs=pl.BlockSpec((tm, d), lambda i: (i, 0)),
        out_shape=jax.ShapeDtypeStruct((rows, d), F32),
        scratch_shapes=[
            pltpu.VMEM((tm + 2 * HALO, d), BF16),
            pltpu.VMEM((4, tm + 2 * HALO, FF_TN), F32),
            pltpu.VMEM((tm, D_FF), BF16),
        ],
        compiler_params=pltpu.CompilerParams(
            dimension_semantics=("arbitrary",), vmem_limit_bytes=VMEM_LIMIT),
        name="ffn",
    )(x2d, x2d, x2d, shift, scale, gate, norm_g, w_up, cw, cb, w_down)


def kernel(x, c, ctx, c_ctx, ada_w, ada_b, norm1_g, norm2_g, w_in, sgu_ln_g, sgu_w, sgu_b, q_norm_g, k_norm_g, lam_q1, lam_k1, lam_q2, lam_k2, subln_g, conv_w, w_br_a, w_br_b, w_br_c, w_out, ffn_up, ffn_conv_w, ffn_conv_b, ffn_down):
    batch, seq, d = x.shape
    ctx_len = ctx.shape[1]
    depth = ada_w.shape[0]
    assert d == D_MODEL and seq % GRID_W == 0

    tm_lat = min(512, seq)
    tm_mrg = min(512, seq)
    tm_ctx = ctx_len
    tq_lat = min(1024, seq)
    tk = 512
    assert seq % tm_lat == 0 and seq % tq_lat == 0 and ctx_len % HALO == 0

    w_in_b = _z_weights(w_in)
    wv_b = w_in[:, :, _REF_B_V:_REF_B_V + B_WIDTH].astype(BF16)
    wa_b, wb_b, wc_b, wo_b = (w.astype(BF16) for w in (w_br_a, w_br_b, w_br_c, w_out))
    up_b = ffn_up.astype(BF16)
    down_b = ffn_down.astype(BF16)
    sw_b = sgu_w.astype(BF16)
    sgu_bias = jnp.repeat(jnp.swapaxes(sgu_b, 1, 2), LANES, axis=2)
    gmat = jnp.asarray(np.kron(np.eye(MXU_DIM // B_QK_DIM), np.full((B_QK_DIM, B_QK_DIM), 1.0 / B_QK_DIM)), BF16)
    cos, sin = _rope_tables(seq)
    cos_ctx = jnp.ones((tm_ctx, LANES), F32)
    sin_ctx = jnp.zeros((tm_ctx, LANES), F32)
    lam_vecs = jnp.stack([lam_q1, lam_k1, lam_q2, lam_k2], axis=1).astype(F32)
    score_cap = (1.02 * B_QK_DIM * QK_SCALE_LOG2E
                 * jnp.max(jnp.abs(q_norm_g), axis=1) * jnp.max(jnp.abs(k_norm_g), axis=1)).astype(F32)

    pad = (-(batch + 1)) % 8
    cc = jnp.concatenate([c, c_ctx[None, :], jnp.zeros((pad, d), F32)], axis=0)
    mod = _modulation(cc, ada_w, ada_b)
    mod = mod.reshape(depth, cc.shape[0], 6, 1, d)

    xs = x.reshape(batch * seq, d)
    cs = ctx.reshape(batch * ctx_len, d)

    for l in range(depth):
        last = l == depth - 1
        lam_init = 0.8 - 0.6 * math.exp(-0.3 * l)
        lat = [mod[l, :batch, k] for k in range(6)]
        cm = [mod[l, batch:batch + 1, k] for k in range(6)]
        n1 = norm1_g[l].reshape(1, d)
        n2 = norm2_g[l].reshape(1, d)
        qg = jnp.tile(q_norm_g[l], PROJ_TN // B_QK_DIM).reshape(1, PROJ_TN)
        kg = jnp.tile(k_norm_g[l], PROJ_TN // B_QK_DIM).reshape(1, PROJ_TN)
        lng = sgu_ln_g[l].reshape(1, A_WIDTH)
        subg = subln_g[l].reshape(B_V_DIM, 1)

        z_lat, vt_lat = _proj(xs, lat[0], lat[1], n1, w_in_b, wv_b, qg, kg, lng, cos, sin, gmat,
                              tm=tm_lat, seq_len=seq, use_rope=True, layer=l)
        z_ctx, vt_ctx = _proj(cs, cm[0], cm[1], n1, w_in_b, wv_b, qg, kg, lng, cos_ctx, sin_ctx, gmat,
                              tm=tm_ctx, seq_len=ctx_len, use_rope=False, layer=l)

        yb_lat = _attention(z_lat, [(z_lat, vt_lat, seq), (z_ctx, vt_ctx, ctx_len)], score_cap, lam_vecs[l], subg,
                            batch=batch, seq_len=seq, tq=tq_lat, tk=tk, lam_init=lam_init, layer=l)
        merge_w = (sw_b[l], sgu_bias[l], conv_w[l], wa_b, wb_b, wc_b, wo_b)
        ffn_w = (up_b, ffn_conv_w[l], ffn_conv_b[l].reshape(1, 2 * D_FF), down_b)
        xs = _merge(z_lat, yb_lat, xs, lat[2], *merge_w, tm=tm_mrg, seq_len=seq, layer=l)
        xs = _ffn(xs, lat[3], lat[4], lat[5], n2, *ffn_w, tm=tm_lat, seq_len=seq, layer=l)

        if not last:
            yb_ctx = _attention(z_ctx, [(z_ctx, vt_ctx, ctx_len)], score_cap, lam_vecs[l], subg,
                                batch=batch, seq_len=ctx_len, tq=tm_ctx, tk=tk, lam_init=lam_init, layer=l)
            cs = _merge(z_ctx, yb_ctx, cs, cm[2], *merge_w, tm=tm_ctx, seq_len=ctx_len, layer=l)
            cs = _ffn(cs, cm[3], cm[4], cm[5], n2, *ffn_w, tm=tm_ctx, seq_len=ctx_len, layer=l)

    return xs.reshape(batch, seq, d)
```

```python
import functools
import math

import jax
import jax.numpy as jnp
import numpy as np
from jax import lax
from jax.experimental import pallas as pl
from jax.experimental.pallas import tpu as pltpu

F32 = jnp.float32
BF16 = jnp.bfloat16

D_MODEL = 1024
GRID_W = 64
A_WIDTH = 512
A_GROUPS = 4
CHUNK = 128
B_HEADS = 8
B_QK_DIM = 64
B_V_DIM = 128
B_WIDTH = B_HEADS * B_V_DIM
ROPE_THETA = 10000.0
C_WIDTH = 512
N_BRANCH = 3
D_FF = 2816
EPS = 1e-6

_REF_A_U = 0
_REF_A_V = 512
_REF_B_Q = 1024
_REF_B_K = 2048
_REF_B_V = 3072
_REF_C_IN = 4096
_REF_GATE = 5632

PROJ_TN = 512
_J_U = 0
_J_V = 1
_J_Q = 2
_J_K = 4
_J_GATE = 6
_J_CIN = 12
_J_VALT = 15
_N_Z_TILES = 15
_N_PROJ_STEPS = 17
Z_COLS = _N_Z_TILES * PROJ_TN

HALO = 16
LANES = 128
SUBLANES = 8
MXU_DIM = 256
FF_TN = 256
MERGE_ROWS = 256
QK_SCALE_LOG2E = (B_QK_DIM ** -0.5) * math.log2(math.e)
SAFE_EXP2_RANGE = 60.0
VMEM_LIMIT = 56 * 1024 * 1024
_NT = (((1,), (1,)), ((), ()))


def _z_weights(w_in):
    def head_major(w):
        lead = w.shape[:-1]
        return jnp.swapaxes(w.reshape(*lead, 2, B_HEADS, B_QK_DIM), -3, -2).reshape(*lead, 2 * B_HEADS * B_QK_DIM)

    parts = [
        w_in[..., _REF_A_U:_REF_B_Q],
        head_major(w_in[..., _REF_B_Q:_REF_B_K]),
        head_major(w_in[..., _REF_B_K:_REF_B_V]),
        w_in[..., _REF_GATE:_REF_GATE + N_BRANCH * D_MODEL],
        w_in[..., _REF_C_IN:_REF_GATE],
    ]
    w = jnp.concatenate([p.astype(BF16) for p in parts], axis=-1)
    assert w.shape[-1] == Z_COLS
    return w


def _rope_tables(n_tokens):
    rows = n_tokens // GRID_W
    row = jnp.repeat(jnp.arange(rows, dtype=F32), GRID_W)
    col = jnp.tile(jnp.arange(GRID_W, dtype=F32), rows)
    n_freq = B_QK_DIM // 4
    inv = ROPE_THETA ** (-jnp.arange(n_freq, dtype=F32) / n_freq)
    ang_r = row[:, None] * inv
    ang_c = col[:, None] * inv
    cos64 = jnp.concatenate([jnp.cos(ang_r), jnp.cos(ang_r), jnp.cos(ang_c), jnp.cos(ang_c)], axis=1)
    sin64 = jnp.concatenate([-jnp.sin(ang_r), jnp.sin(ang_r), -jnp.sin(ang_c), jnp.sin(ang_c)], axis=1)
    return jnp.tile(cos64, (1, 2)), jnp.tile(sin64, (1, 2))


def _mod_kernel(cc_ref, w_ref, b_ref, o_ref):
    cc = cc_ref[...]
    s = cc * jax.nn.sigmoid(cc)
    o_ref[...] = jnp.dot(s, w_ref[...], preferred_element_type=F32,
                         precision=lax.Precision.HIGHEST) + b_ref[...]


def _modulation(cc, ada_w, ada_b):
    depth, d, n = ada_w.shape
    tn = 1536
    rows = cc.shape[0]
    return pl.pallas_call(
        _mod_kernel,
        grid=(depth, n // tn),
        in_specs=[
            pl.BlockSpec((rows, d), lambda l, j: (0, 0)),
            pl.BlockSpec((None, d, tn), lambda l, j: (l, 0, j)),
            pl.BlockSpec((None, 1, tn), lambda l, j: (l, 0, j)),
        ],
        out_specs=pl.BlockSpec((None, rows, tn), lambda l, j: (l, 0, j)),
        out_shape=jax.ShapeDtypeStruct((depth, rows, n), F32),
        compiler_params=pltpu.CompilerParams(
            dimension_semantics=("arbitrary", "arbitrary"), vmem_limit_bytes=VMEM_LIMIT),
        name="modulation",
    )(cc, ada_w, ada_b.reshape(depth, 1, n))


def _norm_modulate(x, gain, shift, scale):
    ms = jnp.mean(x * x, axis=-1, keepdims=True)
    return x * lax.rsqrt(ms + EPS) * (gain * (1.0 + scale)) + shift


def _sigmoid(x):
    return 0.5 * jnp.tanh(0.5 * x) + 0.5


def _proj_kernel(x_ref, sh_ref, sc_ref, ng_ref, w_ref, wv_ref, qg_ref, kg_ref, lng_ref,
                 cos_ref, sin_ref, gmat_ref, o_ref, vt_ref, h_ref, wvt_ref, *, tm, use_rope, z_tiles):
    @pl.when(pl.program_id(0) == 0)
    def _():
        for c in range(0, B_WIDTH, MXU_DIM):
            wvt_ref[c:c + MXU_DIM, :] = wv_ref[:, c:c + MXU_DIM].astype(F32).T.astype(BF16)

    h_ref[...] = _norm_modulate(x_ref[...], ng_ref[...], sh_ref[...], sc_ref[...]).astype(BF16)

    def qk_epilogue(acc, cols, gain, scale):
        sq = (acc * acc).astype(BF16)
        ms = jnp.concatenate(
            [jnp.dot(sq[:, c:c + MXU_DIM], gmat_ref[...], preferred_element_type=F32)
             for c in range(0, PROJ_TN, MXU_DIM)], axis=1)
        y = acc * lax.rsqrt(ms + EPS) * gain
        if not use_rope:
            o_ref[:, cols] = (y * scale).astype(BF16)
            return
        lane = lax.broadcasted_iota(jnp.int32, (tm, LANES), 1)
        first_half = (lane & 16) == 0
        cos = cos_ref[...]
        sin = sin_ref[...]
        for c in range(PROJ_TN // LANES):
            yc = y[:, c * LANES:(c + 1) * LANES]
            partner = jnp.where(first_half, pltpu.roll(yc, LANES - 16, 1), pltpu.roll(yc, 16, 1))
            lo = cols.start + c * LANES
            o_ref[:, lo:lo + LANES] = ((yc * cos + partner * sin) * scale).astype(BF16)

    for n, j in enumerate(z_tiles):
        cols = slice(n * PROJ_TN, (n + 1) * PROJ_TN)
        acc = jnp.dot(h_ref[...], w_ref[:, j * PROJ_TN:(j + 1) * PROJ_TN],
                      preferred_element_type=F32)
        if j == _J_U:
            o_ref[:, cols] = jax.nn.gelu(acc).astype(BF16)
        elif j == _J_V:
            g = jax.nn.gelu(acc)
            mu = jnp.mean(g, axis=-1, keepdims=True)
            c = g - mu
            var = jnp.mean(c * c, axis=-1, keepdims=True)
            o_ref[:, cols] = (c * lax.rsqrt(var + EPS) * lng_ref[...]).astype(BF16)
        elif j < _J_K:
            qk_epilogue(acc, cols, qg_ref[...], QK_SCALE_LOG2E)
        elif j < _J_GATE:
            qk_epilogue(acc, cols, kg_ref[...], 1.0)
        elif j < _J_CIN:
            o_ref[:, cols] = _sigmoid(acc).astype(BF16)
        else:
            o_ref[:, cols] = acc.astype(BF16)

    for r in range(B_WIDTH // PROJ_TN):
        rs = slice(r * PROJ_TN, (r + 1) * PROJ_TN)
        vt_ref[rs, :] = lax.dot_general(wvt_ref[rs, :], h_ref[...], _NT,
                                        preferred_element_type=F32).astype(BF16)


def _resident(shape, layer=None):
    if layer is None:
        return pl.BlockSpec(shape, lambda *_: (0,) * len(shape), pipeline_mode=pl.Buffered(1))
    return pl.BlockSpec((None,) + tuple(shape), lambda *_: (layer,) + (0,) * len(shape),
                        pipeline_mode=pl.Buffered(1))


def _proj(x2d, shift, scale, norm_g, w, wv, qg, kg, lng, cos, sin, gmat, *, tm, seq_len, use_rope, layer,
          keys_values_only=False):
    z_tiles = tuple(range(_J_K, _J_GATE)) if keys_values_only else tuple(range(_N_Z_TILES))
    z_cols = len(z_tiles) * PROJ_TN
    rows, d = x2d.shape
    tps = seq_len // tm
    nb = shift.shape[0]
    pos_tiles = cos.shape[0] // tm

    def bidx(i):
        return (i // tps) % nb

    return pl.pallas_call(
        functools.partial(_proj_kernel, tm=tm, use_rope=use_rope, z_tiles=z_tiles),
        grid=(rows // tm,),
        in_specs=[
            pl.BlockSpec((tm, d), lambda i: (i, 0)),
            pl.BlockSpec((None, 1, d), lambda i: (bidx(i), 0, 0)),
            pl.BlockSpec((None, 1, d), lambda i: (bidx(i), 0, 0)),
            _resident((1, d)),
            _resident((d, Z_COLS), layer),
            _resident((d, B_WIDTH), layer),
            _resident((1, PROJ_TN)),
            _resident((1, PROJ_TN)),
            _resident((1, PROJ_TN)),
            pl.BlockSpec((tm, LANES), lambda i: (i % pos_tiles, 0)),
            pl.BlockSpec((tm, LANES), lambda i: (i % pos_tiles, 0)),
            _resident((MXU_DIM, MXU_DIM)),
        ],
        out_specs=[
            pl.BlockSpec((tm, z_cols), lambda i: (i, 0)),
            pl.BlockSpec((B_WIDTH, tm), lambda i: (0, i)),
        ],
        out_shape=[
            jax.ShapeDtypeStruct((rows, z_cols), BF16),
            jax.ShapeDtypeStruct((B_WIDTH, rows), BF16),
        ],
        scratch_shapes=[pltpu.VMEM((tm, d), BF16), pltpu.VMEM((B_WIDTH, d), BF16)],
        compiler_params=pltpu.CompilerParams(
            dimension_semantics=("arbitrary",), vmem_limit_bytes=VMEM_LIMIT),
        name="proj",
    )(x2d, shift, scale, norm_g, w, wv, qg, kg, lng, cos, sin, gmat)


def _attn_kernel(*refs, tq, tk, src_rows, lam_init, layer, heads):
    n_src = len(src_rows)
    kv_refs = [(refs[1 + 2 * s], refs[2 + 2 * s]) for s in range(n_src)]
    cap_ref = refs[1 + 2 * n_src]

    cap = cap_ref[layer]
    safe = cap < SAFE_EXP2_RANGE

    @pl.when(safe)
    def _():
        for hh in range(heads):
            _attn_head(refs, hh, cap, tq=tq, tk=tk, src_rows=src_rows, lam_init=lam_init, row_max=False)

    @pl.when(jnp.logical_not(safe))
    def _():
        for hh in range(heads):
            _attn_head(refs, hh, cap, tq=tq, tk=tk, src_rows=src_rows, lam_init=lam_init, row_max=True)


def _attn_head(refs, hh, cap, *, tq, tk, src_rows, lam_init, row_max):
    n_src = len(src_rows)
    q_ref = refs[0]
    kv_refs = [(refs[1 + 2 * s], refs[2 + 2 * s]) for s in range(n_src)]
    _, lam_ref, subg_ref, o_ref, p1_ref, p2_ref = refs[1 + 2 * n_src:]
    head_lanes = slice(hh * LANES, (hh + 1) * LANES)
    head_rows = slice(hh * B_V_DIM, (hh + 1) * B_V_DIM)

    lane_k = lax.broadcasted_iota(jnp.int32, (1, LANES), 1)
    map1 = lane_k < B_QK_DIM

    q = q_ref[:, head_lanes]
    zero = jnp.zeros_like(q)
    q1 = jnp.where(map1, q, zero)
    q2 = jnp.where(map1, zero, q)

    def scores_t(qm, k_ref, start, size):
        kb = k_ref[start:start + size, head_lanes]
        return lax.dot_general(kb, qm, _NT, preferred_element_type=F32)

    src_offsets = [sum(src_rows[:s]) for s in range(n_src)]

    def blocks(fn, carry):
        for (k_ref, _), rows, offset in zip(kv_refs, src_rows, src_offsets):
            size = min(tk, rows)
            for b in range(rows // size):
                carry = fn(k_ref, b * size, size, offset + b * size, carry)
        return carry

    def fold(p, op):
        acc = p[0:SUBLANES, :]
        for r in range(1, p.shape[0] // SUBLANES):
            acc = op(acc, p[r * SUBLANES:(r + 1) * SUBLANES, :])
        return acc

    def finish(shift1, shift2):
        def step(k_ref, start, size, base, carry):
            l1, l2 = carry
            p1 = jnp.exp2(scores_t(q1, k_ref, start, size) - shift1)
            p2 = jnp.exp2(scores_t(q2, k_ref, start, size) - shift2)
            p1_ref[base:base + size, :] = p1.astype(BF16)
            p2_ref[base:base + size, :] = p2.astype(BF16)
            return l1 + fold(p1, jnp.add), l2 + fold(p2, jnp.add)

        zl = jnp.zeros((SUBLANES, tq), F32)
        l1, l2 = blocks(step, (zl, zl))

        lv = lam_ref[...]
        lam = (jnp.exp(jnp.sum(lv[0:1] * lv[1:2], axis=1, keepdims=True))
               - jnp.exp(jnp.sum(lv[2:3] * lv[3:4], axis=1, keepdims=True)) + lam_init)
        r1 = (1.0 / jnp.sum(l1, axis=0, keepdims=True)).astype(BF16)
        r2 = (lam / jnp.sum(l2, axis=0, keepdims=True)).astype(BF16)

        parts = []
        for qs in range(0, tq, MXU_DIM):
            acc = jnp.zeros((B_V_DIM, MXU_DIM), F32)
            r1q = r1[:, qs:qs + MXU_DIM]
            r2q = r2[:, qs:qs + MXU_DIM]
            for (_, vt_ref), rows, base in zip(kv_refs, src_rows, src_offsets):
                for ks in range(0, rows, MXU_DIM):
                    keys = slice(base + ks, base + ks + MXU_DIM)
                    a_tile = p1_ref[keys, qs:qs + MXU_DIM] * r1q - p2_ref[keys, qs:qs + MXU_DIM] * r2q
                    acc = acc + jnp.dot(vt_ref[head_rows, ks:ks + MXU_DIM], a_tile, preferred_element_type=F32)
            parts.append(acc)
        a = jnp.concatenate(parts, axis=1)

        ms = jnp.mean(a * a, axis=0, keepdims=True)
        y = a * lax.rsqrt(ms + EPS) * subg_ref[...] * (1.0 - lam_init)
        o_ref[:, head_lanes] = y.T.astype(BF16)

    if not row_max:
        finish(cap, cap)
        return

    def mx_step(k_ref, start, size, base, carry):
        m1, m2 = carry
        m1 = jnp.maximum(m1, fold(scores_t(q1, k_ref, start, size), jnp.maximum))
        m2 = jnp.maximum(m2, fold(scores_t(q2, k_ref, start, size), jnp.maximum))
        return m1, m2

    neg = jnp.full((SUBLANES, tq), -jnp.inf, F32)
    m1, m2 = blocks(mx_step, (neg, neg))
    finish(jnp.max(m1, axis=0, keepdims=True), jnp.max(m2, axis=0, keepdims=True))


def _attention(zq, kv_sources, score_cap, lam_vecs, subg, *, batch, seq_len, tq, tk, lam_init, layer, heads=1):
    rows = zq.shape[0]
    nq = seq_len // tq
    width = heads * LANES
    q_col = _J_Q * PROJ_TN // width

    in_specs = [pl.BlockSpec((tq, width), lambda b, h, i: (b * nq + i, q_col + h))]
    args = [zq]
    src_rows = []
    for z, vt, n in kv_sources:
        k_col = _J_K * PROJ_TN // width if z.shape[1] == Z_COLS else 0
        in_specs.append(pl.BlockSpec((n, width), lambda b, h, i, k_col=k_col: (b, k_col + h)))
        in_specs.append(pl.BlockSpec((heads * B_V_DIM, n), lambda b, h, i: (h, b)))
        args += [z, vt]
        src_rows.append(n)
    in_specs.append(pl.BlockSpec(memory_space=pltpu.SMEM))
    in_specs.append(pl.BlockSpec(lam_vecs.shape, lambda b, h, i: (0, 0)))
    in_specs.append(pl.BlockSpec((B_V_DIM, 1), lambda b, h, i: (0, 0)))
    args += [score_cap, lam_vecs, subg]

    return pl.pallas_call(
        functools.partial(_attn_kernel, tq=tq, tk=tk, src_rows=tuple(src_rows), lam_init=lam_init,
                          layer=layer, heads=heads),
        grid=(batch, B_HEADS // heads, nq),
        in_specs=in_specs,
        out_specs=pl.BlockSpec((tq, width), lambda b, h, i: (b * nq + i, h)),
        out_shape=jax.ShapeDtypeStruct((rows, B_WIDTH), BF16),
        scratch_shapes=[
            pltpu.VMEM((sum(src_rows), tq), BF16),
            pltpu.VMEM((sum(src_rows), tq), BF16),
        ],
        compiler_params=pltpu.CompilerParams(
            dimension_semantics=("arbitrary", "arbitrary", "arbitrary"), vmem_limit_bytes=VMEM_LIMIT),
        name="attn",
    )(*args)


def _merge_kernel(u_ref, v_ref, yb_ref, ga_ref, gb_ref, gc_ref, cb_ref, cg_ref, cx_ref,
                  pg_ref, px_ref, ng_ref, nx_ref, x_ref, g1_ref,
                  sw_ref, sb_ref, cw_ref, wa_ref, wb_ref, wc_ref, wo_ref,
                  o_ref, ya_ref, yc_ref, *, tm, tps):
    i = pl.program_id(0)
    first = (i % tps) == 0
    last = (i % tps) == (tps - 1)

    for c in range(tm // CHUNK):
        rs = slice(c * CHUNK, (c + 1) * CHUNK)
        for g in range(A_GROUPS):
            cs = slice(g * LANES, (g + 1) * LANES)
            mixed = jnp.dot(sw_ref[g], v_ref[rs, cs], preferred_element_type=F32) + sb_ref[:, cs]
            ya_ref[rs, cs] = (u_ref[rs, cs].astype(F32) * mixed).astype(BF16)

    t = cg_ref[...].astype(F32) * cx_ref[...].astype(F32)
    t_prev_row = pg_ref[HALO - 1:HALO, :].astype(F32) * px_ref[HALO - 1:HALO, :].astype(F32)
    t_next_row = ng_ref[0:1, :].astype(F32) * nx_ref[0:1, :].astype(F32)
    t_prev_row = jnp.where(first, 0.0, t_prev_row)
    t_next_row = jnp.where(last, 0.0, t_next_row)
    row = lax.broadcasted_iota(jnp.int32, (tm, C_WIDTH), 0)
    t_prev = jnp.where(row == 0, t_prev_row, pltpu.roll(t, 1, 0))
    t_next = jnp.where(row == tm - 1, t_next_row, pltpu.roll(t, tm - 1, 0))
    cw = cw_ref[...]
    y_c = cb_ref[...].astype(F32) * (t_prev * cw[0:1] + t * cw[1:2] + t_next * cw[2:3])
    yc_ref[...] = y_c.astype(BF16)

    for r in range(0, tm, MERGE_ROWS):
        rows = slice(r, r + MERGE_ROWS)
        m = ga_ref[rows, :].astype(F32) * jnp.dot(ya_ref[rows, :], wa_ref[...], preferred_element_type=F32)
        m = m + gb_ref[rows, :].astype(F32) * jnp.dot(yb_ref[rows, :], wb_ref[...], preferred_element_type=F32)
        m = m + gc_ref[rows, :].astype(F32) * jnp.dot(yc_ref[rows, :], wc_ref[...], preferred_element_type=F32)
        out = jnp.dot(m.astype(BF16), wo_ref[...], preferred_element_type=F32)
        o_ref[rows, :] = x_ref[rows, :] + g1_ref[...] * out


def _merge(z, yb, x2d, g1, sw, sb, cw, wa, wb, wc, wo, *, tm, seq_len, layer):
    rows, d = x2d.shape
    tps = seq_len // tm
    nb = g1.shape[0]
    hb = tm // HALO
    n_halo = rows // HALO
    gate_col = _J_GATE * PROJ_TN // D_MODEL

    def bidx(i):
        return (i // tps) % nb

    def prev_blk(i):
        return jnp.maximum(i * hb - 1, 0)

    def next_blk(i):
        return jnp.minimum((i + 1) * hb, n_halo - 1)

    return pl.pallas_call(
        functools.partial(_merge_kernel, tm=tm, tps=tps),
        grid=(rows // tm,),
        in_specs=[
            pl.BlockSpec((tm, A_WIDTH), lambda i: (i, _J_U)),
            pl.BlockSpec((tm, A_WIDTH), lambda i: (i, _J_V)),
            pl.BlockSpec((tm, B_WIDTH), lambda i: (i, 0)),
            pl.BlockSpec((tm, d), lambda i: (i, gate_col)),
            pl.BlockSpec((tm, d), lambda i: (i, gate_col + 1)),
            pl.BlockSpec((tm, d), lambda i: (i, gate_col + 2)),
            pl.BlockSpec((tm, C_WIDTH), lambda i: (i, _J_CIN)),
            pl.BlockSpec((tm, C_WIDTH), lambda i: (i, _J_CIN + 1)),
            pl.BlockSpec((tm, C_WIDTH), lambda i: (i, _J_CIN + 2)),
            pl.BlockSpec((HALO, C_WIDTH), lambda i: (prev_blk(i), _J_CIN + 1)),
            pl.BlockSpec((HALO, C_WIDTH), lambda i: (prev_blk(i), _J_CIN + 2)),
            pl.BlockSpec((HALO, C_WIDTH), lambda i: (next_blk(i), _J_CIN + 1)),
            pl.BlockSpec((HALO, C_WIDTH), lambda i: (next_blk(i), _J_CIN + 2)),
            pl.BlockSpec((tm, d), lambda i: (i, 0)),
            pl.BlockSpec((None, 1, d), lambda i: (bidx(i), 0, 0)),
            _resident((A_GROUPS, CHUNK, CHUNK)),
            _resident((CHUNK, A_WIDTH)),
            _resident((3, C_WIDTH)),
            _resident((A_WIDTH, d), layer),
            _resident((B_WIDTH, d), layer),
            _resident((C_WIDTH, d), layer),
            _resident((d, d), layer),
        ],
        out_specs=pl.BlockSpec((tm, d), lambda i: (i, 0)),
        out_shape=jax.ShapeDtypeStruct((rows, d), F32),
        scratch_shapes=[pltpu.VMEM((tm, A_WIDTH), BF16), pltpu.VMEM((tm, C_WIDTH), BF16)],
        compiler_params=pltpu.CompilerParams(
            dimension_semantics=("arbitrary",), vmem_limit_bytes=VMEM_LIMIT),
        name="merge",
    )(z, z, yb, z, z, z, z, z, z, z, z, z, z, x2d, g1, sw, sb, cw, wa, wb, wc, wo)


def _ffn_kernel(x_ref, xp_ref, xn_ref, sh_ref, sc_ref, gt_ref, ng_ref,
                wu_ref, cw_ref, cb_ref, wd_ref, o_ref, h_ref, u_ref, a_ref, *, tm, tps):
    i = pl.program_id(0)

    def nm(x):
        return _norm_modulate(x, ng_ref[...], sh_ref[...], sc_ref[...])

    first = (i % tps) == 0
    last = (i % tps) == (tps - 1)
    h_ref[HALO:HALO + tm, :] = nm(x_ref[...]).astype(BF16)
    h_ref[0:HALO, :] = jnp.where(first, 0.0, nm(xp_ref[...])).astype(BF16)
    h_ref[HALO + tm:, :] = jnp.where(last, 0.0, nm(xn_ref[...])).astype(BF16)

    def conv(slot, cols):
        cw = cw_ref[:, cols]
        return (u_ref[slot, HALO - 1:HALO - 1 + tm, :] * cw[0:1] + u_ref[slot, HALO:HALO + tm, :] * cw[1:2]
                + u_ref[slot, HALO + 1:HALO + 1 + tm, :] * cw[2:3] + cb_ref[:, cols])

    for c in range(D_FF // FF_TN):
        g_cols = slice(c * FF_TN, (c + 1) * FF_TN)
        v_cols = slice(D_FF + c * FF_TN, D_FF + (c + 1) * FF_TN)
        g_slot = 2 * (c % 2)
        v_slot = g_slot + 1
        h = h_ref[...]
        u_ref[g_slot] = jnp.dot(h, wu_ref[:, g_cols], preferred_element_type=F32)
        u_ref[v_slot] = jnp.dot(h, wu_ref[:, v_cols], preferred_element_type=F32)
        g = conv(g_slot, g_cols)
        val = conv(v_slot, v_cols)
        a_ref[:, g_cols] = (g * _sigmoid(g) * val).astype(BF16)

    out = jnp.dot(a_ref[...], wd_ref[...], preferred_element_type=F32)
    o_ref[...] = x_ref[...] + gt_ref[...] * out


def _ffn(x2d, shift, scale, gate, norm_g, w_up, cw, cb, w_down, *, tm, seq_len, layer):
    rows, d = x2d.shape
    tps = seq_len // tm
    nb = shift.shape[0]
    hb = tm // HALO
    n_halo = rows // HALO

    def bidx(i):
        return (i // tps) % nb

    vec = pl.BlockSpec((None, 1, d), lambda i: (bidx(i), 0, 0))
    return pl.pallas_call(
        functools.partial(_ffn_kernel, tm=tm, tps=tps),
        grid=(rows // tm,),
        in_specs=[
            pl.BlockSpec((tm, d), lambda i: (i, 0)),
            pl.BlockSpec((HALO, d), lambda i: (jnp.maximum(i * hb - 1, 0), 0)),
            pl.BlockSpec((HALO, d), lambda i: (jnp.minimum((i + 1) * hb, n_halo - 1), 0)),
            vec, vec, vec,
            _resident((1, d)),
            _resident((d, 2 * D_FF), layer),
            _resident((3, 2 * D_FF)),
            _resident((1, 2 * D_FF)),
            _resident((D_FF, d), layer),
        ],
        out_specs=pl.BlockSpec((tm, d), lambda i: (i, 0)),
        out_shape=jax.ShapeDtypeStruct((rows, d), F32),
        scratch_shapes=[
            pltpu.VMEM((tm + 2 * HALO, d), BF16),
            pltpu.VMEM((4, tm + 2 * HALO, FF_TN), F32),
            pltpu.VMEM((tm, D_FF), BF16),
        ],
        compiler_params=pltpu.CompilerParams(
            dimension_semantics=("arbitrary",), vmem_limit_bytes=VMEM_LIMIT),
        name="ffn",
    )(x2d, x2d, x2d, shift, scale, gate, norm_g, w_up, cw, cb, w_down)


def kernel(x, c, ctx, c_ctx, ada_w, ada_b, norm1_g, norm2_g, w_in, sgu_ln_g, sgu_w, sgu_b, q_norm_g, k_norm_g, lam_q1, lam_k1, lam_q2, lam_k2, subln_g, conv_w, w_br_a, w_br_b, w_br_c, w_out, ffn_up, ffn_conv_w, ffn_conv_b, ffn_down):
    batch, seq, d = x.shape
    ctx_len = ctx.shape[1]
    depth = ada_w.shape[0]
    assert d == D_MODEL and seq % GRID_W == 0

    tm_lat = min(512, seq)
    tm_mrg = min(512, seq)
    tm_ctx = ctx_len
    tq_lat = min(1024, seq)
    tk = 512
    assert seq % tm_lat == 0 and seq % tq_lat == 0 and ctx_len % HALO == 0

    w_in_b = _z_weights(w_in)
    wv_b = w_in[:, :, _REF_B_V:_REF_B_V + B_WIDTH].astype(BF16)
    wa_b, wb_b, wc_b, wo_b = (w.astype(BF16) for w in (w_br_a, w_br_b, w_br_c, w_out))
    up_b = ffn_up.astype(BF16)
    down_b = ffn_down.astype(BF16)
    sw_b = sgu_w.astype(BF16)
    sgu_bias = jnp.repeat(jnp.swapaxes(sgu_b, 1, 2), LANES, axis=2)
    gmat = jnp.asarray(np.kron(np.eye(MXU_DIM // B_QK_DIM), np.full((B_QK_DIM, B_QK_DIM), 1.0 / B_QK_DIM)), BF16)
    cos, sin = _rope_tables(seq)
    cos_ctx = jnp.ones((tm_ctx, LANES), F32)
    sin_ctx = jnp.zeros((tm_ctx, LANES), F32)
    lam_vecs = jnp.stack([lam_q1, lam_k1, lam_q2, lam_k2], axis=1).astype(F32)
    score_cap = (1.02 * B_QK_DIM * QK_SCALE_LOG2E
                 * jnp.max(jnp.abs(q_norm_g), axis=1) * jnp.max(jnp.abs(k_norm_g), axis=1)).astype(F32)

    pad = (-(batch + 1)) % 8
    cc = jnp.concatenate([c, c_ctx[None, :], jnp.zeros((pad, d), F32)], axis=0)
    mod = _modulation(cc, ada_w, ada_b)
    mod = mod.reshape(depth, cc.shape[0], 6, 1, d)

    xs = x.reshape(batch * seq, d)
    cs = ctx.reshape(batch * ctx_len, d)

    for l in range(depth):
        last = l == depth - 1
        lam_init = 0.8 - 0.6 * math.exp(-0.3 * l)
        lat = [mod[l, :batch, k] for k in range(6)]
        cm = [mod[l, batch:batch + 1, k] for k in range(6)]
        n1 = norm1_g[l].reshape(1, d)
        n2 = norm2_g[l].reshape(1, d)
        qg = jnp.tile(q_norm_g[l], PROJ_TN // B_QK_DIM).reshape(1, PROJ_TN)
        kg = jnp.tile(k_norm_g[l], PROJ_TN // B_QK_DIM).reshape(1, PROJ_TN)
        lng = sgu_ln_g[l].reshape(1, A_WIDTH)
        subg = subln_g[l].reshape(B_V_DIM, 1)

        z_lat, vt_lat = _proj(xs, lat[0], lat[1], n1, w_in_b, wv_b, qg, kg, lng, cos, sin, gmat,
                              tm=tm_lat, seq_len=seq, use_rope=True, layer=l)
        z_ctx, vt_ctx = _proj(cs, cm[0], cm[1], n1, w_in_b, wv_b, qg, kg, lng, cos_ctx, sin_ctx, gmat,
                              tm=tm_ctx, seq_len=ctx_len, use_rope=False, layer=l, keys_values_only=last)

        yb_lat = _attention(z_lat, [(z_lat, vt_lat, seq), (z_ctx, vt_ctx, ctx_len)], score_cap, lam_vecs[l], subg,
                            batch=batch, seq_len=seq, tq=tq_lat, tk=tk, lam_init=lam_init, layer=l)
        merge_w = (sw_b[l], sgu_bias[l], conv_w[l], wa_b, wb_b, wc_b, wo_b)
        ffn_w = (up_b, ffn_conv_w[l], ffn_conv_b[l].reshape(1, 2 * D_FF), down_b)
        xs = _merge(z_lat, yb_lat, xs, lat[2], *merge_w, tm=tm_mrg, seq_len=seq, layer=l)
        xs = _ffn(xs, lat[3], lat[4], lat[5], n2, *ffn_w, tm=tm_lat, seq_len=seq, layer=l)

        if not last:
            yb_ctx = _attention(z_ctx, [(z_ctx, vt_ctx, ctx_len)], score_cap, lam_vecs[l], subg,
                                batch=batch, seq_len=ctx_len, tq=tm_ctx, tk=tk, lam_init=lam_init, layer=l,
                                heads=B_HEADS)
            cs = _merge(z_ctx, yb_ctx, cs, cm[2], *merge_w, tm=tm_ctx, seq_len=ctx_len, layer=l)
            cs = _ffn(cs, cm[3], cm[4], cm[5], n2, *ffn_w, tm=tm_ctx, seq_len=ctx_len, layer=l)

    return xs.reshape(batch, seq, d)
```

```python
import functools
import math

import jax
import jax.numpy as jnp
import numpy as np
from jax import lax
from jax.experimental import pallas as pl
from jax.experimental.pallas import tpu as pltpu

F32 = jnp.float32
BF16 = jnp.bfloat16

D_MODEL = 1024
GRID_W = 64
A_WIDTH = 512
A_GROUPS = 4
CHUNK = 128
B_HEADS = 8
B_QK_DIM = 64
B_V_DIM = 128
B_WIDTH = B_HEADS * B_V_DIM
ROPE_THETA = 10000.0
C_WIDTH = 512
N_BRANCH = 3
D_FF = 2816
EPS = 1e-6

_REF_A_U = 0
_REF_A_V = 512
_REF_B_Q = 1024
_REF_B_K = 2048
_REF_B_V = 3072
_REF_C_IN = 4096
_REF_GATE = 5632

PROJ_TN = 512
_J_U = 0
_J_V = 1
_J_Q = 2
_J_K = 4
_J_GATE = 6
_J_CIN = 12
_J_VALT = 15
_N_Z_TILES = 15
_N_PROJ_STEPS = 17
Z_COLS = _N_Z_TILES * PROJ_TN

HALO = 16
LANES = 128
SUBLANES = 8
MXU_DIM = 256
FF_TN = 256
MERGE_ROWS = 256
QK_SCALE_LOG2E = (B_QK_DIM ** -0.5) * math.log2(math.e)
SAFE_EXP2_RANGE = 60.0
VMEM_LIMIT = 56 * 1024 * 1024
_NT = (((1,), (1,)), ((), ()))


def _z_weights(w_in):
    def head_major(w):
        lead = w.shape[:-1]
        return jnp.swapaxes(w.reshape(*lead, 2, B_HEADS, B_QK_DIM), -3, -2).reshape(*lead, 2 * B_HEADS * B_QK_DIM)

    parts = [
        w_in[..., _REF_A_U:_REF_B_Q],
        head_major(w_in[..., _REF_B_Q:_REF_B_K]),
        head_major(w_in[..., _REF_B_K:_REF_B_V]),
        w_in[..., _REF_GATE:_REF_GATE + N_BRANCH * D_MODEL],
        w_in[..., _REF_C_IN:_REF_GATE],
    ]
    w = jnp.concatenate([p.astype(BF16) for p in parts], axis=-1)
    assert w.shape[-1] == Z_COLS
    return w


def _rope_tables(n_tokens):
    rows = n_tokens // GRID_W
    row = jnp.repeat(jnp.arange(rows, dtype=F32), GRID_W)
    col = jnp.tile(jnp.arange(GRID_W, dtype=F32), rows)
    n_freq = B_QK_DIM // 4
    inv = ROPE_THETA ** (-jnp.arange(n_freq, dtype=F32) / n_freq)
    ang_r = row[:, None] * inv
    ang_c = col[:, None] * inv
    cos64 = jnp.concatenate([jnp.cos(ang_r), jnp.cos(ang_r), jnp.cos(ang_c), jnp.cos(ang_c)], axis=1)
    sin64 = jnp.concatenate([-jnp.sin(ang_r), jnp.sin(ang_r), -jnp.sin(ang_c), jnp.sin(ang_c)], axis=1)
    return jnp.tile(cos64, (1, 2)), jnp.tile(sin64, (1, 2))


def _mod_kernel(cc_ref, w_ref, b_ref, o_ref):
    cc = cc_ref[...]
    s = cc * jax.nn.sigmoid(cc)
    o_ref[...] = jnp.dot(s, w_ref[...], preferred_element_type=F32,
                         precision=lax.Precision.HIGHEST) + b_ref[...]


def _modulation(cc, ada_w, ada_b):
    depth, d, n = ada_w.shape
    tn = 1536
    rows = cc.shape[0]
    return pl.pallas_call(
        _mod_kernel,
        grid=(depth, n // tn),
        in_specs=[
            pl.BlockSpec((rows, d), lambda l, j: (0, 0)),
            pl.BlockSpec((None, d, tn), lambda l, j: (l, 0, j)),
            pl.BlockSpec((None, 1, tn), lambda l, j: (l, 0, j)),
        ],
        out_specs=pl.BlockSpec((None, rows, tn), lambda l, j: (l, 0, j)),
        out_shape=jax.ShapeDtypeStruct((depth, rows, n), F32),
        compiler_params=pltpu.CompilerParams(
            dimension_semantics=("arbitrary", "arbitrary"), vmem_limit_bytes=VMEM_LIMIT),
        name="modulation",
    )(cc, ada_w, ada_b.reshape(depth, 1, n))


def _norm_modulate(x, gain, shift, scale):
    ms = jnp.mean(x * x, axis=-1, keepdims=True)
    return x * lax.rsqrt(ms + EPS) * (gain * (1.0 + scale)) + shift


def _sigmoid(x):
    return 0.5 * jnp.tanh(0.5 * x) + 0.5


def _proj_kernel(x_ref, sh_ref, sc_ref, ng_ref, w_ref, wv_ref, qg_ref, kg_ref, lng_ref,
                 cos_ref, sin_ref, gmat_ref, o_ref, vt_ref, h_ref, wvt_ref, *, tm, use_rope, z_tiles):
    @pl.when(pl.program_id(0) == 0)
    def _():
        for c in range(0, B_WIDTH, MXU_DIM):
            wvt_ref[c:c + MXU_DIM, :] = wv_ref[:, c:c + MXU_DIM].astype(F32).T.astype(BF16)

    h_ref[...] = _norm_modulate(x_ref[...], ng_ref[...], sh_ref[...], sc_ref[...]).astype(BF16)

    def qk_epilogue(acc, cols, gain, scale):
        sq = (acc * acc).astype(BF16)
        ms = jnp.concatenate(
            [jnp.dot(sq[:, c:c + MXU_DIM], gmat_ref[...], preferred_element_type=F32)
             for c in range(0, PROJ_TN, MXU_DIM)], axis=1)
        y = acc * lax.rsqrt(ms + EPS) * gain
        if not use_rope:
            o_ref[:, cols] = (y * scale).astype(BF16)
            return
        lane = lax.broadcasted_iota(jnp.int32, (tm, LANES), 1)
        first_half = (lane & 16) == 0
        cos = cos_ref[...]
        sin = sin_ref[...]
        for c in range(PROJ_TN // LANES):
            yc = y[:, c * LANES:(c + 1) * LANES]
            partner = jnp.where(first_half, pltpu.roll(yc, LANES - 16, 1), pltpu.roll(yc, 16, 1))
            lo = cols.start + c * LANES
            o_ref[:, lo:lo + LANES] = ((yc * cos + partner * sin) * scale).astype(BF16)

    for n, j in enumerate(z_tiles):
        cols = slice(n * PROJ_TN, (n + 1) * PROJ_TN)
        acc = jnp.dot(h_ref[...], w_ref[:, j * PROJ_TN:(j + 1) * PROJ_TN],
                      preferred_element_type=F32)
        if j == _J_U:
            o_ref[:, cols] = jax.nn.gelu(acc).astype(BF16)
        elif j == _J_V:
            g = jax.nn.gelu(acc)
            mu = jnp.mean(g, axis=-1, keepdims=True)
            c = g - mu
            var = jnp.mean(c * c, axis=-1, keepdims=True)
            o_ref[:, cols] = (c * lax.rsqrt(var + EPS) * lng_ref[...]).astype(BF16)
        elif j < _J_K:
            qk_epilogue(acc, cols, qg_ref[...], QK_SCALE_LOG2E)
        elif j < _J_GATE:
            qk_epilogue(acc, cols, kg_ref[...], 1.0)
        elif j < _J_CIN:
            o_ref[:, cols] = _sigmoid(acc).astype(BF16)
        else:
            o_ref[:, cols] = acc.astype(BF16)

    for r in range(B_WIDTH // PROJ_TN):
        rs = slice(r * PROJ_TN, (r + 1) * PROJ_TN)
        vt_ref[rs, :] = lax.dot_general(wvt_ref[rs, :], h_ref[...], _NT,
                                        preferred_element_type=F32).astype(BF16)


def _resident(shape, layer=None):
    if layer is None:
        return pl.BlockSpec(shape, lambda *_: (0,) * len(shape), pipeline_mode=pl.Buffered(1))
    return pl.BlockSpec((None,) + tuple(shape), lambda *_: (layer,) + (0,) * len(shape),
                        pipeline_mode=pl.Buffered(1))


def _proj(x2d, shift, scale, norm_g, w, wv, qg, kg, lng, cos, sin, gmat, *, tm, seq_len, use_rope, layer,
          keys_values_only=False):
    z_tiles = tuple(range(_J_K, _J_GATE)) if keys_values_only else tuple(range(_N_Z_TILES))
    z_cols = len(z_tiles) * PROJ_TN
    rows, d = x2d.shape
    tps = seq_len // tm
    nb = shift.shape[0]
    pos_tiles = cos.shape[0] // tm

    def bidx(i):
        return (i // tps) % nb

    return pl.pallas_call(
        functools.partial(_proj_kernel, tm=tm, use_rope=use_rope, z_tiles=z_tiles),
        grid=(rows // tm,),
        in_specs=[
            pl.BlockSpec((tm, d), lambda i: (i, 0)),
            pl.BlockSpec((None, 1, d), lambda i: (bidx(i), 0, 0)),
            pl.BlockSpec((None, 1, d), lambda i: (bidx(i), 0, 0)),
            _resident((1, d)),
            _resident((d, Z_COLS), layer),
            _resident((d, B_WIDTH), layer),
            _resident((1, PROJ_TN)),
            _resident((1, PROJ_TN)),
            _resident((1, PROJ_TN)),
            pl.BlockSpec((tm, LANES), lambda i: (i % pos_tiles, 0)),
            pl.BlockSpec((tm, LANES), lambda i: (i % pos_tiles, 0)),
            _resident((MXU_DIM, MXU_DIM)),
        ],
        out_specs=[
            pl.BlockSpec((tm, z_cols), lambda i: (i, 0)),
            pl.BlockSpec((B_WIDTH, tm), lambda i: (0, i)),
        ],
        out_shape=[
            jax.ShapeDtypeStruct((rows, z_cols), BF16),
            jax.ShapeDtypeStruct((B_WIDTH, rows), BF16),
        ],
        scratch_shapes=[pltpu.VMEM((tm, d), BF16), pltpu.VMEM((B_WIDTH, d), BF16)],
        compiler_params=pltpu.CompilerParams(
            dimension_semantics=("arbitrary",), vmem_limit_bytes=VMEM_LIMIT),
        name="proj",
    )(x2d, shift, scale, norm_g, w, wv, qg, kg, lng, cos, sin, gmat)


def _attn_kernel(*refs, tq, tk, src_rows, lam_init, layer, heads):
    n_src = len(src_rows)
    kv_refs = [(refs[1 + 2 * s], refs[2 + 2 * s]) for s in range(n_src)]
    cap_ref = refs[1 + 2 * n_src]

    cap = cap_ref[layer]
    safe = cap < SAFE_EXP2_RANGE

    @pl.when(safe)
    def _():
        for hh in range(heads):
            _attn_head(refs, hh, cap, tq=tq, tk=tk, src_rows=src_rows, lam_init=lam_init, row_max=False)

    @pl.when(jnp.logical_not(safe))
    def _():
        for hh in range(heads):
            _attn_head(refs, hh, cap, tq=tq, tk=tk, src_rows=src_rows, lam_init=lam_init, row_max=True)


def _attn_head(refs, hh, cap, *, tq, tk, src_rows, lam_init, row_max):
    n_src = len(src_rows)
    q_ref = refs[0]
    kv_refs = [(refs[1 + 2 * s], refs[2 + 2 * s]) for s in range(n_src)]
    _, lam_ref, subg_ref, o_ref, p1_ref, p2_ref = refs[1 + 2 * n_src:]
    head_lanes = slice(hh * LANES, (hh + 1) * LANES)
    head_rows = slice(hh * B_V_DIM, (hh + 1) * B_V_DIM)

    lane_k = lax.broadcasted_iota(jnp.int32, (1, LANES), 1)
    map1 = lane_k < B_QK_DIM

    q = q_ref[:, head_lanes]
    zero = jnp.zeros_like(q)
    q1 = jnp.where(map1, q, zero)
    q2 = jnp.where(map1, zero, q)

    def scores_t(qm, k_ref, start, size):
        kb = k_ref[start:start + size, head_lanes]
        return lax.dot_general(kb, qm, _NT, preferred_element_type=F32)

    src_offsets = [sum(src_rows[:s]) for s in range(n_src)]

    def blocks(fn, carry):
        for (k_ref, _), rows, offset in zip(kv_refs, src_rows, src_offsets):
            size = min(tk, rows)
            for b in range(rows // size):
                carry = fn(k_ref, b * size, size, offset + b * size, carry)
        return carry

    def fold(p, op):
        acc = p[0:SUBLANES, :]
        for r in range(1, p.shape[0] // SUBLANES):
            acc = op(acc, p[r * SUBLANES:(r + 1) * SUBLANES, :])
        return acc

    def finish(shift1, shift2):
        def step(k_ref, start, size, base, carry):
            l1, l2 = carry
            p1 = jnp.exp2(scores_t(q1, k_ref, start, size) - shift1)
            p2 = jnp.exp2(scores_t(q2, k_ref, start, size) - shift2)
            p1_ref[base:base + size, :] = p1.astype(BF16)
            p2_ref[base:base + size, :] = p2.astype(BF16)
            return l1 + fold(p1, jnp.add), l2 + fold(p2, jnp.add)

        zl = jnp.zeros((SUBLANES, tq), F32)
        l1, l2 = blocks(step, (zl, zl))

        lv = lam_ref[...]
        lam = (jnp.exp(jnp.sum(lv[0:1] * lv[1:2], axis=1, keepdims=True))
               - jnp.exp(jnp.sum(lv[2:3] * lv[3:4], axis=1, keepdims=True)) + lam_init)
        r1 = (1.0 / jnp.sum(l1, axis=0, keepdims=True)).astype(BF16)
        r2 = (lam / jnp.sum(l2, axis=0, keepdims=True)).astype(BF16)

        parts = []
        for qs in range(0, tq, MXU_DIM):
            acc = jnp.zeros((B_V_DIM, MXU_DIM), F32)
            r1q = r1[:, qs:qs + MXU_DIM]
            r2q = r2[:, qs:qs + MXU_DIM]
            for (_, vt_ref), rows, base in zip(kv_refs, src_rows, src_offsets):
                for ks in range(0, rows, MXU_DIM):
                    keys = slice(base + ks, base + ks + MXU_DIM)
                    a_tile = p1_ref[keys, qs:qs + MXU_DIM] * r1q - p2_ref[keys, qs:qs + MXU_DIM] * r2q
                    acc = acc + jnp.dot(vt_ref[head_rows, ks:ks + MXU_DIM], a_tile, preferred_element_type=F32)
            parts.append(acc)
        a = jnp.concatenate(parts, axis=1)

        ms = jnp.mean(a * a, axis=0, keepdims=True)
        y = a * lax.rsqrt(ms + EPS) * subg_ref[...] * (1.0 - lam_init)
        o_ref[:, head_lanes] = y.T.astype(BF16)

    if not row_max:
        finish(cap, cap)
        return

    def mx_step(k_ref, start, size, base, carry):
        m1, m2 = carry
        m1 = jnp.maximum(m1, fold(scores_t(q1, k_ref, start, size), jnp.maximum))
        m2 = jnp.maximum(m2, fold(scores_t(q2, k_ref, start, size), jnp.maximum))
        return m1, m2

    neg = jnp.full((SUBLANES, tq), -jnp.inf, F32)
    m1, m2 = blocks(mx_step, (neg, neg))
    finish(jnp.max(m1, axis=0, keepdims=True), jnp.max(m2, axis=0, keepdims=True))


def _attention(zq, kv_sources, score_cap, lam_vecs, subg, *, batch, seq_len, tq, tk, lam_init, layer, heads=1):
    rows = zq.shape[0]
    nq = seq_len // tq
    width = heads * LANES
    q_col = _J_Q * PROJ_TN // width

    in_specs = [pl.BlockSpec((tq, width), lambda b, h, i: (b * nq + i, q_col + h))]
    args = [zq]
    src_rows = []
    for z, vt, n in kv_sources:
        k_col = _J_K * PROJ_TN // width if z.shape[1] == Z_COLS else 0
        in_specs.append(pl.BlockSpec((n, width), lambda b, h, i, k_col=k_col: (b, k_col + h)))
        in_specs.append(pl.BlockSpec((heads * B_V_DIM, n), lambda b, h, i: (h, b)))
        args += [z, vt]
        src_rows.append(n)
    in_specs.append(pl.BlockSpec(memory_space=pltpu.SMEM))
    in_specs.append(pl.BlockSpec(lam_vecs.shape, lambda b, h, i: (0, 0)))
    in_specs.append(pl.BlockSpec((B_V_DIM, 1), lambda b, h, i: (0, 0)))
    args += [score_cap, lam_vecs, subg]

    return pl.pallas_call(
        functools.partial(_attn_kernel, tq=tq, tk=tk, src_rows=tuple(src_rows), lam_init=lam_init,
                          layer=layer, heads=heads),
        grid=(batch, B_HEADS // heads, nq),
        in_specs=in_specs,
        out_specs=pl.BlockSpec((tq, width), lambda b, h, i: (b * nq + i, h)),
        out_shape=jax.ShapeDtypeStruct((rows, B_WIDTH), BF16),
        scratch_shapes=[
            pltpu.VMEM((sum(src_rows), tq), BF16),
            pltpu.VMEM((sum(src_rows), tq), BF16),
        ],
        compiler_params=pltpu.CompilerParams(
            dimension_semantics=("arbitrary", "arbitrary", "arbitrary"), vmem_limit_bytes=VMEM_LIMIT),
        name="attn",
    )(*args)


def _merge_kernel(u_ref, v_ref, yb_ref, ga_ref, gb_ref, gc_ref, cb_ref, cg_ref, cx_ref,
                  pg_ref, px_ref, ng_ref, nx_ref, x_ref, g1_ref,
                  sw_ref, sb_ref, cw_ref, wa_ref, wb_ref, wc_ref, wo_ref,
                  o_ref, ya_ref, yc_ref, *, tm, tps):
    i = pl.program_id(0)
    first = (i % tps) == 0
    last = (i % tps) == (tps - 1)

    for c in range(tm // CHUNK):
        rs = slice(c * CHUNK, (c + 1) * CHUNK)
        for g in range(A_GROUPS):
            cs = slice(g * LANES, (g + 1) * LANES)
            mixed = jnp.dot(sw_ref[g], v_ref[rs, cs], preferred_element_type=F32) + sb_ref[:, cs]
            ya_ref[rs, cs] = (u_ref[rs, cs].astype(F32) * mixed).astype(BF16)

    t = cg_ref[...].astype(F32) * cx_ref[...].astype(F32)
    t_prev_row = pg_ref[HALO - 1:HALO, :].astype(F32) * px_ref[HALO - 1:HALO, :].astype(F32)
    t_next_row = ng_ref[0:1, :].astype(F32) * nx_ref[0:1, :].astype(F32)
    t_prev_row = jnp.where(first, 0.0, t_prev_row)
    t_next_row = jnp.where(last, 0.0, t_next_row)
    row = lax.broadcasted_iota(jnp.int32, (tm, C_WIDTH), 0)
    t_prev = jnp.where(row == 0, t_prev_row, pltpu.roll(t, 1, 0))
    t_next = jnp.where(row == tm - 1, t_next_row, pltpu.roll(t, tm - 1, 0))
    cw = cw_ref[...]
    y_c = cb_ref[...].astype(F32) * (t_prev * cw[0:1] + t * cw[1:2] + t_next * cw[2:3])
    yc_ref[...] = y_c.astype(BF16)

    for r in range(0, tm, MERGE_ROWS):
        rows = slice(r, r + MERGE_ROWS)
        m = ga_ref[rows, :].astype(F32) * jnp.dot(ya_ref[rows, :], wa_ref[...], preferred_element_type=F32)
        m = m + gb_ref[rows, :].astype(F32) * jnp.dot(yb_ref[rows, :], wb_ref[...], preferred_element_type=F32)
        m = m + gc_ref[rows, :].astype(F32) * jnp.dot(yc_ref[rows, :], wc_ref[...], preferred_element_type=F32)
        out = jnp.dot(m.astype(BF16), wo_ref[...], preferred_element_type=F32)
        o_ref[rows, :] = x_ref[rows, :] + g1_ref[...] * out


def _merge(z, yb, x2d, g1, sw, sb, cw, wa, wb, wc, wo, *, tm, seq_len, layer):
    rows, d = x2d.shape
    tps = seq_len // tm
    nb = g1.shape[0]
    hb = tm // HALO
    n_halo = rows // HALO
    gate_col = _J_GATE * PROJ_TN // D_MODEL

    def bidx(i):
        return (i // tps) % nb

    def prev_blk(i):
        return jnp.maximum(i * hb - 1, 0)

    def next_blk(i):
        return jnp.minimum((i + 1) * hb, n_halo - 1)

    return pl.pallas_call(
        functools.partial(_merge_kernel, tm=tm, tps=tps),
        grid=(rows // tm,),
        in_specs=[
            pl.BlockSpec((tm, A_WIDTH), lambda i: (i, _J_U)),
            pl.BlockSpec((tm, A_WIDTH), lambda i: (i, _J_V)),
            pl.BlockSpec((tm, B_WIDTH), lambda i: (i, 0)),
            pl.BlockSpec((tm, d), lambda i: (i, gate_col)),
            pl.BlockSpec((tm, d), lambda i: (i, gate_col + 1)),
            pl.BlockSpec((tm, d), lambda i: (i, gate_col + 2)),
            pl.BlockSpec((tm, C_WIDTH), lambda i: (i, _J_CIN)),
            pl.BlockSpec((tm, C_WIDTH), lambda i: (i, _J_CIN + 1)),
            pl.BlockSpec((tm, C_WIDTH), lambda i: (i, _J_CIN + 2)),
            pl.BlockSpec((HALO, C_WIDTH), lambda i: (prev_blk(i), _J_CIN + 1)),
            pl.BlockSpec((HALO, C_WIDTH), lambda i: (prev_blk(i), _J_CIN + 2)),
            pl.BlockSpec((HALO, C_WIDTH), lambda i: (next_blk(i), _J_CIN + 1)),
            pl.BlockSpec((HALO, C_WIDTH), lambda i: (next_blk(i), _J_CIN + 2)),
            pl.BlockSpec((tm, d), lambda i: (i, 0)),
            pl.BlockSpec((None, 1, d), lambda i: (bidx(i), 0, 0)),
            _resident((A_GROUPS, CHUNK, CHUNK)),
            _resident((CHUNK, A_WIDTH)),
            _resident((3, C_WIDTH)),
            _resident((A_WIDTH, d), layer),
            _resident((B_WIDTH, d), layer),
            _resident((C_WIDTH, d), layer),
            _resident((d, d), layer),
        ],
        out_specs=pl.BlockSpec((tm, d), lambda i: (i, 0)),
        out_shape=jax.ShapeDtypeStruct((rows, d), F32),
        scratch_shapes=[pltpu.VMEM((tm, A_WIDTH), BF16), pltpu.VMEM((tm, C_WIDTH), BF16)],
        compiler_params=pltpu.CompilerParams(
            dimension_semantics=("arbitrary",), vmem_limit_bytes=VMEM_LIMIT),
        name="merge",
    )(z, z, yb, z, z, z, z, z, z, z, z, z, z, x2d, g1, sw, sb, cw, wa, wb, wc, wo)


def _ffn_kernel(x_ref, xp_ref, xn_ref, sh_ref, sc_ref, gt_ref, ng_ref,
                wu_ref, cw_ref, cb_ref, wd_ref, o_ref, h_ref, u_ref, a_ref, *, tm, tps):
    i = pl.program_id(0)

    def nm(x):
        return _norm_modulate(x, ng_ref[...], sh_ref[...], sc_ref[...])

    first = (i % tps) == 0
    last = (i % tps) == (tps - 1)
    h_ref[HALO:HALO + tm, :] = nm(x_ref[...]).astype(BF16)
    h_ref[0:HALO, :] = jnp.where(first, 0.0, nm(xp_ref[...])).astype(BF16)
    h_ref[HALO + tm:, :] = jnp.where(last, 0.0, nm(xn_ref[...])).astype(BF16)

    def conv(slot, cols):
        cw = cw_ref[:, cols]
        return (u_ref[slot, HALO - 1:HALO - 1 + tm, :] * cw[0:1] + u_ref[slot, HALO:HALO + tm, :] * cw[1:2]
                + u_ref[slot, HALO + 1:HALO + 1 + tm, :] * cw[2:3] + cb_ref[:, cols])

    for c in range(D_FF // FF_TN):
        g_cols = slice(c * FF_TN, (c + 1) * FF_TN)
        v_cols = slice(D_FF + c * FF_TN, D_FF + (c + 1) * FF_TN)
        g_slot = 2 * (c % 2)
        v_slot = g_slot + 1
        h = h_ref[...]
        u_ref[g_slot] = jnp.dot(h, wu_ref[:, g_cols], preferred_element_type=F32)
        u_ref[v_slot] = jnp.dot(h, wu_ref[:, v_cols], preferred_element_type=F32)
        g = conv(g_slot, g_cols)
        val = conv(v_slot, v_cols)
        a_ref[:, g_cols] = (g * _sigmoid(g) * val).astype(BF16)

    out = jnp.dot(a_ref[...], wd_ref[...], preferred_element_type=F32)
    o_ref[...] = x_ref[...] + gt_ref[...] * out


def _ffn(x2d, shift, scale, gate, norm_g, w_up, cw, cb, w_down, *, tm, seq_len, layer):
    rows, d = x2d.shape
    tps = seq_len // tm
    nb = shift.shape[0]
    hb = tm // HALO
    n_halo = rows // HALO

    def bidx(i):
        return (i // tps) % nb

    vec = pl.BlockSpec((None, 1, d), lambda i: (bidx(i), 0, 0))
    return pl.pallas_call(
        functools.partial(_ffn_kernel, tm=tm, tps=tps),
        grid=(rows // tm,),
        in_specs=[
            pl.BlockSpec((tm, d), lambda i: (i, 0)),
            pl.BlockSpec((HALO, d), lambda i: (jnp.maximum(i * hb - 1, 0), 0)),
            pl.BlockSpec((HALO, d), lambda i: (jnp.minimum((i + 1) * hb, n_halo - 1), 0)),
            vec, vec, vec,
            _resident((1, d)),
            _resident((d, 2 * D_FF), layer),
            _resident((3, 2 * D_FF)),
            _resident((1, 2 * D_FF)),
            _resident((D_FF, d), layer),
        ],
        out_specs=pl.BlockSpec((tm, d), lambda i: (i, 0)),
        out_shape=jax.ShapeDtypeStruct((rows, d), F32),
        scratch_shapes=[
            pltpu.VMEM((tm + 2 * HALO, d), BF16),
            pltpu.VMEM((4, tm + 2 * HALO, FF_TN), F32),
            pltpu.VMEM((tm, D_FF), BF16),
        ],
        compiler_params=pltpu.CompilerParams(
            dimension_semantics=("arbitrary",), vmem_limit_bytes=VMEM_LIMIT),
        name="ffn",
    )(x2d, x2d, x2d, shift, scale, gate, norm_g, w_up, cw, cb, w_down)


def kernel(x, c, ctx, c_ctx, ada_w, ada_b, norm1_g, norm2_g, w_in, sgu_ln_g, sgu_w, sgu_b, q_norm_g, k_norm_g, lam_q1, lam_k1, lam_q2, lam_k2, subln_g, conv_w, w_br_a, w_br_b, w_br_c, w_out, ffn_up, ffn_conv_w, ffn_conv_b, ffn_down):
    batch, seq, d = x.shape
    ctx_len = ctx.shape[1]
    depth = ada_w.shape[0]
    assert d == D_MODEL and seq % GRID_W == 0

    tm_lat = min(512, seq)
    tm_mrg = min(512, seq)
    tm_ctx = ctx_len
    tq_lat = min(2048, seq)
    tk = 512
    assert seq % tm_lat == 0 and seq % tq_lat == 0 and ctx_len % HALO == 0

    w_in_b = _z_weights(w_in)
    wv_b = w_in[:, :, _REF_B_V:_REF_B_V + B_WIDTH].astype(BF16)
    wa_b, wb_b, wc_b, wo_b = (w.astype(BF16) for w in (w_br_a, w_br_b, w_br_c, w_out))
    up_b = ffn_up.astype(BF16)
    down_b = ffn_down.astype(BF16)
    sw_b = sgu_w.astype(BF16)
    sgu_bias = jnp.repeat(jnp.swapaxes(sgu_b, 1, 2), LANES, axis=2)
    gmat = jnp.asarray(np.kron(np.eye(MXU_DIM // B_QK_DIM), np.full((B_QK_DIM, B_QK_DIM), 1.0 / B_QK_DIM)), BF16)
    cos, sin = _rope_tables(seq)
    cos_ctx = jnp.ones((tm_ctx, LANES), F32)
    sin_ctx = jnp.zeros((tm_ctx, LANES), F32)
    lam_vecs = jnp.stack([lam_q1, lam_k1, lam_q2, lam_k2], axis=1).astype(F32)
    score_cap = (1.02 * B_QK_DIM * QK_SCALE_LOG2E
                 * jnp.max(jnp.abs(q_norm_g), axis=1) * jnp.max(jnp.abs(k_norm_g), axis=1)).astype(F32)

    pad = (-(batch + 1)) % 8
    cc = jnp.concatenate([c, c_ctx[None, :], jnp.zeros((pad, d), F32)], axis=0)
    mod = _modulation(cc, ada_w, ada_b)
    mod = mod.reshape(depth, cc.shape[0], 6, 1, d)

    xs = x.reshape(batch * seq, d)
    cs = ctx.reshape(batch * ctx_len, d)

    for l in range(depth):
        last = l == depth - 1
        lam_init = 0.8 - 0.6 * math.exp(-0.3 * l)
        lat = [mod[l, :batch, k] for k in range(6)]
        cm = [mod[l, batch:batch + 1, k] for k in range(6)]
        n1 = norm1_g[l].reshape(1, d)
        n2 = norm2_g[l].reshape(1, d)
        qg = jnp.tile(q_norm_g[l], PROJ_TN // B_QK_DIM).reshape(1, PROJ_TN)
        kg = jnp.tile(k_norm_g[l], PROJ_TN // B_QK_DIM).reshape(1, PROJ_TN)
        lng = sgu_ln_g[l].reshape(1, A_WIDTH)
        subg = subln_g[l].reshape(B_V_DIM, 1)

        z_lat, vt_lat = _proj(xs, lat[0], lat[1], n1, w_in_b, wv_b, qg, kg, lng, cos, sin, gmat,
                              tm=tm_lat, seq_len=seq, use_rope=True, layer=l)
        z_ctx, vt_ctx = _proj(cs, cm[0], cm[1], n1, w_in_b, wv_b, qg, kg, lng, cos_ctx, sin_ctx, gmat,
                              tm=tm_ctx, seq_len=ctx_len, use_rope=False, layer=l, keys_values_only=last)

        yb_lat = _attention(z_lat, [(z_lat, vt_lat, seq), (z_ctx, vt_ctx, ctx_len)], score_cap, lam_vecs[l], subg,
                            batch=batch, seq_len=seq, tq=tq_lat, tk=tk, lam_init=lam_init, layer=l)
        merge_w = (sw_b[l], sgu_bias[l], conv_w[l], wa_b, wb_b, wc_b, wo_b)
        ffn_w = (up_b, ffn_conv_w[l], ffn_conv_b[l].reshape(1, 2 * D_FF), down_b)
        xs = _merge(z_lat, yb_lat, xs, lat[2], *merge_w, tm=tm_mrg, seq_len=seq, layer=l)
        xs = _ffn(xs, lat[3], lat[4], lat[5], n2, *ffn_w, tm=tm_lat, seq_len=seq, layer=l)

        if not last:
            yb_ctx = _attention(z_ctx, [(z_ctx, vt_ctx, ctx_len)], score_cap, lam_vecs[l], subg,
                                batch=batch, seq_len=ctx_len, tq=tm_ctx, tk=tk, lam_init=lam_init, layer=l,
                                heads=B_HEADS)
            cs = _merge(z_ctx, yb_ctx, cs, cm[2], *merge_w, tm=tm_ctx, seq_len=ctx_len, layer=l)
            cs = _ffn(cs, cm[3], cm[4], cm[5], n2, *ffn_w, tm=tm_ctx, seq_len=ctx_len, layer=l)

    return xs.reshape(batch, seq, d)
```

```python
import functools
import math

import jax
import jax.numpy as jnp
import numpy as np
from jax import lax
from jax.experimental import pallas as pl
from jax.experimental.pallas import tpu as pltpu

F32 = jnp.float32
BF16 = jnp.bfloat16

D_MODEL = 1024
GRID_W = 64
A_WIDTH = 512
A_GROUPS = 4
CHUNK = 128
B_HEADS = 8
B_QK_DIM = 64
B_V_DIM = 128
B_WIDTH = B_HEADS * B_V_DIM
ROPE_THETA = 10000.0
C_WIDTH = 512
N_BRANCH = 3
D_FF = 2816
EPS = 1e-6

_REF_A_U = 0
_REF_A_V = 512
_REF_B_Q = 1024
_REF_B_K = 2048
_REF_B_V = 3072
_REF_C_IN = 4096
_REF_GATE = 5632

PROJ_TN = 512
_J_U = 0
_J_V = 1
_J_Q = 2
_J_K = 4
_J_GATE = 6
_J_CIN = 12
_J_VALT = 15
_N_Z_TILES = 15
_N_PROJ_STEPS = 17
Z_COLS = _N_Z_TILES * PROJ_TN

HALO = 16
LANES = 128
SUBLANES = 8
MXU_DIM = 256
FF_TN = 256
MERGE_ROWS = 256
QK_SCALE_LOG2E = (B_QK_DIM ** -0.5) * math.log2(math.e)
SAFE_EXP2_RANGE = 60.0
VMEM_LIMIT = 56 * 1024 * 1024
_NT = (((1,), (1,)), ((), ()))


def _z_weights(w_in):
    def head_major(w):
        lead = w.shape[:-1]
        return jnp.swapaxes(w.reshape(*lead, 2, B_HEADS, B_QK_DIM), -3, -2).reshape(*lead, 2 * B_HEADS * B_QK_DIM)

    parts = [
        w_in[..., _REF_A_U:_REF_B_Q],
        head_major(w_in[..., _REF_B_Q:_REF_B_K]),
        head_major(w_in[..., _REF_B_K:_REF_B_V]),
        w_in[..., _REF_GATE:_REF_GATE + N_BRANCH * D_MODEL],
        w_in[..., _REF_C_IN:_REF_GATE],
    ]
    w = jnp.concatenate([p.astype(BF16) for p in parts], axis=-1)
    assert w.shape[-1] == Z_COLS
    return w


def _rope_tables(n_tokens):
    rows = n_tokens // GRID_W
    row = jnp.repeat(jnp.arange(rows, dtype=F32), GRID_W)
    col = jnp.tile(jnp.arange(GRID_W, dtype=F32), rows)
    n_freq = B_QK_DIM // 4
    inv = ROPE_THETA ** (-jnp.arange(n_freq, dtype=F32) / n_freq)
    ang_r = row[:, None] * inv
    ang_c = col[:, None] * inv
    cos64 = jnp.concatenate([jnp.cos(ang_r), jnp.cos(ang_r), jnp.cos(ang_c), jnp.cos(ang_c)], axis=1)
    sin64 = jnp.concatenate([-jnp.sin(ang_r), jnp.sin(ang_r), -jnp.sin(ang_c), jnp.sin(ang_c)], axis=1)
    return jnp.tile(cos64, (1, 2)), jnp.tile(sin64, (1, 2))


def _mod_kernel(cc_ref, w_ref, b_ref, o_ref):
    cc = cc_ref[...]
    s = cc * jax.nn.sigmoid(cc)
    o_ref[...] = jnp.dot(s, w_ref[...], preferred_element_type=F32,
                         precision=lax.Precision.HIGHEST) + b_ref[...]


def _modulation(cc, ada_w, ada_b):
    depth, d, n = ada_w.shape
    tn = 1536
    rows = cc.shape[0]
    return pl.pallas_call(
        _mod_kernel,
        grid=(depth, n // tn),
        in_specs=[
            pl.BlockSpec((rows, d), lambda l, j: (0, 0)),
            pl.BlockSpec((None, d, tn), lambda l, j: (l, 0, j)),
            pl.BlockSpec((None, 1, tn), lambda l, j: (l, 0, j)),
        ],
        out_specs=pl.BlockSpec((None, rows, tn), lambda l, j: (l, 0, j)),
        out_shape=jax.ShapeDtypeStruct((depth, rows, n), F32),
        compiler_params=pltpu.CompilerParams(
            dimension_semantics=("arbitrary", "arbitrary"), vmem_limit_bytes=VMEM_LIMIT),
        name="modulation",
    )(cc, ada_w, ada_b.reshape(depth, 1, n))


def _norm_modulate(x, gain, shift, scale):
    ms = jnp.mean(x * x, axis=-1, keepdims=True)
    return x * lax.rsqrt(ms + EPS) * (gain * (1.0 + scale)) + shift


def _sigmoid(x):
    return 0.5 * jnp.tanh(0.5 * x) + 0.5


def _proj_kernel(x_ref, sh_ref, sc_ref, ng_ref, w_ref, wv_ref, qg_ref, kg_ref, lng_ref,
                 cos_ref, sin_ref, gmat_ref, o_ref, vt_ref, h_ref, wvt_ref, *, tm, use_rope, z_tiles):
    @pl.when(pl.program_id(0) == 0)
    def _():
        for c in range(0, B_WIDTH, MXU_DIM):
            wvt_ref[c:c + MXU_DIM, :] = wv_ref[:, c:c + MXU_DIM].astype(F32).T.astype(BF16)

    h_ref[...] = _norm_modulate(x_ref[...], ng_ref[...], sh_ref[...], sc_ref[...]).astype(BF16)

    def qk_epilogue(acc, cols, gain, scale):
        sq = (acc * acc).astype(BF16)
        ms = jnp.concatenate(
            [jnp.dot(sq[:, c:c + MXU_DIM], gmat_ref[...], preferred_element_type=F32)
             for c in range(0, PROJ_TN, MXU_DIM)], axis=1)
        y = acc * lax.rsqrt(ms + EPS) * gain
        if not use_rope:
            o_ref[:, cols] = (y * scale).astype(BF16)
            return
        lane = lax.broadcasted_iota(jnp.int32, (tm, LANES), 1)
        first_half = (lane & 16) == 0
        cos = cos_ref[...]
        sin = sin_ref[...]
        for c in range(PROJ_TN // LANES):
            yc = y[:, c * LANES:(c + 1) * LANES]
            partner = jnp.where(first_half, pltpu.roll(yc, LANES - 16, 1), pltpu.roll(yc, 16, 1))
            lo = cols.start + c * LANES
            o_ref[:, lo:lo + LANES] = ((yc * cos + partner * sin) * scale).astype(BF16)

    for n, j in enumerate(z_tiles):
        cols = slice(n * PROJ_TN, (n + 1) * PROJ_TN)
        acc = jnp.dot(h_ref[...], w_ref[:, j * PROJ_TN:(j + 1) * PROJ_TN],
                      preferred_element_type=F32)
        if j == _J_U:
            o_ref[:, cols] = jax.nn.gelu(acc).astype(BF16)
        elif j == _J_V:
            g = jax.nn.gelu(acc)
            mu = jnp.mean(g, axis=-1, keepdims=True)
            c = g - mu
            var = jnp.mean(c * c, axis=-1, keepdims=True)
            o_ref[:, cols] = (c * lax.rsqrt(var + EPS) * lng_ref[...]).astype(BF16)
        elif j < _J_K:
            qk_epilogue(acc, cols, qg_ref[...], QK_SCALE_LOG2E)
        elif j < _J_GATE:
            qk_epilogue(acc, cols, kg_ref[...], 1.0)
        elif j < _J_CIN:
            o_ref[:, cols] = _sigmoid(acc).astype(BF16)
        else:
            o_ref[:, cols] = acc.astype(BF16)

    for r in range(B_WIDTH // PROJ_TN):
        rs = slice(r * PROJ_TN, (r + 1) * PROJ_TN)
        vt_ref[rs, :] = lax.dot_general(wvt_ref[rs, :], h_ref[...], _NT,
                                        preferred_element_type=F32).astype(BF16)


def _resident(shape, layer=None):
    if layer is None:
        return pl.BlockSpec(shape, lambda *_: (0,) * len(shape), pipeline_mode=pl.Buffered(1))
    return pl.BlockSpec((None,) + tuple(shape), lambda *_: (layer,) + (0,) * len(shape),
                        pipeline_mode=pl.Buffered(1))


def _proj(x2d, shift, scale, norm_g, w, wv, qg, kg, lng, cos, sin, gmat, *, tm, seq_len, use_rope, layer,
          keys_values_only=False):
    z_tiles = tuple(range(_J_K, _J_GATE)) if keys_values_only else tuple(range(_N_Z_TILES))
    z_cols = len(z_tiles) * PROJ_TN
    rows, d = x2d.shape
    tps = seq_len // tm
    nb = shift.shape[0]
    pos_tiles = cos.shape[0] // tm

    def bidx(i):
        return (i // tps) % nb

    return pl.pallas_call(
        functools.partial(_proj_kernel, tm=tm, use_rope=use_rope, z_tiles=z_tiles),
        grid=(rows // tm,),
        in_specs=[
            pl.BlockSpec((tm, d), lambda i: (i, 0)),
            pl.BlockSpec((None, 1, d), lambda i: (bidx(i), 0, 0)),
            pl.BlockSpec((None, 1, d), lambda i: (bidx(i), 0, 0)),
            _resident((1, d)),
            _resident((d, Z_COLS), layer),
            _resident((d, B_WIDTH), layer),
            _resident((1, PROJ_TN)),
            _resident((1, PROJ_TN)),
            _resident((1, PROJ_TN)),
            pl.BlockSpec((tm, LANES), lambda i: (i % pos_tiles, 0)),
            pl.BlockSpec((tm, LANES), lambda i: (i % pos_tiles, 0)),
            _resident((MXU_DIM, MXU_DIM)),
        ],
        out_specs=[
            pl.BlockSpec((tm, z_cols), lambda i: (i, 0)),
            pl.BlockSpec((B_WIDTH, tm), lambda i: (0, i)),
        ],
        out_shape=[
            jax.ShapeDtypeStruct((rows, z_cols), BF16),
            jax.ShapeDtypeStruct((B_WIDTH, rows), BF16),
        ],
        scratch_shapes=[pltpu.VMEM((tm, d), BF16), pltpu.VMEM((B_WIDTH, d), BF16)],
        compiler_params=pltpu.CompilerParams(
            dimension_semantics=("arbitrary",), vmem_limit_bytes=VMEM_LIMIT),
        name="proj",
    )(x2d, shift, scale, norm_g, w, wv, qg, kg, lng, cos, sin, gmat)


def _attn_kernel(*refs, tq, tk, src_rows, lam_init, layer, heads):
    n_src = len(src_rows)
    kv_refs = [(refs[1 + 2 * s], refs[2 + 2 * s]) for s in range(n_src)]
    cap_ref = refs[1 + 2 * n_src]

    cap = cap_ref[layer]
    safe = cap < SAFE_EXP2_RANGE

    @pl.when(safe)
    def _():
        for hh in range(heads):
            _attn_head(refs, hh, cap, tq=tq, tk=tk, src_rows=src_rows, lam_init=lam_init, row_max=False)

    @pl.when(jnp.logical_not(safe))
    def _():
        for hh in range(heads):
            _attn_head(refs, hh, cap, tq=tq, tk=tk, src_rows=src_rows, lam_init=lam_init, row_max=True)


def _attn_head(refs, hh, cap, *, tq, tk, src_rows, lam_init, row_max):
    n_src = len(src_rows)
    q_ref = refs[0]
    kv_refs = [(refs[1 + 2 * s], refs[2 + 2 * s]) for s in range(n_src)]
    _, lam_ref, subg_ref, o_ref, p1_ref, p2_ref = refs[1 + 2 * n_src:]
    head_lanes = slice(hh * LANES, (hh + 1) * LANES)
    head_rows = slice(hh * B_V_DIM, (hh + 1) * B_V_DIM)

    lane_k = lax.broadcasted_iota(jnp.int32, (1, LANES), 1)
    map1 = lane_k < B_QK_DIM

    q = q_ref[:, head_lanes]
    zero = jnp.zeros_like(q)
    q1 = jnp.where(map1, q, zero)
    q2 = jnp.where(map1, zero, q)

    def scores_t(qm, k_ref, start, size):
        kb = k_ref[start:start + size, head_lanes]
        return lax.dot_general(kb, qm, _NT, preferred_element_type=F32)

    src_offsets = [sum(src_rows[:s]) for s in range(n_src)]

    def blocks(fn, carry):
        for (k_ref, _), rows, offset in zip(kv_refs, src_rows, src_offsets):
            size = min(tk, rows)
            for b in range(rows // size):
                carry = fn(k_ref, b * size, size, offset + b * size, carry)
        return carry

    def fold(p, op):
        acc = p[0:SUBLANES, :]
        for r in range(1, p.shape[0] // SUBLANES):
            acc = op(acc, p[r * SUBLANES:(r + 1) * SUBLANES, :])
        return acc

    def finish(shift1, shift2):
        def step(k_ref, start, size, base, carry):
            l1, l2 = carry
            p1 = jnp.exp2(scores_t(q1, k_ref, start, size) - shift1)
            p2 = jnp.exp2(scores_t(q2, k_ref, start, size) - shift2)
            p1_ref[base:base + size, :] = p1.astype(BF16)
            p2_ref[base:base + size, :] = p2.astype(BF16)
            return l1 + fold(p1, jnp.add), l2 + fold(p2, jnp.add)

        zl = jnp.zeros((SUBLANES, tq), F32)
        l1, l2 = blocks(step, (zl, zl))

        lv = lam_ref[...]
        lam = (jnp.exp(jnp.sum(lv[0:1] * lv[1:2], axis=1, keepdims=True))
               - jnp.exp(jnp.sum(lv[2:3] * lv[3:4], axis=1, keepdims=True)) + lam_init)
        r1 = (1.0 / jnp.sum(l1, axis=0, keepdims=True)).astype(BF16)
        r2 = (lam / jnp.sum(l2, axis=0, keepdims=True)).astype(BF16)

        parts = []
        for qs in range(0, tq, MXU_DIM):
            acc = jnp.zeros((B_V_DIM, MXU_DIM), F32)
            r1q = r1[:, qs:qs + MXU_DIM]
            r2q = r2[:, qs:qs + MXU_DIM]
            for (_, vt_ref), rows, base in zip(kv_refs, src_rows, src_offsets):
                for ks in range(0, rows, MXU_DIM):
                    keys = slice(base + ks, base + ks + MXU_DIM)
                    a_tile = p1_ref[keys, qs:qs + MXU_DIM] * r1q - p2_ref[keys, qs:qs + MXU_DIM] * r2q
                    acc = acc + jnp.dot(vt_ref[head_rows, ks:ks + MXU_DIM], a_tile, preferred_element_type=F32)
            parts.append(acc)
        a = jnp.concatenate(parts, axis=1)

        ms = jnp.mean(a * a, axis=0, keepdims=True)
        y = a * lax.rsqrt(ms + EPS) * subg_ref[...] * (1.0 - lam_init)
        o_ref[:, head_lanes] = y.T.astype(BF16)

    if not row_max:
        finish(cap, cap)
        return

    def mx_step(k_ref, start, size, base, carry):
        m1, m2 = carry
        m1 = jnp.maximum(m1, fold(scores_t(q1, k_ref, start, size), jnp.maximum))
        m2 = jnp.maximum(m2, fold(scores_t(q2, k_ref, start, size), jnp.maximum))
        return m1, m2

    neg = jnp.full((SUBLANES, tq), -jnp.inf, F32)
    m1, m2 = blocks(mx_step, (neg, neg))
    finish(jnp.max(m1, axis=0, keepdims=True), jnp.max(m2, axis=0, keepdims=True))


def _attention(zq, kv_sources, score_cap, lam_vecs, subg, *, batch, seq_len, tq, tk, lam_init, layer, heads=1):
    rows = zq.shape[0]
    nq = seq_len // tq
    width = heads * LANES
    q_col = _J_Q * PROJ_TN // width

    in_specs = [pl.BlockSpec((tq, width), lambda b, h, i: (b * nq + i, q_col + h))]
    args = [zq]
    src_rows = []
    for z, vt, n in kv_sources:
        k_col = _J_K * PROJ_TN // width if z.shape[1] == Z_COLS else 0
        in_specs.append(pl.BlockSpec((n, width), lambda b, h, i, k_col=k_col: (b, k_col + h)))
        in_specs.append(pl.BlockSpec((heads * B_V_DIM, n), lambda b, h, i: (h, b)))
        args += [z, vt]
        src_rows.append(n)
    in_specs.append(pl.BlockSpec(memory_space=pltpu.SMEM))
    in_specs.append(pl.BlockSpec(lam_vecs.shape, lambda b, h, i: (0, 0)))
    in_specs.append(pl.BlockSpec((B_V_DIM, 1), lambda b, h, i: (0, 0)))
    args += [score_cap, lam_vecs, subg]

    return pl.pallas_call(
        functools.partial(_attn_kernel, tq=tq, tk=tk, src_rows=tuple(src_rows), lam_init=lam_init,
                          layer=layer, heads=heads),
        grid=(batch, B_HEADS // heads, nq),
        in_specs=in_specs,
        out_specs=pl.BlockSpec((tq, width), lambda b, h, i: (b * nq + i, h)),
        out_shape=jax.ShapeDtypeStruct((rows, B_WIDTH), BF16),
        scratch_shapes=[
            pltpu.VMEM((sum(src_rows), tq), BF16),
            pltpu.VMEM((sum(src_rows), tq), BF16),
        ],
        compiler_params=pltpu.CompilerParams(
            dimension_semantics=("arbitrary", "arbitrary", "arbitrary"), vmem_limit_bytes=VMEM_LIMIT),
        name="attn",
    )(*args)


def _merge_kernel(u_ref, v_ref, yb_ref, ga_ref, gb_ref, gc_ref, cb_ref, cg_ref, cx_ref,
                  pg_ref, px_ref, ng_ref, nx_ref, x_ref, g1_ref,
                  sw_ref, sb_ref, cw_ref, wa_ref, wb_ref, wc_ref, wo_ref,
                  o_ref, ya_ref, yc_ref, *, tm, tps):
    i = pl.program_id(0)
    first = (i % tps) == 0
    last = (i % tps) == (tps - 1)

    for c in range(tm // CHUNK):
        rs = slice(c * CHUNK, (c + 1) * CHUNK)
        for g in range(A_GROUPS):
            cs = slice(g * LANES, (g + 1) * LANES)
            mixed = jnp.dot(sw_ref[g], v_ref[rs, cs], preferred_element_type=F32) + sb_ref[:, cs]
            ya_ref[rs, cs] = (u_ref[rs, cs].astype(F32) * mixed).astype(BF16)

    t = cg_ref[...].astype(F32) * cx_ref[...].astype(F32)
    t_prev_row = pg_ref[HALO - 1:HALO, :].astype(F32) * px_ref[HALO - 1:HALO, :].astype(F32)
    t_next_row = ng_ref[0:1, :].astype(F32) * nx_ref[0:1, :].astype(F32)
    t_prev_row = jnp.where(first, 0.0, t_prev_row)
    t_next_row = jnp.where(last, 0.0, t_next_row)
    row = lax.broadcasted_iota(jnp.int32, (tm, C_WIDTH), 0)
    t_prev = jnp.where(row == 0, t_prev_row, pltpu.roll(t, 1, 0))
    t_next = jnp.where(row == tm - 1, t_next_row, pltpu.roll(t, tm - 1, 0))
    cw = cw_ref[...]
    y_c = cb_ref[...].astype(F32) * (t_prev * cw[0:1] + t * cw[1:2] + t_next * cw[2:3])
    yc_ref[...] = y_c.astype(BF16)

    for r in range(0, tm, MERGE_ROWS):
        rows = slice(r, r + MERGE_ROWS)
        m = ga_ref[rows, :].astype(F32) * jnp.dot(ya_ref[rows, :], wa_ref[...], preferred_element_type=F32)
        m = m + gb_ref[rows, :].astype(F32) * jnp.dot(yb_ref[rows, :], wb_ref[...], preferred_element_type=F32)
        m = m + gc_ref[rows, :].astype(F32) * jnp.dot(yc_ref[rows, :], wc_ref[...], preferred_element_type=F32)
        out = jnp.dot(m.astype(BF16), wo_ref[...], preferred_element_type=F32)
        o_ref[rows, :] = x_ref[rows, :] + g1_ref[...] * out


def _merge(z, yb, x2d, g1, sw, sb, cw, wa, wb, wc, wo, *, tm, seq_len, layer):
    rows, d = x2d.shape
    tps = seq_len // tm
    nb = g1.shape[0]
    hb = tm // HALO
    n_halo = rows // HALO
    gate_col = _J_GATE * PROJ_TN // D_MODEL

    def bidx(i):
        return (i // tps) % nb

    def prev_blk(i):
        return jnp.maximum(i * hb - 1, 0)

    def next_blk(i):
        return jnp.minimum((i + 1) * hb, n_halo - 1)

    return pl.pallas_call(
        functools.partial(_merge_kernel, tm=tm, tps=tps),
        grid=(rows // tm,),
        in_specs=[
            pl.BlockSpec((tm, A_WIDTH), lambda i: (i, _J_U)),
            pl.BlockSpec((tm, A_WIDTH), lambda i: (i, _J_V)),
            pl.BlockSpec((tm, B_WIDTH), lambda i: (i, 0)),
            pl.BlockSpec((tm, d), lambda i: (i, gate_col)),
            pl.BlockSpec((tm, d), lambda i: (i, gate_col + 1)),
            pl.BlockSpec((tm, d), lambda i: (i, gate_col + 2)),
            pl.BlockSpec((tm, C_WIDTH), lambda i: (i, _J_CIN)),
            pl.BlockSpec((tm, C_WIDTH), lambda i: (i, _J_CIN + 1)),
            pl.BlockSpec((tm, C_WIDTH), lambda i: (i, _J_CIN + 2)),
            pl.BlockSpec((HALO, C_WIDTH), lambda i: (prev_blk(i), _J_CIN + 1)),
            pl.BlockSpec((HALO, C_WIDTH), lambda i: (prev_blk(i), _J_CIN + 2)),
            pl.BlockSpec((HALO, C_WIDTH), lambda i: (next_blk(i), _J_CIN + 1)),
            pl.BlockSpec((HALO, C_WIDTH), lambda i: (next_blk(i), _J_CIN + 2)),
            pl.BlockSpec((tm, d), lambda i: (i, 0)),
            pl.BlockSpec((None, 1, d), lambda i: (bidx(i), 0, 0)),
            _resident((A_GROUPS, CHUNK, CHUNK)),
            _resident((CHUNK, A_WIDTH)),
            _resident((3, C_WIDTH)),
            _resident((A_WIDTH, d), layer),
            _resident((B_WIDTH, d), layer),
            _resident((C_WIDTH, d), layer),
            _resident((d, d), layer),
        ],
        out_specs=pl.BlockSpec((tm, d), lambda i: (i, 0)),
        out_shape=jax.ShapeDtypeStruct((rows, d), F32),
        scratch_shapes=[pltpu.VMEM((tm, A_WIDTH), BF16), pltpu.VMEM((tm, C_WIDTH), BF16)],
        compiler_params=pltpu.CompilerParams(
            dimension_semantics=("arbitrary",), vmem_limit_bytes=VMEM_LIMIT),
        name="merge",
    )(z, z, yb, z, z, z, z, z, z, z, z, z, z, x2d, g1, sw, sb, cw, wa, wb, wc, wo)


def _ffn_kernel(x_ref, xp_ref, xn_ref, sh_ref, sc_ref, gt_ref, ng_ref,
                wu_ref, cw_ref, cb_ref, wd_ref, o_ref, h_ref, u_ref, a_ref, *, tm, tps):
    i = pl.program_id(0)

    def nm(x):
        return _norm_modulate(x, ng_ref[...], sh_ref[...], sc_ref[...])

    first = (i % tps) == 0
    last = (i % tps) == (tps - 1)
    h_ref[HALO:HALO + tm, :] = nm(x_ref[...]).astype(BF16)
    h_ref[0:HALO, :] = jnp.where(first, 0.0, nm(xp_ref[...])).astype(BF16)
    h_ref[HALO + tm:, :] = jnp.where(last, 0.0, nm(xn_ref[...])).astype(BF16)

    def conv(slot, cols):
        cw = cw_ref[:, cols]
        return (u_ref[slot, HALO - 1:HALO - 1 + tm, :] * cw[0:1] + u_ref[slot, HALO:HALO + tm, :] * cw[1:2]
                + u_ref[slot, HALO + 1:HALO + 1 + tm, :] * cw[2:3] + cb_ref[:, cols])

    for c in range(D_FF // FF_TN):
        g_cols = slice(c * FF_TN, (c + 1) * FF_TN)
        v_cols = slice(D_FF + c * FF_TN, D_FF + (c + 1) * FF_TN)
        g_slot = 2 * (c % 2)
        v_slot = g_slot + 1
        h = h_ref[...]
        u_ref[g_slot] = jnp.dot(h, wu_ref[:, g_cols], preferred_element_type=F32)
        u_ref[v_slot] = jnp.dot(h, wu_ref[:, v_cols], preferred_element_type=F32)
        g = conv(g_slot, g_cols)
        val = conv(v_slot, v_cols)
        a_ref[:, g_cols] = (g * _sigmoid(g) * val).astype(BF16)

    out = jnp.dot(a_ref[...], wd_ref[...], preferred_element_type=F32)
    o_ref[...] = x_ref[...] + gt_ref[...] * out


def _ffn(x2d, shift, scale, gate, norm_g, w_up, cw, cb, w_down, *, tm, seq_len, layer):
    rows, d = x2d.shape
    tps = seq_len // tm
    nb = shift.shape[0]
    hb = tm // HALO
    n_halo = rows // HALO

    def bidx(i):
        return (i // tps) % nb

    vec = pl.BlockSpec((None, 1, d), lambda i: (bidx(i), 0, 0))
    return pl.pallas_call(
        functools.partial(_ffn_kernel, tm=tm, tps=tps),
        grid=(rows // tm,),
        in_specs=[
            pl.BlockSpec((tm, d), lambda i: (i, 0)),
            pl.BlockSpec((HALO, d), lambda i: (jnp.maximum(i * hb - 1, 0), 0)),
            pl.BlockSpec((HALO, d), lambda i: (jnp.minimum((i + 1) * hb, n_halo - 1), 0)),
            vec, vec, vec,
            _resident((1, d)),
            _resident((d, 2 * D_FF), layer),
            _resident((3, 2 * D_FF)),
            _resident((1, 2 * D_FF)),
            _resident((D_FF, d), layer),
        ],
        out_specs=pl.BlockSpec((tm, d), lambda i: (i, 0)),
        out_shape=jax.ShapeDtypeStruct((rows, d), F32),
        scratch_shapes=[
            pltpu.VMEM((tm + 2 * HALO, d), BF16),
            pltpu.VMEM((4, tm + 2 * HALO, FF_TN), F32),
            pltpu.VMEM((tm, D_FF), BF16),
        ],
        compiler_params=pltpu.CompilerParams(
            dimension_semantics=("arbitrary",), vmem_limit_bytes=VMEM_LIMIT),
        name="ffn",
    )(x2d, x2d, x2d, shift, scale, gate, norm_g, w_up, cw, cb, w_down)


def kernel(x, c, ctx, c_ctx, ada_w, ada_b, norm1_g, norm2_g, w_in, sgu_ln_g, sgu_w, sgu_b, q_norm_g, k_norm_g, lam_q1, lam_k1, lam_q2, lam_k2, subln_g, conv_w, w_br_a, w_br_b, w_br_c, w_out, ffn_up, ffn_conv_w, ffn_conv_b, ffn_down):
    batch, seq, d = x.shape
    ctx_len = ctx.shape[1]
    depth = ada_w.shape[0]
    assert d == D_MODEL and seq % GRID_W == 0

    tm_lat = min(512, seq)
    tm_ffn = min(1024, seq)
    tm_mrg = min(512, seq)
    tm_ctx = ctx_len
    tq_lat = min(2048, seq)
    tk = 512
    assert seq % tm_lat == 0 and seq % tq_lat == 0 and ctx_len % HALO == 0

    w_in_b = _z_weights(w_in)
    wv_b = w_in[:, :, _REF_B_V:_REF_B_V + B_WIDTH].astype(BF16)
    wa_b, wb_b, wc_b, wo_b = (w.astype(BF16) for w in (w_br_a, w_br_b, w_br_c, w_out))
    up_b = ffn_up.astype(BF16)
    down_b = ffn_down.astype(BF16)
    sw_b = sgu_w.astype(BF16)
    sgu_bias = jnp.repeat(jnp.swapaxes(sgu_b, 1, 2), LANES, axis=2)
    gmat = jnp.asarray(np.kron(np.eye(MXU_DIM // B_QK_DIM), np.full((B_QK_DIM, B_QK_DIM), 1.0 / B_QK_DIM)), BF16)
    cos, sin = _rope_tables(seq)
    cos_ctx = jnp.ones((tm_ctx, LANES), F32)
    sin_ctx = jnp.zeros((tm_ctx, LANES), F32)
    lam_vecs = jnp.stack([lam_q1, lam_k1, lam_q2, lam_k2], axis=1).astype(F32)
    score_cap = (1.02 * B_QK_DIM * QK_SCALE_LOG2E
                 * jnp.max(jnp.abs(q_norm_g), axis=1) * jnp.max(jnp.abs(k_norm_g), axis=1)).astype(F32)

    pad = (-(batch + 1)) % 8
    cc = jnp.concatenate([c, c_ctx[None, :], jnp.zeros((pad, d), F32)], axis=0)
    mod = _modulation(cc, ada_w, ada_b)
    mod = mod.reshape(depth, cc.shape[0], 6, 1, d)

    xs = x.reshape(batch * seq, d)
    cs = ctx.reshape(batch * ctx_len, d)

    for l in range(depth):
        last = l == depth - 1
        lam_init = 0.8 - 0.6 * math.exp(-0.3 * l)
        lat = [mod[l, :batch, k] for k in range(6)]
        cm = [mod[l, batch:batch + 1, k] for k in range(6)]
        n1 = norm1_g[l].reshape(1, d)
        n2 = norm2_g[l].reshape(1, d)
        qg = jnp.tile(q_norm_g[l], PROJ_TN // B_QK_DIM).reshape(1, PROJ_TN)
        kg = jnp.tile(k_norm_g[l], PROJ_TN // B_QK_DIM).reshape(1, PROJ_TN)
        lng = sgu_ln_g[l].reshape(1, A_WIDTH)
        subg = subln_g[l].reshape(B_V_DIM, 1)

        z_lat, vt_lat = _proj(xs, lat[0], lat[1], n1, w_in_b, wv_b, qg, kg, lng, cos, sin, gmat,
                              tm=tm_lat, seq_len=seq, use_rope=True, layer=l)
        z_ctx, vt_ctx = _proj(cs, cm[0], cm[1], n1, w_in_b, wv_b, qg, kg, lng, cos_ctx, sin_ctx, gmat,
                              tm=tm_ctx, seq_len=ctx_len, use_rope=False, layer=l, keys_values_only=last)

        yb_lat = _attention(z_lat, [(z_lat, vt_lat, seq), (z_ctx, vt_ctx, ctx_len)], score_cap, lam_vecs[l], subg,
                            batch=batch, seq_len=seq, tq=tq_lat, tk=tk, lam_init=lam_init, layer=l)
        merge_w = (sw_b[l], sgu_bias[l], conv_w[l], wa_b, wb_b, wc_b, wo_b)
        ffn_w = (up_b, ffn_conv_w[l], ffn_conv_b[l].reshape(1, 2 * D_FF), down_b)
        xs = _merge(z_lat, yb_lat, xs, lat[2], *merge_w, tm=tm_mrg, seq_len=seq, layer=l)
        xs = _ffn(xs, lat[3], lat[4], lat[5], n2, *ffn_w, tm=tm_ffn, seq_len=seq, layer=l)

        if not last:
            yb_ctx = _attention(z_ctx, [(z_ctx, vt_ctx, ctx_len)], score_cap, lam_vecs[l], subg,
                                batch=batch, seq_len=ctx_len, tq=tm_ctx, tk=tk, lam_init=lam_init, layer=l,
                                heads=B_HEADS)
            cs = _merge(z_ctx, yb_ctx, cs, cm[2], *merge_w, tm=tm_ctx, seq_len=ctx_len, layer=l)
            cs = _ffn(cs, cm[3], cm[4], cm[5], n2, *ffn_w, tm=tm_ctx, seq_len=ctx_len, layer=l)

    return xs.reshape(batch, seq, d)
```

```python
import functools
import math

import jax
import jax.numpy as jnp
import numpy as np
from jax import lax
from jax.experimental import pallas as pl
from jax.experimental.pallas import tpu as pltpu

F32 = jnp.float32
BF16 = jnp.bfloat16

D_MODEL = 1024
GRID_W = 64
A_WIDTH = 512
A_GROUPS = 4
CHUNK = 128
B_HEADS = 8
B_QK_DIM = 64
B_V_DIM = 128
B_WIDTH = B_HEADS * B_V_DIM
ROPE_THETA = 10000.0
C_WIDTH = 512
N_BRANCH = 3
D_FF = 2816
EPS = 1e-6

_REF_A_U = 0
_REF_A_V = 512
_REF_B_Q = 1024
_REF_B_K = 2048
_REF_B_V = 3072
_REF_C_IN = 4096
_REF_GATE = 5632

PROJ_TN = 512
_J_U = 0
_J_V = 1
_J_Q = 2
_J_K = 4
_J_GATE = 6
_J_CIN = 12
_N_Z_TILES = 15
Z_COLS = _N_Z_TILES * PROJ_TN

LANES = 128
SUBLANES = 8
MXU_DIM = 256
VMEM_BYTES = 64 * 1024 * 1024
VMEM_LIMIT = VMEM_BYTES - 8 * 1024 * 1024

HALO = 16
FF_TN = 256
MERGE_ROWS = 256
QK_SCALE_LOG2E = (B_QK_DIM ** -0.5) * math.log2(math.e)
SAFE_EXP2_RANGE = 60.0
_NT = (((1,), (1,)), ((), ()))


def _z_weights(w_in):
    def head_major(w):
        lead = w.shape[:-1]
        return jnp.swapaxes(w.reshape(*lead, 2, B_HEADS, B_QK_DIM), -3, -2).reshape(*lead, 2 * B_HEADS * B_QK_DIM)

    parts = [
        w_in[..., _REF_A_U:_REF_B_Q],
        head_major(w_in[..., _REF_B_Q:_REF_B_K]),
        head_major(w_in[..., _REF_B_K:_REF_B_V]),
        w_in[..., _REF_GATE:_REF_GATE + N_BRANCH * D_MODEL],
        w_in[..., _REF_C_IN:_REF_GATE],
    ]
    w = jnp.concatenate([p.astype(BF16) for p in parts], axis=-1)
    assert w.shape[-1] == Z_COLS
    return w


def _rope_tables(n_tokens):
    rows = n_tokens // GRID_W
    row = jnp.repeat(jnp.arange(rows, dtype=F32), GRID_W)
    col = jnp.tile(jnp.arange(GRID_W, dtype=F32), rows)
    n_freq = B_QK_DIM // 4
    inv = ROPE_THETA ** (-jnp.arange(n_freq, dtype=F32) / n_freq)
    ang_r = row[:, None] * inv
    ang_c = col[:, None] * inv
    cos64 = jnp.concatenate([jnp.cos(ang_r), jnp.cos(ang_r), jnp.cos(ang_c), jnp.cos(ang_c)], axis=1)
    sin64 = jnp.concatenate([-jnp.sin(ang_r), jnp.sin(ang_r), -jnp.sin(ang_c), jnp.sin(ang_c)], axis=1)
    return jnp.tile(cos64, (1, 2)), jnp.tile(sin64, (1, 2))


def _mod_kernel(cc_ref, w_ref, b_ref, o_ref):
    cc = cc_ref[...]
    s = cc * jax.nn.sigmoid(cc)
    o_ref[...] = jnp.dot(s, w_ref[...], preferred_element_type=F32,
                         precision=lax.Precision.HIGHEST) + b_ref[...]


def _modulation(cc, ada_w, ada_b):
    depth, d, n = ada_w.shape
    tn = 1536
    rows = cc.shape[0]
    return pl.pallas_call(
        _mod_kernel,
        grid=(depth, n // tn),
        in_specs=[
            pl.BlockSpec((rows, d), lambda l, j: (0, 0)),
            pl.BlockSpec((None, d, tn), lambda l, j: (l, 0, j)),
            pl.BlockSpec((None, 1, tn), lambda l, j: (l, 0, j)),
        ],
        out_specs=pl.BlockSpec((None, rows, tn), lambda l, j: (l, 0, j)),
        out_shape=jax.ShapeDtypeStruct((depth, rows, n), F32),
        compiler_params=pltpu.CompilerParams(
            dimension_semantics=("arbitrary", "arbitrary"), vmem_limit_bytes=VMEM_LIMIT),
        name="modulation",
    )(cc, ada_w, ada_b.reshape(depth, 1, n))


def _norm_modulate(x, gain, shift, scale):
    ms = jnp.mean(x * x, axis=-1, keepdims=True)
    return x * lax.rsqrt(ms + EPS) * (gain * (1.0 + scale)) + shift


def _sigmoid(x):
    return 0.5 * jnp.tanh(0.5 * x) + 0.5


def _proj_kernel(x_ref, sh_ref, sc_ref, ng_ref, w_ref, wv_ref, qg_ref, kg_ref, lng_ref,
                 cos_ref, sin_ref, gmat_ref, o_ref, vt_ref, h_ref, wvt_ref, *, tm, use_rope, z_tiles):
    @pl.when(pl.program_id(0) == 0)
    def _():
        for c in range(0, B_WIDTH, MXU_DIM):
            wvt_ref[c:c + MXU_DIM, :] = wv_ref[:, c:c + MXU_DIM].astype(F32).T.astype(BF16)

    h_ref[...] = _norm_modulate(x_ref[...], ng_ref[...], sh_ref[...], sc_ref[...]).astype(BF16)

    def qk_epilogue(acc, cols, gain, scale):
        sq = (acc * acc).astype(BF16)
        ms = jnp.concatenate(
            [jnp.dot(sq[:, c:c + MXU_DIM], gmat_ref[...], preferred_element_type=F32)
             for c in range(0, PROJ_TN, MXU_DIM)], axis=1)
        y = acc * lax.rsqrt(ms + EPS) * gain
        if not use_rope:
            o_ref[:, cols] = (y * scale).astype(BF16)
            return
        lane = lax.broadcasted_iota(jnp.int32, (tm, LANES), 1)
        first_half = (lane & 16) == 0
        cos = cos_ref[...]
        sin = sin_ref[...]
        for c in range(PROJ_TN // LANES):
            yc = y[:, c * LANES:(c + 1) * LANES]
            partner = jnp.where(first_half, pltpu.roll(yc, LANES - 16, 1), pltpu.roll(yc, 16, 1))
            lo = cols.start + c * LANES
            o_ref[:, lo:lo + LANES] = ((yc * cos + partner * sin) * scale).astype(BF16)

    for n, j in enumerate(z_tiles):
        cols = slice(n * PROJ_TN, (n + 1) * PROJ_TN)
        acc = jnp.dot(h_ref[...], w_ref[:, j * PROJ_TN:(j + 1) * PROJ_TN],
                      preferred_element_type=F32)
        if j == _J_U:
            o_ref[:, cols] = jax.nn.gelu(acc).astype(BF16)
        elif j == _J_V:
            g = jax.nn.gelu(acc)
            mu = jnp.mean(g, axis=-1, keepdims=True)
            c = g - mu
            var = jnp.mean(c * c, axis=-1, keepdims=True)
            o_ref[:, cols] = (c * lax.rsqrt(var + EPS) * lng_ref[...]).astype(BF16)
        elif j < _J_K:
            qk_epilogue(acc, cols, qg_ref[...], QK_SCALE_LOG2E)
        elif j < _J_GATE:
            qk_epilogue(acc, cols, kg_ref[...], 1.0)
        elif j < _J_CIN:
            o_ref[:, cols] = _sigmoid(acc).astype(BF16)
        else:
            o_ref[:, cols] = acc.astype(BF16)

    for r in range(B_WIDTH // PROJ_TN):
        rs = slice(r * PROJ_TN, (r + 1) * PROJ_TN)
        vt_ref[rs, :] = lax.dot_general(wvt_ref[rs, :], h_ref[...], _NT,
                                        preferred_element_type=F32).astype(BF16)


def _resident(shape, layer=None):
    if layer is None:
        return pl.BlockSpec(shape, lambda *_: (0,) * len(shape), pipeline_mode=pl.Buffered(1))
    return pl.BlockSpec((None,) + tuple(shape), lambda *_: (layer,) + (0,) * len(shape),
                        pipeline_mode=pl.Buffered(1))


def _proj(x2d, shift, scale, norm_g, w, wv, qg, kg, lng, cos, sin, gmat, *, tm, seq_len, use_rope, layer,
          keys_values_only=False):
    z_tiles = tuple(range(_J_K, _J_GATE)) if keys_values_only else tuple(range(_N_Z_TILES))
    z_cols = len(z_tiles) * PROJ_TN
    rows, d = x2d.shape
    tps = seq_len // tm
    nb = shift.shape[0]
    pos_tiles = cos.shape[0] // tm

    def bidx(i):
        return (i // tps) % nb

    return pl.pallas_call(
        functools.partial(_proj_kernel, tm=tm, use_rope=use_rope, z_tiles=z_tiles),
        grid=(rows // tm,),
        in_specs=[
            pl.BlockSpec((tm, d), lambda i: (i, 0)),
            pl.BlockSpec((None, 1, d), lambda i: (bidx(i), 0, 0)),
            pl.BlockSpec((None, 1, d), lambda i: (bidx(i), 0, 0)),
            _resident((1, d)),
            _resident((d, Z_COLS), layer),
            _resident((d, B_WIDTH), layer),
            _resident((1, PROJ_TN)),
            _resident((1, PROJ_TN)),
            _resident((1, PROJ_TN)),
            pl.BlockSpec((tm, LANES), lambda i: (i % pos_tiles, 0)),
            pl.BlockSpec((tm, LANES), lambda i: (i % pos_tiles, 0)),
            _resident((MXU_DIM, MXU_DIM)),
        ],
        out_specs=[
            pl.BlockSpec((tm, z_cols), lambda i: (i, 0)),
            pl.BlockSpec((B_WIDTH, tm), lambda i: (0, i)),
        ],
        out_shape=[
            jax.ShapeDtypeStruct((rows, z_cols), BF16),
            jax.ShapeDtypeStruct((B_WIDTH, rows), BF16),
        ],
        scratch_shapes=[pltpu.VMEM((tm, d), BF16), pltpu.VMEM((B_WIDTH, d), BF16)],
        compiler_params=pltpu.CompilerParams(
            dimension_semantics=("arbitrary",), vmem_limit_bytes=VMEM_LIMIT),
        name="proj",
    )(x2d, shift, scale, norm_g, w, wv, qg, kg, lng, cos, sin, gmat)


def _attn_kernel(*refs, tq, tk, src_rows, lam_init, layer, heads):
    n_src = len(src_rows)
    kv_refs = [(refs[1 + 2 * s], refs[2 + 2 * s]) for s in range(n_src)]
    cap_ref = refs[1 + 2 * n_src]

    cap = cap_ref[layer]
    safe = cap < SAFE_EXP2_RANGE

    @pl.when(safe)
    def _():
        for hh in range(heads):
            _attn_head(refs, hh, cap, tq=tq, tk=tk, src_rows=src_rows, lam_init=lam_init, row_max=False)

    @pl.when(jnp.logical_not(safe))
    def _():
        for hh in range(heads):
            _attn_head(refs, hh, cap, tq=tq, tk=tk, src_rows=src_rows, lam_init=lam_init, row_max=True)


def _attn_head(refs, hh, cap, *, tq, tk, src_rows, lam_init, row_max):
    n_src = len(src_rows)
    q_ref = refs[0]
    kv_refs = [(refs[1 + 2 * s], refs[2 + 2 * s]) for s in range(n_src)]
    _, lam_ref, subg_ref, o_ref, p1_ref, p2_ref = refs[1 + 2 * n_src:]
    head_lanes = slice(hh * LANES, (hh + 1) * LANES)
    head_rows = slice(hh * B_V_DIM, (hh + 1) * B_V_DIM)

    lane_k = lax.broadcasted_iota(jnp.int32, (1, LANES), 1)
    map1 = lane_k < B_QK_DIM

    q = q_ref[:, head_lanes]
    zero = jnp.zeros_like(q)
    q1 = jnp.where(map1, q, zero)
    q2 = jnp.where(map1, zero, q)

    def scores_t(qm, k_ref, start, size):
        kb = k_ref[start:start + size, head_lanes]
        return lax.dot_general(kb, qm, _NT, preferred_element_type=F32)

    src_offsets = [sum(src_rows[:s]) for s in range(n_src)]

    def blocks(fn, carry):
        for (k_ref, _), rows, offset in zip(kv_refs, src_rows, src_offsets):
            size = min(tk, rows)
            for b in range(rows // size):
                carry = fn(k_ref, b * size, size, offset + b * size, carry)
        return carry

    def fold(p, op):
        acc = p[0:SUBLANES, :]
        for r in range(1, p.shape[0] // SUBLANES):
            acc = op(acc, p[r * SUBLANES:(r + 1) * SUBLANES, :])
        return acc

    def finish(shift1, shift2):
        def step(k_ref, start, size, base, carry):
            l1, l2 = carry
            p1 = jnp.exp2(scores_t(q1, k_ref, start, size) - shift1)
            p2 = jnp.exp2(scores_t(q2, k_ref, start, size) - shift2)
            p1_ref[base:base + size, :] = p1.astype(BF16)
            p2_ref[base:base + size, :] = p2.astype(BF16)
            return l1 + fold(p1, jnp.add), l2 + fold(p2, jnp.add)

        zl = jnp.zeros((SUBLANES, tq), F32)
        l1, l2 = blocks(step, (zl, zl))

        lv = lam_ref[...]
        lam = (jnp.exp(jnp.sum(lv[0:1] * lv[1:2], axis=1, keepdims=True))
               - jnp.exp(jnp.sum(lv[2:3] * lv[3:4], axis=1, keepdims=True)) + lam_init)
        r1 = (1.0 / jnp.sum(l1, axis=0, keepdims=True)).astype(BF16)
        r2 = (lam / jnp.sum(l2, axis=0, keepdims=True)).astype(BF16)

        parts = []
        for qs in range(0, tq, MXU_DIM):
            acc = jnp.zeros((B_V_DIM, MXU_DIM), F32)
            r1q = r1[:, qs:qs + MXU_DIM]
            r2q = r2[:, qs:qs + MXU_DIM]
            for (_, vt_ref), rows, base in zip(kv_refs, src_rows, src_offsets):
                for ks in range(0, rows, MXU_DIM):
                    keys = slice(base + ks, base + ks + MXU_DIM)
                    a_tile = p1_ref[keys, qs:qs + MXU_DIM] * r1q - p2_ref[keys, qs:qs + MXU_DIM] * r2q
                    acc = acc + jnp.dot(vt_ref[head_rows, ks:ks + MXU_DIM], a_tile, preferred_element_type=F32)
            parts.append(acc)
        a = jnp.concatenate(parts, axis=1)

        ms = jnp.mean(a * a, axis=0, keepdims=True)
        y = a * lax.rsqrt(ms + EPS) * subg_ref[...] * (1.0 - lam_init)
        o_ref[:, head_lanes] = y.T.astype(BF16)

    if not row_max:
        finish(cap, cap)
        return

    def mx_step(k_ref, start, size, base, carry):
        m1, m2 = carry
        m1 = jnp.maximum(m1, fold(scores_t(q1, k_ref, start, size), jnp.maximum))
        m2 = jnp.maximum(m2, fold(scores_t(q2, k_ref, start, size), jnp.maximum))
        return m1, m2

    neg = jnp.full((SUBLANES, tq), -jnp.inf, F32)
    m1, m2 = blocks(mx_step, (neg, neg))
    finish(jnp.max(m1, axis=0, keepdims=True), jnp.max(m2, axis=0, keepdims=True))


def _attention(zq, kv_sources, score_cap, lam_vecs, subg, *, batch, seq_len, tq, tk, lam_init, layer, heads=1):
    rows = zq.shape[0]
    nq = seq_len // tq
    width = heads * LANES
    q_col = _J_Q * PROJ_TN // width

    in_specs = [pl.BlockSpec((tq, width), lambda b, h, i: (b * nq + i, q_col + h))]
    args = [zq]
    src_rows = []
    for z, vt, n in kv_sources:
        k_col = _J_K * PROJ_TN // width if z.shape[1] == Z_COLS else 0
        in_specs.append(pl.BlockSpec((n, width), lambda b, h, i, k_col=k_col: (b, k_col + h)))
        in_specs.append(pl.BlockSpec((heads * B_V_DIM, n), lambda b, h, i: (h, b)))
        args += [z, vt]
        src_rows.append(n)
    in_specs.append(pl.BlockSpec(memory_space=pltpu.SMEM))
    in_specs.append(pl.BlockSpec(lam_vecs.shape, lambda b, h, i: (0, 0)))
    in_specs.append(pl.BlockSpec((B_V_DIM, 1), lambda b, h, i: (0, 0)))
    args += [score_cap, lam_vecs, subg]

    return pl.pallas_call(
        functools.partial(_attn_kernel, tq=tq, tk=tk, src_rows=tuple(src_rows), lam_init=lam_init,
                          layer=layer, heads=heads),
        grid=(batch, B_HEADS // heads, nq),
        in_specs=in_specs,
        out_specs=pl.BlockSpec((tq, width), lambda b, h, i: (b * nq + i, h)),
        out_shape=jax.ShapeDtypeStruct((rows, B_WIDTH), BF16),
        scratch_shapes=[
            pltpu.VMEM((sum(src_rows), tq), BF16),
            pltpu.VMEM((sum(src_rows), tq), BF16),
        ],
        compiler_params=pltpu.CompilerParams(
            dimension_semantics=("arbitrary", "arbitrary", "arbitrary"), vmem_limit_bytes=VMEM_LIMIT),
        name="attn",
    )(*args)


def _merge_kernel(u_ref, v_ref, yb_ref, ga_ref, gb_ref, gc_ref, cb_ref, cg_ref, cx_ref,
                  pg_ref, px_ref, ng_ref, nx_ref, x_ref, g1_ref,
                  sw_ref, sb_ref, cw_ref, wa_ref, wb_ref, wc_ref, wo_ref,
                  o_ref, ya_ref, yc_ref, *, tm, tps):
    i = pl.program_id(0)
    first = (i % tps) == 0
    last = (i % tps) == (tps - 1)

    for c in range(tm // CHUNK):
        rs = slice(c * CHUNK, (c + 1) * CHUNK)
        for g in range(A_GROUPS):
            cs = slice(g * LANES, (g + 1) * LANES)
            mixed = jnp.dot(sw_ref[g], v_ref[rs, cs], preferred_element_type=F32) + sb_ref[:, cs]
            ya_ref[rs, cs] = (u_ref[rs, cs].astype(F32) * mixed).astype(BF16)

    t = cg_ref[...].astype(F32) * cx_ref[...].astype(F32)
    t_prev_row = pg_ref[HALO - 1:HALO, :].astype(F32) * px_ref[HALO - 1:HALO, :].astype(F32)
    t_next_row = ng_ref[0:1, :].astype(F32) * nx_ref[0:1, :].astype(F32)
    t_prev_row = jnp.where(first, 0.0, t_prev_row)
    t_next_row = jnp.where(last, 0.0, t_next_row)
    row = lax.broadcasted_iota(jnp.int32, (tm, C_WIDTH), 0)
    t_prev = jnp.where(row == 0, t_prev_row, pltpu.roll(t, 1, 0))
    t_next = jnp.where(row == tm - 1, t_next_row, pltpu.roll(t, tm - 1, 0))
    cw = cw_ref[...]
    y_c = cb_ref[...].astype(F32) * (t_prev * cw[0:1] + t * cw[1:2] + t_next * cw[2:3])
    yc_ref[...] = y_c.astype(BF16)

    for r in range(0, tm, MERGE_ROWS):
        rows = slice(r, r + MERGE_ROWS)
        m = ga_ref[rows, :].astype(F32) * jnp.dot(ya_ref[rows, :], wa_ref[...], preferred_element_type=F32)
        m = m + gb_ref[rows, :].astype(F32) * jnp.dot(yb_ref[rows, :], wb_ref[...], preferred_element_type=F32)
        m = m + gc_ref[rows, :].astype(F32) * jnp.dot(yc_ref[rows, :], wc_ref[...], preferred_element_type=F32)
        out = jnp.dot(m.astype(BF16), wo_ref[...], preferred_element_type=F32)
        o_ref[rows, :] = x_ref[rows, :] + g1_ref[...] * out


def _merge(z, yb, x2d, g1, sw, sb, cw, wa, wb, wc, wo, *, tm, seq_len, layer):
    rows, d = x2d.shape
    tps = seq_len // tm
    nb = g1.shape[0]
    hb = tm // HALO
    n_halo = rows // HALO
    gate_col = _J_GATE * PROJ_TN // D_MODEL

    def bidx(i):
        return (i // tps) % nb

    def prev_blk(i):
        return jnp.maximum(i * hb - 1, 0)

    def next_blk(i):
        return jnp.minimum((i + 1) * hb, n_halo - 1)

    return pl.pallas_call(
        functools.partial(_merge_kernel, tm=tm, tps=tps),
        grid=(rows // tm,),
        in_specs=[
            pl.BlockSpec((tm, A_WIDTH), lambda i: (i, _J_U)),
            pl.BlockSpec((tm, A_WIDTH), lambda i: (i, _J_V)),
            pl.BlockSpec((tm, B_WIDTH), lambda i: (i, 0)),
            pl.BlockSpec((tm, d), lambda i: (i, gate_col)),
            pl.BlockSpec((tm, d), lambda i: (i, gate_col + 1)),
            pl.BlockSpec((tm, d), lambda i: (i, gate_col + 2)),
            pl.BlockSpec((tm, C_WIDTH), lambda i: (i, _J_CIN)),
            pl.BlockSpec((tm, C_WIDTH), lambda i: (i, _J_CIN + 1)),
            pl.BlockSpec((tm, C_WIDTH), lambda i: (i, _J_CIN + 2)),
            pl.BlockSpec((HALO, C_WIDTH), lambda i: (prev_blk(i), _J_CIN + 1)),
            pl.BlockSpec((HALO, C_WIDTH), lambda i: (prev_blk(i), _J_CIN + 2)),
            pl.BlockSpec((HALO, C_WIDTH), lambda i: (next_blk(i), _J_CIN + 1)),
            pl.BlockSpec((HALO, C_WIDTH), lambda i: (next_blk(i), _J_CIN + 2)),
            pl.BlockSpec((tm, d), lambda i: (i, 0)),
            pl.BlockSpec((None, 1, d), lambda i: (bidx(i), 0, 0)),
            _resident((A_GROUPS, CHUNK, CHUNK)),
            _resident((CHUNK, A_WIDTH)),
            _resident((3, C_WIDTH)),
            _resident((A_WIDTH, d), layer),
            _resident((B_WIDTH, d), layer),
            _resident((C_WIDTH, d), layer),
            _resident((d, d), layer),
        ],
        out_specs=pl.BlockSpec((tm, d), lambda i: (i, 0)),
        out_shape=jax.ShapeDtypeStruct((rows, d), F32),
        scratch_shapes=[pltpu.VMEM((tm, A_WIDTH), BF16), pltpu.VMEM((tm, C_WIDTH), BF16)],
        compiler_params=pltpu.CompilerParams(
            dimension_semantics=("arbitrary",), vmem_limit_bytes=VMEM_LIMIT),
        name="merge",
    )(z, z, yb, z, z, z, z, z, z, z, z, z, z, x2d, g1, sw, sb, cw, wa, wb, wc, wo)


def _ffn_kernel(x_ref, xp_ref, xn_ref, sh_ref, sc_ref, gt_ref, ng_ref,
                wu_ref, cw_ref, cb_ref, wd_ref, o_ref, h_ref, u_ref, a_ref, *, tm, tps):
    i = pl.program_id(0)

    def nm(x):
        return _norm_modulate(x, ng_ref[...], sh_ref[...], sc_ref[...])

    first = (i % tps) == 0
    last = (i % tps) == (tps - 1)
    h_ref[HALO:HALO + tm, :] = nm(x_ref[...]).astype(BF16)
    h_ref[0:HALO, :] = jnp.where(first, 0.0, nm(xp_ref[...])).astype(BF16)
    h_ref[HALO + tm:, :] = jnp.where(last, 0.0, nm(xn_ref[...])).astype(BF16)

    def conv(slot, cols, scale=1.0):
        cw = cw_ref[:, cols] * scale
        return (u_ref[slot, HALO - 1:HALO - 1 + tm, :] * cw[0:1] + u_ref[slot, HALO:HALO + tm, :] * cw[1:2]
                + u_ref[slot, HALO + 1:HALO + 1 + tm, :] * cw[2:3] + cb_ref[:, cols] * scale)

    for c in range(D_FF // FF_TN):
        g_cols = slice(c * FF_TN, (c + 1) * FF_TN)
        v_cols = slice(D_FF + c * FF_TN, D_FF + (c + 1) * FF_TN)
        g_slot = 2 * (c % 2)
        v_slot = g_slot + 1
        h = h_ref[...]
        u_ref[g_slot] = jnp.dot(h, wu_ref[:, g_cols], preferred_element_type=F32)
        u_ref[v_slot] = jnp.dot(h, wu_ref[:, v_cols], preferred_element_type=F32)
        half_g = conv(g_slot, g_cols, 0.5)
        val = conv(v_slot, v_cols)
        a_ref[:, g_cols] = (half_g * (jnp.tanh(half_g) + 1.0) * val).astype(BF16)

    out = jnp.dot(a_ref[...], wd_ref[...], preferred_element_type=F32)
    o_ref[...] = x_ref[...] + gt_ref[...] * out


def _ffn(x2d, shift, scale, gate, norm_g, w_up, cw, cb, w_down, *, tm, seq_len, layer):
    rows, d = x2d.shape
    tps = seq_len // tm
    nb = shift.shape[0]
    hb = tm // HALO
    n_halo = rows // HALO

    def bidx(i):
        return (i // tps) % nb

    vec = pl.BlockSpec((None, 1, d), lambda i: (bidx(i), 0, 0))
    return pl.pallas_call(
        functools.partial(_ffn_kernel, tm=tm, tps=tps),
        grid=(rows // tm,),
        in_specs=[
            pl.BlockSpec((tm, d), lambda i: (i, 0)),
            pl.BlockSpec((HALO, d), lambda i: (jnp.maximum(i * hb - 1, 0), 0)),
            pl.BlockSpec((HALO, d), lambda i: (jnp.minimum((i + 1) * hb, n_halo - 1), 0)),
            vec, vec, vec,
            _resident((1, d)),
            _resident((d, 2 * D_FF), layer),
            _resident((3, 2 * D_FF)),
            _resident((1, 2 * D_FF)),
            _resident((D_FF, d), layer),
        ],
        out_specs=pl.BlockSpec((tm, d), lambda i: (i, 0)),
        out_shape=jax.ShapeDtypeStruct((rows, d), F32),
        scratch_shapes=[
            pltpu.VMEM((tm + 2 * HALO, d), BF16),
            pltpu.VMEM((4, tm + 2 * HALO, FF_TN), F32),
            pltpu.VMEM((tm, D_FF), BF16),
        ],
        compiler_params=pltpu.CompilerParams(
            dimension_semantics=("arbitrary",), vmem_limit_bytes=VMEM_LIMIT),
        name="ffn",
    )(x2d, x2d, x2d, shift, scale, gate, norm_g, w_up, cw, cb, w_down)


def kernel(x, c, ctx, c_ctx, ada_w, ada_b, norm1_g, norm2_g, w_in, sgu_ln_g, sgu_w, sgu_b, q_norm_g, k_norm_g, lam_q1, lam_k1, lam_q2, lam_k2, subln_g, conv_w, w_br_a, w_br_b, w_br_c, w_out, ffn_up, ffn_conv_w, ffn_conv_b, ffn_down):
    batch, seq, d = x.shape
    ctx_len = ctx.shape[1]
    depth = ada_w.shape[0]
    assert d == D_MODEL and seq % GRID_W == 0

    tm_lat = min(512, seq)
    tm_ffn = min(1024, seq)
    tm_mrg = min(512, seq)
    tm_ctx = ctx_len
    tq_lat = min(2048, seq)
    tk = 512
    for tile in (tm_lat, tm_ffn, tm_mrg, tq_lat):
        assert seq % tile == 0
    assert ctx_len % MXU_DIM == 0 and ctx_len % HALO == 0 and seq % tk == 0

    w_in_b = _z_weights(w_in)
    wv_b = w_in[:, :, _REF_B_V:_REF_B_V + B_WIDTH].astype(BF16)
    wa_b, wb_b, wc_b, wo_b = (w.astype(BF16) for w in (w_br_a, w_br_b, w_br_c, w_out))
    up_b = ffn_up.astype(BF16)
    down_b = ffn_down.astype(BF16)
    sw_b = sgu_w.astype(BF16)
    sgu_bias = jnp.repeat(jnp.swapaxes(sgu_b, 1, 2), LANES, axis=2)
    gmat = jnp.asarray(np.kron(np.eye(MXU_DIM // B_QK_DIM), np.full((B_QK_DIM, B_QK_DIM), 1.0 / B_QK_DIM)), BF16)
    cos, sin = _rope_tables(seq)
    cos_ctx = jnp.ones((tm_ctx, LANES), F32)
    sin_ctx = jnp.zeros((tm_ctx, LANES), F32)
    lam_vecs = jnp.stack([lam_q1, lam_k1, lam_q2, lam_k2], axis=1).astype(F32)
    score_cap = (1.02 * B_QK_DIM * QK_SCALE_LOG2E
                 * jnp.max(jnp.abs(q_norm_g), axis=1) * jnp.max(jnp.abs(k_norm_g), axis=1)).astype(F32)

    pad = (-(batch + 1)) % 8
    cc = jnp.concatenate([c, c_ctx[None, :], jnp.zeros((pad, d), F32)], axis=0)
    mod = _modulation(cc, ada_w, ada_b)
    mod = mod.reshape(depth, cc.shape[0], 6, 1, d)

    xs = x.reshape(batch * seq, d)
    cs = ctx.reshape(batch * ctx_len, d)

    for l in range(depth):
        last = l == depth - 1
        lam_init = 0.8 - 0.6 * math.exp(-0.3 * l)
        lat = [mod[l, :batch, k] for k in range(6)]
        cm = [mod[l, batch:batch + 1, k] for k in range(6)]
        n1 = norm1_g[l].reshape(1, d)
        n2 = norm2_g[l].reshape(1, d)
        qg = jnp.tile(q_norm_g[l], PROJ_TN // B_QK_DIM).reshape(1, PROJ_TN)
        kg = jnp.tile(k_norm_g[l], PROJ_TN // B_QK_DIM).reshape(1, PROJ_TN)
        lng = sgu_ln_g[l].reshape(1, A_WIDTH)
        subg = subln_g[l].reshape(B_V_DIM, 1)

        z_lat, vt_lat = _proj(xs, lat[0], lat[1], n1, w_in_b, wv_b, qg, kg, lng, cos, sin, gmat,
                              tm=tm_lat, seq_len=seq, use_rope=True, layer=l)
        z_ctx, vt_ctx = _proj(cs, cm[0], cm[1], n1, w_in_b, wv_b, qg, kg, lng, cos_ctx, sin_ctx, gmat,
                              tm=tm_ctx, seq_len=ctx_len, use_rope=False, layer=l, keys_values_only=last)

        yb_lat = _attention(z_lat, [(z_lat, vt_lat, seq), (z_ctx, vt_ctx, ctx_len)], score_cap, lam_vecs[l], subg,
                            batch=batch, seq_len=seq, tq=tq_lat, tk=tk, lam_init=lam_init, layer=l)
        merge_w = (sw_b[l], sgu_bias[l], conv_w[l], wa_b, wb_b, wc_b, wo_b)
        ffn_w = (up_b, ffn_conv_w[l], ffn_conv_b[l].reshape(1, 2 * D_FF), down_b)
        xs = _merge(z_lat, yb_lat, xs, lat[2], *merge_w, tm=tm_mrg, seq_len=seq, layer=l)
        xs = _ffn(xs, lat[3], lat[4], lat[5], n2, *ffn_w, tm=tm_ffn, seq_len=seq, layer=l)

        if not last:
            yb_ctx = _attention(z_ctx, [(z_ctx, vt_ctx, ctx_len)], score_cap, lam_vecs[l], subg,
                                batch=batch, seq_len=ctx_len, tq=tm_ctx, tk=tk, lam_init=lam_init, layer=l,
                                heads=B_HEADS)
            cs = _merge(z_ctx, yb_ctx, cs, cm[2], *merge_w, tm=tm_ctx, seq_len=ctx_len, layer=l)
            cs = _ffn(cs, cm[3], cm[4], cm[5], n2, *ffn_w, tm=tm_ctx, seq_len=ctx_len, layer=l)

    return xs.reshape(batch, seq, d)
```
